```python
import jax, jax.numpy as jnp
from jax import lax
import numpy as np

D_MODEL = 1024
BATCH = 2
SEQ = 8192
DEPTH = 2

GRID_W = 64
CTX_LEN = 256
N_MIXERS = 2
HG_HEADS = 8
HG_HEAD_DIM = D_MODEL // HG_HEADS
HG_CHUNK = 64
SC_WIDTH = 3
N_EXPERTS = 16
EC_CAPACITY_FACTOR = 2
D_EXPERT = 2 * D_MODEL
N_ADA = 6
N_HGRN_LAYERS = (DEPTH + 1) // 2
N_CONV_LAYERS = DEPTH // 2
LAST_CTX_READER = N_MIXERS * ((DEPTH - 1) // N_MIXERS)
EPS = 1e-6
POS_TEMP = 10000.0

kernel_name = "hybrid_hgrn2_shortconv_ecmoe_dit"


def _rmsnorm(x, g):
    xf = x.astype(jnp.float32)
    y = xf * lax.rsqrt(jnp.mean(xf * xf, axis=-1, keepdims=True) + EPS)
    return (y * g.astype(jnp.float32)).astype(x.dtype)


def _modulate(x, g, shift, scale):
    return _rmsnorm(x, g) * (1 + scale) + shift


def _grid_sincos(n_tokens, dtype):
    rows = n_tokens // GRID_W
    row = jnp.repeat(jnp.arange(rows), GRID_W).astype(jnp.float32)
    col = jnp.tile(jnp.arange(GRID_W), rows).astype(jnp.float32)
    n_freq = D_MODEL // 4
    omega = POS_TEMP ** (-jnp.arange(n_freq, dtype=jnp.float32) / n_freq)
    def emb(p):
        a = p[:, None] * omega[None, :]
        return jnp.concatenate([jnp.sin(a), jnp.cos(a)], axis=-1)
    return jnp.concatenate([emb(row), emb(col)], axis=-1).astype(dtype)


def _gla_chunks(q, k, v, logf, s0):
    b_, h_, t_, _ = q.shape
    n = t_ // HG_CHUNK
    def to_chunks(a):
        return jnp.moveaxis(a.reshape(b_, h_, n, HG_CHUNK, a.shape[-1]), 2, 0)
    incl = jnp.tril(jnp.ones((HG_CHUNK, HG_CHUNK), dtype=bool))[:, :, None]
    def step(s, inp):
        qc, kc, vc, lf = inp
        cum = jnp.cumsum(lf.astype(jnp.float32), axis=2)
        diff = cum[:, :, :, None, :] - cum[:, :, None, :, :]
        decay = jnp.exp(jnp.where(incl, diff, -jnp.inf))
        scores = jnp.einsum('bhtk,bhsk,bhtsk->bhts', qc, kc, decay)
        o = (jnp.einsum('bhts,bhsv->bhtv', scores, vc)
             + jnp.einsum('bhtk,bhkv->bhtv', qc * jnp.exp(cum), s))
        last = cum[:, :, -1, :]
        s = (jnp.exp(last)[..., None] * s
             + jnp.einsum('bhsk,bhsv->bhkv', kc * jnp.exp(last[:, :, None, :] - cum), vc))
        return s, o
    s_fin, o = lax.scan(step, s0, (to_chunks(q), to_chunks(k), to_chunks(v), to_chunks(logf)))
    o = jnp.moveaxis(o, 0, 2).reshape(b_, h_, t_, -1)
    return o, s_fin


def _hgrn_heads(h, w_in, lb, s0_fwd, s0_bwd):
    b_, t_, _ = h.shape
    q, i_in, z_fwd, z_bwd, g = jnp.split(h @ w_in, 5, axis=-1)
    def heads(a):
        return a.reshape(b_, t_, HG_HEADS, HG_HEAD_DIM).transpose(0, 2, 1, 3)
    q = heads(q) * HG_HEAD_DIM ** -0.5
    v = heads(i_in)
    lbh = lb.reshape(HG_HEADS, 1, HG_HEAD_DIM)
    def forget(z):
        f = lbh + (1 - lbh) * jax.nn.sigmoid(heads(z).astype(jnp.float32))
        return jnp.log(f), 1.0 - f
    logf_f, k_f = forget(z_fwd)
    logf_b, k_b = forget(z_bwd)
    o_f, s_f = _gla_chunks(q, k_f, v, logf_f, s0_fwd)
    flip = lambda a: jnp.flip(a, axis=2)
    o_b, s_b = _gla_chunks(flip(q), flip(k_b), flip(v), flip(logf_b), s0_bwd)
    return o_f + flip(o_b), g, s_f, s_b


def _hgrn_out(o, g, norm_g, w_out):
    b_, h_, t_, dv = o.shape
    o = _rmsnorm(o, norm_g)
    o = o.transpose(0, 2, 1, 3).reshape(b_, t_, h_ * dv).astype(g.dtype) * jax.nn.silu(g)
    return o @ w_out


def _short_conv(h, w_in, w_conv, w_out):
    b_gate, c_gate, u = jnp.split(h @ w_in, 3, axis=-1)
    u = c_gate * u
    y = lax.conv_general_dilated(
        u, w_conv.astype(u.dtype)[:, None, :], window_strides=(1,),
        padding=((SC_WIDTH // 2, SC_WIDTH // 2),),
        dimension_numbers=('NWC', 'WIO', 'NWC'), feature_group_count=D_MODEL)
    return (b_gate * y) @ w_out


def _ec_moe(h, w_router, w_gate, w_up, w_down):
    b_, t_, _ = h.shape
    cap = EC_CAPACITY_FACTOR * t_ // N_EXPERTS
    affinity = jax.nn.softmax((h @ w_router).astype(jnp.float32), axis=-1)
    gate, idx = lax.top_k(jnp.swapaxes(affinity, 1, 2), cap)
    bidx = jnp.arange(b_)[:, None, None]
    xg = h[bidx, idx]
    a = jnp.einsum('becd,edf->becf', xg, w_gate)
    u = jnp.einsum('becd,edf->becf', xg, w_up)
    y = jnp.einsum('becf,efd->becd', jax.nn.silu(a) * u, w_down)
    y = y * gate[..., None].astype(y.dtype)
    return jnp.zeros_like(h).at[bidx, idx].add(y)


def setup_inputs(seed: int = 0) -> dict:
    key = jax.random.key(seed)
    ks = jax.random.split(key, 22)
    f32 = jnp.float32
    def nrm(k, shape, fan_in):
        return jax.random.normal(k, shape, f32) * fan_in ** -0.5
    def gain(k, shape):
        return 1.0 + 0.02 * jax.random.normal(k, shape, f32)
    return {
        "x": jax.random.normal(ks[0], (BATCH, SEQ, D_MODEL), f32),
        "c": jax.random.normal(ks[1], (BATCH, D_MODEL), f32),
        "ctx": jax.random.normal(ks[2], (BATCH, CTX_LEN, D_MODEL), f32),
        "c_ctx": jax.random.normal(ks[3], (D_MODEL,), f32),
        "ada_w": 0.5 * nrm(ks[4], (DEPTH, D_MODEL, N_ADA * D_MODEL), D_MODEL),
        "ada_b": 0.02 * jax.random.normal(ks[5], (DEPTH, N_ADA * D_MODEL), f32),
        "norm_mix": gain(ks[6], (DEPTH, D_MODEL)),
        "norm_ffn": gain(ks[7], (DEPTH, D_MODEL)),
        "norm_final": gain(ks[8], (D_MODEL,)),
        "hg_w_in": nrm(ks[9], (N_HGRN_LAYERS, D_MODEL, 5 * D_MODEL), D_MODEL),
        "hg_lb_logits": 0.1 * jax.random.normal(ks[10], (DEPTH + 1, D_MODEL), f32),
        "hg_norm": gain(ks[11], (N_HGRN_LAYERS, HG_HEAD_DIM)),
        "hg_w_out": nrm(ks[12], (N_HGRN_LAYERS, D_MODEL, D_MODEL), D_MODEL),
        "sc_w_in": nrm(ks[13], (N_CONV_LAYERS, D_MODEL, 3 * D_MODEL), D_MODEL),
        "sc_conv": nrm(ks[14], (N_CONV_LAYERS, SC_WIDTH, D_MODEL), SC_WIDTH),
        "sc_w_out": nrm(ks[15], (N_CONV_LAYERS, D_MODEL, D_MODEL), D_MODEL),
        "moe_router": nrm(ks[16], (DEPTH, D_MODEL, N_EXPERTS), D_MODEL),
        "moe_w_gate": nrm(ks[17], (DEPTH, N_EXPERTS, D_MODEL, D_EXPERT), D_MODEL),
        "moe_w_up": nrm(ks[18], (DEPTH, N_EXPERTS, D_MODEL, D_EXPERT), D_MODEL),
        "moe_w_down": nrm(ks[19], (DEPTH, N_EXPERTS, D_EXPERT, D_MODEL), D_EXPERT),
    }


def reference(x, c, ctx, c_ctx, ada_w, ada_b, norm_mix, norm_ffn, norm_final,
              hg_w_in, hg_lb_logits, hg_norm, hg_w_out,
              sc_w_in, sc_conv, sc_w_out,
              moe_router, moe_w_gate, moe_w_up, moe_w_down):
    b_, t_, _ = x.shape
    x = x + _grid_sincos(t_, x.dtype)[None]
    xc = ctx
    lower_bounds = jnp.cumsum(jax.nn.softmax(hg_lb_logits.astype(jnp.float32), axis=0), axis=0)
    silu_c = jax.nn.silu(c)
    silu_cc = jax.nn.silu(c_ctx)
    for i in range(DEPTH):
        j = i // N_MIXERS
        ctx_live = i < LAST_CTX_READER
        sh1, sc1, g1, sh2, sc2, g2 = jnp.split((silu_c @ ada_w[i] + ada_b[i])[:, None, :], N_ADA, axis=-1)
        csh1, csc1, cg1, csh2, csc2, cg2 = jnp.split(silu_cc @ ada_w[i] + ada_b[i], N_ADA, axis=-1)
        h = _modulate(x, norm_mix[i], sh1, sc1)
        if i % N_MIXERS == 0:
            hc = _modulate(xc, norm_mix[i], csh1, csc1)
            zeros = jnp.zeros((b_, HG_HEADS, HG_HEAD_DIM, HG_HEAD_DIM), jnp.float32)
            oc, gc, s_f, s_b = _hgrn_heads(hc, hg_w_in[j], lower_bounds[i], zeros, zeros)
            o, g, _, _ = _hgrn_heads(h, hg_w_in[j], lower_bounds[i], s_f, s_b)
            x = x + g1 * _hgrn_out(o, g, hg_norm[j], hg_w_out[j])
            if ctx_live:
                xc = xc + cg1 * _hgrn_out(oc, gc, hg_norm[j], hg_w_out[j])
        else:
            x = x + g1 * _short_conv(h, sc_w_in[j], sc_conv[j], sc_w_out[j])
            if ctx_live:
                hc = _modulate(xc, norm_mix[i], csh1, csc1)
                xc = xc + cg1 * _short_conv(hc, sc_w_in[j], sc_conv[j], sc_w_out[j])
        h = _modulate(x, norm_ffn[i], sh2, sc2)
        x = x + g2 * _ec_moe(h, moe_router[i], moe_w_gate[i], moe_w_up[i], moe_w_down[i])
        if ctx_live:
            hc = _modulate(xc, norm_ffn[i], csh2, csc2)
            xc = xc + cg2 * _ec_moe(hc, moe_router[i], moe_w_gate[i], moe_w_up[i], moe_w_down[i])
    return _rmsnorm(x, norm_final)
```

```python
import functools
import math

import numpy as np
import jax
import jax.numpy as jnp
from jax import lax
from jax.experimental import pallas as pl
from jax.experimental.pallas import tpu as pltpu

F32 = jnp.float32
BF16 = jnp.bfloat16

EPS = 1e-6
POS_TEMP = 10000.0
GRID_W = 64
HEAD_DIM = 128
EC_CAPACITY_FACTOR = 2
GLA_CHUNK = 256
GLA_LEVELS = 8
ROW_TILE = 256
TOKEN_BLOCK = 256
SLOT_WINDOW = 64
SLOT_ALIGN = 16
FFN_TILE = 512
FFN_ROWS = 512
V7X_VMEM_LIMIT = 56 * 1024 * 1024

NT_DIMS = (((1,), (1,)), ((), ()))
TN_DIMS = (((0,), (0,)), ((), ()))


def _params(*sem, vmem=None):
    return pltpu.CompilerParams(dimension_semantics=sem, vmem_limit_bytes=vmem)


def _split3(x):
    hi = x.astype(BF16)
    r = x - hi.astype(F32)
    mid = r.astype(BF16)
    lo = (r - mid.astype(F32)).astype(BF16)
    return hi, mid, lo


def _dot_f32(a, b, dims):
    a0, a1, a2 = _split3(a)
    b0, b1, b2 = _split3(b)
    d = lambda x, y: lax.dot_general(x, y, dims, preferred_element_type=F32)
    return ((d(a0, b2) + d(a2, b0) + d(a1, b1)) + (d(a0, b1) + d(a1, b0))) + d(a0, b0)


def _sigmoid(x):
    return 1.0 / (1.0 + jnp.exp(-x))


def _silu(x):
    return x * _sigmoid(x)


def _modulate(x, gain, shift, scale):
    y = x * lax.rsqrt(jnp.mean(x * x, axis=-1, keepdims=True) + EPS)
    return (y * gain) * (1.0 + scale) + shift


def _ada_kernel(cond_ref, w_ref, b_ref, out_ref):
    s = _silu(cond_ref[...])
    out_ref[0, 0] = _dot_f32(s, w_ref[0], (((1,), (0,)), ((), ()))) + b_ref[0, 0]


def _ada_vectors(cond, ada_w, ada_b, n_ada):
    depth, d_, _ = ada_w.shape
    return pl.pallas_call(
        _ada_kernel,
        grid=(depth, n_ada),
        in_specs=[pl.BlockSpec((8, d_), lambda i, j: (0, 0)),
                  pl.BlockSpec((1, d_, d_), lambda i, j: (i, 0, j)),
                  pl.BlockSpec((1, 1, 1, d_), lambda i, j: (i, j, 0, 0))],
        out_specs=pl.BlockSpec((1, 1, 8, d_), lambda i, j: (i, j, 0, 0)),
        out_shape=jax.ShapeDtypeStruct((depth, n_ada, 8, d_), F32),
        compiler_params=_params("arbitrary", "arbitrary"),
        name="ada_vectors",
    )(cond, ada_w, ada_b.reshape(depth, n_ada, 1, d_))


def _tables_kernel(lb_logits_ref, rowtab_ref, coltab_ref, lb_ref, *, n_freq):
    def table(n_pos):
        p = lax.broadcasted_iota(jnp.int32, (n_pos, n_freq), 0).astype(F32)
        j = lax.broadcasted_iota(jnp.int32, (n_pos, n_freq), 1).astype(F32)
        omega = jnp.exp(j * (-math.log(POS_TEMP) / n_freq))
        ang = p * omega
        return jnp.concatenate([jnp.sin(ang), jnp.cos(ang)], axis=-1)

    rowtab_ref[...] = table(rowtab_ref.shape[0])
    coltab_ref[...] = table(coltab_ref.shape[0])
    logits = lb_logits_ref[...]
    e = jnp.exp(logits - jnp.max(logits, axis=0, keepdims=True))
    sm = e / jnp.sum(e, axis=0, keepdims=True)
    acc = sm[0:1]
    lb_ref[0:1] = acc
    for i in range(1, lb_ref.shape[0]):
        acc = acc + sm[i:i + 1]
        lb_ref[i:i + 1] = acc


def _tables(lb_logits, n_tokens):
    n_lb, d_ = lb_logits.shape
    n_freq = d_ // 4
    rows = n_tokens // GRID_W
    return pl.pallas_call(
        functools.partial(_tables_kernel, n_freq=n_freq),
        out_shape=[jax.ShapeDtypeStruct((rows, 2 * n_freq), F32),
                   jax.ShapeDtypeStruct((GRID_W, 2 * n_freq), F32),
                   jax.ShapeDtypeStruct((n_lb, d_), F32)],
        name="pos_tables",
    )(lb_logits)


def _hgrn_in_kernel(*refs, with_pos, with_gate):
    it = iter(refs)
    x_ref = next(it)
    if with_pos:
        rowtab_ref, coltab_ref = next(it), next(it)
    gain_ref, shift_ref, scale_ref, lb_ref, w_ref = next(it), next(it), next(it), next(it), next(it)
    if with_pos:
        x0_ref = next(it)
    q_ref, v_ref, lff_ref, lfb_ref = next(it), next(it), next(it), next(it)
    g_ref = next(it) if with_gate else None

    x = x_ref[0]
    d_ = x.shape[-1]
    if with_pos:
        rt = rowtab_ref[0]
        ct = coltab_ref[...]
        n_rows = rt.shape[0]
        pos_row = jnp.concatenate([jnp.broadcast_to(rt[r:r + 1], (GRID_W, rt.shape[1])) for r in range(n_rows)], axis=0)
        pos_col = jnp.concatenate([ct] * n_rows, axis=0)
        x = x + jnp.concatenate([pos_row, pos_col], axis=-1)
        x0_ref[0] = x
    h = _modulate(x, gain_ref[...], shift_ref[0], scale_ref[0]).astype(BF16)
    lb = lb_ref[...]

    def part(p):
        return jnp.dot(h, w_ref[:, p * d_:(p + 1) * d_], preferred_element_type=F32)

    q_ref[0] = (part(0) * HEAD_DIM ** -0.5).astype(BF16)
    v_ref[0] = part(1).astype(BF16)
    lff_ref[0] = jnp.log(lb + (1.0 - lb) * _sigmoid(part(2)))
    lfb_ref[0] = jnp.log(lb + (1.0 - lb) * _sigmoid(part(3)))
    if with_gate:
        g_ref[0] = part(4).astype(BF16)


def _hgrn_in(x, tabs, gain, shift, scale, lb, w_in, *, with_gate):
    b_, t_, d_ = x.shape
    tm = ROW_TILE
    with_pos = tabs is not None
    per_sample = lambda a: pl.BlockSpec((1, 1, d_), (lambda b, i: (b, 0, 0)) if a.shape[0] > 1 else (lambda b, i: (0, 0, 0)))
    row = pl.BlockSpec((1, tm, d_), lambda b, i: (b, i, 0))
    vec = pl.BlockSpec((1, d_), lambda b, i: (0, 0))
    args, in_specs = [x], [row]
    if with_pos:
        rowtab, coltab = tabs
        rows_per_tile = tm // GRID_W
        args += [rowtab.reshape(rowtab.shape[0] // rows_per_tile, rows_per_tile, rowtab.shape[1]), coltab]
        in_specs += [pl.BlockSpec((1, rows_per_tile, rowtab.shape[1]), lambda b, i: (i, 0, 0)),
                     pl.BlockSpec(coltab.shape, lambda b, i: (0, 0))]
    args += [gain, shift, scale, lb, w_in]
    in_specs += [vec, per_sample(shift), per_sample(scale), vec, pl.BlockSpec(w_in.shape, lambda b, i: (0, 0))]
    out_shape, out_specs = [], []
    if with_pos:
        out_shape.append(jax.ShapeDtypeStruct((b_, t_, d_), F32))
        out_specs.append(row)
    out_shape += [jax.ShapeDtypeStruct((b_, t_, d_), BF16)] * 2 + [jax.ShapeDtypeStruct((b_, t_, d_), F32)] * 2
    out_specs += [row] * 4
    if with_gate:
        out_shape.append(jax.ShapeDtypeStruct((b_, t_, d_), BF16))
        out_specs.append(row)
    return pl.pallas_call(
        functools.partial(_hgrn_in_kernel, with_pos=with_pos, with_gate=with_gate),
        grid=(b_, t_ // tm),
        in_specs=in_specs, out_specs=out_specs, out_shape=out_shape,
        compiler_params=_params("arbitrary", "arbitrary", vmem=V7X_VMEM_LIMIT),
        name="hgrn_in_latent" if with_pos else "hgrn_in_context",
    )(*args)


def _gla_consts(reverse):
    c = GLA_CHUNK
    idx = np.arange(c)
    rank = (c - 1 - idx) if reverse else idx
    tri = (rank[None, :] <= rank[:, None]).astype(np.float32)
    lvl = np.full((c, c), -1, np.int32)
    rt, rs = rank[:, None], rank[None, :]
    lvl[rt == rs] = 0
    sgn = np.zeros((GLA_LEVELS, c, HEAD_DIM), np.float32)
    for level in range(1, GLA_LEVELS + 1):
        blk, half = 1 << level, 1 << (level - 1)
        lvl[(rt // blk == rs // blk) & ((rt % blk) >= half) & ((rs % blk) < half)] = level
        sgn[level - 1] = np.where((rank % blk) >= half, 1.0, -1.0)[:, None]
    return (jnp.asarray(np.concatenate([tri, tri, tri], axis=1), BF16), jnp.asarray(lvl),
            jnp.asarray(sgn.reshape(GLA_LEVELS * c, HEAD_DIM)))


def _boundary_rows(level, reverse):
    c = GLA_CHUNK
    blk, half = 1 << level, 1 << (level - 1)
    rows = []
    for i in range(c):
        rank = (c - 1 - i) if reverse else i
        brank = (rank // blk) * blk + half - 1
        rows.append((c - 1 - brank) if reverse else brank)
    return rows


def _gla_chunk(q_ref, v_ref, lf_ref, st_ref, cum_ref, tri3_ref, lvl_ref, sgn_ref, o_ref, reverse):
    c = GLA_CHUNK
    lf = lf_ref[0]
    q = q_ref[0].astype(F32)
    v = v_ref[0]
    kk = 1.0 - jnp.exp(lf)
    parts = jnp.concatenate(_split3(lf), axis=0)
    cum = jnp.dot(tri3_ref[...], parts, preferred_element_type=F32)
    cum_ref[...] = cum
    lvl = lvl_ref[...]
    s0 = lax.dot_general(q.astype(BF16), kk.astype(BF16), NT_DIMS, preferred_element_type=F32)
    scores = jnp.where(lvl == 0, s0, 0.0)
    row8 = lax.broadcasted_iota(jnp.int32, (8, HEAD_DIM), 0)
    for level in range(1, GLA_LEVELS + 1):
        sgn = sgn_ref[(level - 1) * c:level * c, :]
        later = sgn > 0.0
        if level == 1:
            g = jnp.where(later, lf, 0.0)
        else:
            brow = _boundary_rows(level, reverse)
            pieces = []
            for g8 in range(c // 8):
                rows = brow[g8 * 8:(g8 + 1) * 8]
                first = jnp.broadcast_to(cum_ref[rows[0]:rows[0] + 1, :], (8, HEAD_DIM))
                if level == 2:
                    second = jnp.broadcast_to(cum_ref[rows[7]:rows[7] + 1, :], (8, HEAD_DIM))
                    pieces.append(jnp.where(row8 < 4, first, second))
                else:
                    pieces.append(first)
            g = (cum - jnp.concatenate(pieces, axis=0)) * sgn
        xe = (jnp.where(later, q, kk) * jnp.exp(g)).astype(BF16)
        sl = lax.dot_general(xe, xe, NT_DIMS, preferred_element_type=F32)
        scores = jnp.where(lvl == level, sl, scores)
    st = st_ref[...]
    qe = (q * jnp.exp(cum)).astype(BF16)
    o = jnp.dot(scores.astype(BF16), v, preferred_element_type=F32)
    o_ref[0] = o + lax.dot_general(qe, st.astype(BF16), NT_DIMS, preferred_element_type=F32)
    last_row = 0 if reverse else c - 1
    last = cum_ref[last_row:last_row + 1, :]
    ke = (kk * jnp.exp(last - cum)).astype(BF16)
    st_ref[...] = st * jnp.exp(last) + lax.dot_general(v, ke, TN_DIMS, preferred_element_type=F32)


def _gla_kernel(qf_ref, vf_ref, lff_ref, qb_ref, vb_ref, lfb_ref, s0f_ref, s0b_ref,
                tri3f_ref, lvlf_ref, sgnf_ref, tri3b_ref, lvlb_ref, sgnb_ref,
                of_ref, ob_ref, sff_ref, sfb_ref, stf_ref, stb_ref, cum_ref):
    j = pl.program_id(2)

    @pl.when(j == 0)
    def _():
        stf_ref[...] = s0f_ref[0, 0]
        stb_ref[...] = s0b_ref[0, 0]

    _gla_chunk(qf_ref, vf_ref, lff_ref, stf_ref, cum_ref, tri3f_ref, lvlf_ref, sgnf_ref, of_ref, False)
    _gla_chunk(qb_ref, vb_ref, lfb_ref, stb_ref, cum_ref, tri3b_ref, lvlb_ref, sgnb_ref, ob_ref, True)

    @pl.when(j == pl.num_programs(2) - 1)
    def _():
        sff_ref[0, 0] = stf_ref[...]
        sfb_ref[0, 0] = stb_ref[...]


def _gla_bidir(q, v, lf_f, lf_b, s0f, s0b):
    b_, t_, d_ = q.shape
    h_ = d_ // HEAD_DIM
    c = GLA_CHUNK
    n = t_ // c
    fwd = lambda b, h, j: (b, j, h)
    bwd = lambda b, h, j: (b, n - 1 - j, h)
    st = lambda b, h, j: (b, h, 0, 0)
    const = lambda b, h, j: (0, 0)
    blk = lambda im: pl.BlockSpec((1, c, HEAD_DIM), im)
    st_spec = pl.BlockSpec((1, 1, HEAD_DIM, HEAD_DIM), st)
    cspecs = [pl.BlockSpec((c, 3 * c), const), pl.BlockSpec((c, c), const),
              pl.BlockSpec((GLA_LEVELS * c, HEAD_DIM), const)]
    return pl.pallas_call(
        _gla_kernel,
        grid=(b_, h_, n),
        in_specs=[blk(fwd), blk(fwd), blk(fwd), blk(bwd), blk(bwd), blk(bwd), st_spec, st_spec] + cspecs + cspecs,
        out_specs=[blk(fwd), blk(bwd), st_spec, st_spec],
        out_shape=[jax.ShapeDtypeStruct((b_, t_, d_), F32)] * 2
                  + [jax.ShapeDtypeStruct((b_, h_, HEAD_DIM, HEAD_DIM), F32)] * 2,
        scratch_shapes=[pltpu.VMEM((HEAD_DIM, HEAD_DIM), F32), pltpu.VMEM((HEAD_DIM, HEAD_DIM), F32),
                        pltpu.VMEM((c, HEAD_DIM), F32)],
        compiler_params=_params("arbitrary", "arbitrary", "arbitrary"),
        name="gla_bidir",
    )(q, v, lf_f, q, v, lf_b, s0f, s0b, *_gla_consts(False), *_gla_consts(True))


def _ffn_prologue(x, gain_ref, shift_ref, scale_ref, wr_ref, h_ref, aff_ref):
    hf = _modulate(x, gain_ref[...], shift_ref[0], scale_ref[0])
    h_ref[0] = hf.astype(BF16)
    logits = _dot_f32(wr_ref[...], hf, NT_DIMS)
    e = jnp.exp(logits - jnp.max(logits, axis=0, keepdims=True))
    aff_ref[0] = e / jnp.sum(e, axis=0, keepdims=True)


def _hgrn_out_kernel(of_ref, ob_ref, g_ref, x_ref, hnorm_ref, w_ref, gate_ref,
                     gain_ref, shift_ref, scale_ref, wr_ref, x1_ref, h_ref, aff_ref):
    o = of_ref[0] + ob_ref[0]
    hn = hnorm_ref[...]
    heads = []
    for h in range(o.shape[-1] // HEAD_DIM):
        oh = o[:, h * HEAD_DIM:(h + 1) * HEAD_DIM]
        heads.append(oh * lax.rsqrt(jnp.mean(oh * oh, axis=-1, keepdims=True) + EPS) * hn)
    y = jnp.concatenate(heads, axis=-1) * _silu(g_ref[0].astype(F32))
    y = jnp.dot(y.astype(BF16), w_ref[...], preferred_element_type=F32)
    x1 = x_ref[0] + gate_ref[0] * y
    x1_ref[0] = x1
    _ffn_prologue(x1, gain_ref, shift_ref, scale_ref, wr_ref, h_ref, aff_ref)


def _conv_out_kernel(cu_ref, bg_ref, prev_ref, next_ref, x_ref, wc_ref, w_ref, gate_ref,
                     gain_ref, shift_ref, scale_ref, wr_ref, x1_ref, h_ref, aff_ref):
    i = pl.program_id(1)
    cu = cu_ref[0].astype(F32)
    tm = cu.shape[0]
    rid = lax.broadcasted_iota(jnp.int32, cu.shape, 0)
    before = jnp.where(i == 0, 0.0, prev_ref[0, 7:8, :].astype(F32))
    after = jnp.where(i == pl.num_programs(1) - 1, 0.0, next_ref[0, 0:1, :].astype(F32))
    left = jnp.where(rid == 0, before, pltpu.roll(cu, 1, 0))
    right = jnp.where(rid == tm - 1, after, pltpu.roll(cu, tm - 1, 0))
    wc = wc_ref[...]
    y = left * wc[0:1] + cu * wc[1:2] + right * wc[2:3]
    y = (bg_ref[0].astype(F32) * y).astype(BF16)
    y = jnp.dot(y, w_ref[...], preferred_element_type=F32)
    x1 = x_ref[0] + gate_ref[0] * y
    x1_ref[0] = x1
    _ffn_prologue(x1, gain_ref, shift_ref, scale_ref, wr_ref, h_ref, aff_ref)


def _mixer_out(kernel, name, row_args, halo_args, x, small_args, gate, gain, shift, scale, w_router_t):
    b_, t_, d_ = x.shape
    e_ = w_router_t.shape[0]
    tm = ROW_TILE
    row = pl.BlockSpec((1, tm, d_), lambda b, i: (b, i, 0))
    per_sample = pl.BlockSpec((1, 1, d_), lambda b, i: (b, 0, 0))
    whole = lambda a: pl.BlockSpec(a.shape, lambda b, i: (0,) * a.ndim)
    n8 = t_ // 8
    halo_specs = [pl.BlockSpec((1, 8, d_), lambda b, i: (b, jnp.maximum(i * (tm // 8) - 1, 0), 0)),
                  pl.BlockSpec((1, 8, d_), lambda b, i: (b, jnp.minimum((i + 1) * (tm // 8), n8 - 1), 0))]
    return pl.pallas_call(
        kernel,
        grid=(b_, t_ // tm),
        in_specs=[row] * len(row_args) + halo_specs[:len(halo_args)] + [row]
                 + [whole(a) for a in small_args] + [per_sample, whole(gain), per_sample, per_sample, whole(w_router_t)],
        out_specs=[row, row, pl.BlockSpec((1, e_, tm), lambda b, i: (b, 0, i))],
        out_shape=[jax.ShapeDtypeStruct((b_, t_, d_), F32), jax.ShapeDtypeStruct((b_, t_, d_), BF16),
                   jax.ShapeDtypeStruct((b_, e_, t_), F32)],
        compiler_params=_params("arbitrary", "arbitrary", vmem=V7X_VMEM_LIMIT),
        name=name,
    )(*row_args, *halo_args, x, *small_args, gate, gain, shift, scale, w_router_t)


def _conv_in_kernel(x_ref, gain_ref, shift_ref, scale_ref, w_ref, bg_ref, cu_ref):
    x = x_ref[0]
    d_ = x.shape[-1]
    h = _modulate(x, gain_ref[...], shift_ref[0], scale_ref[0]).astype(BF16)
    part = lambda p: jnp.dot(h, w_ref[:, p * d_:(p + 1) * d_], preferred_element_type=F32)
    bg_ref[0] = part(0).astype(BF16)
    cu_ref[0] = (part(1) * part(2)).astype(BF16)


def _conv_in(x, gain, shift, scale, w_in):
    b_, t_, d_ = x.shape
    tm = ROW_TILE
    row = pl.BlockSpec((1, tm, d_), lambda b, i: (b, i, 0))
    per_sample = pl.BlockSpec((1, 1, d_), lambda b, i: (b, 0, 0))
    return pl.pallas_call(
        _conv_in_kernel,
        grid=(b_, t_ // tm),
        in_specs=[row, pl.BlockSpec((1, d_), lambda b, i: (0, 0)), per_sample, per_sample,
                  pl.BlockSpec(w_in.shape, lambda b, i: (0, 0))],
        out_specs=[row, row],
        out_shape=[jax.ShapeDtypeStruct((b_, t_, d_), BF16)] * 2,
        compiler_params=_params("arbitrary", "arbitrary", vmem=V7X_VMEM_LIMIT),
        name="conv_in",
    )(x, gain, shift, scale, w_in)


def _route_kernel(aff_ref, tri_ref, blockind_ref, slot_ref, base_ref, *, cap):
    aff = aff_ref[0]
    e_, t_ = aff.shape

    def as_float(word):
        return pltpu.bitcast(word, F32)

    def count_ge(th):
        return jnp.sum(jnp.where(aff >= th, 1.0, 0.0), axis=1, keepdims=True)

    def search(_, carry):
        lo, hi = carry
        mid = lo + ((hi - lo + 1) >> 1)
        ok = count_ge(as_float(mid)) >= cap
        return jnp.where(ok, mid, lo), jnp.where(ok, hi, mid - 1)

    lo0 = jnp.zeros((e_, 1), jnp.int32)
    hi0 = jnp.full((e_, 1), 0x7F7FFFFF, jnp.int32)
    kth, _ = lax.fori_loop(0, 32, search, (lo0, hi0))
    above = aff >= as_float(kth + 1)
    tied = jnp.logical_and(aff >= as_float(kth), jnp.logical_not(above))
    need = cap - jnp.sum(jnp.where(above, 1.0, 0.0), axis=1, keepdims=True)
    tri = tri_ref[...]
    tb = tri.shape[0]
    carry_t = jnp.zeros((e_, 1), F32)
    carry_s = jnp.zeros((e_, 1), F32)
    sel_blocks = []
    for j in range(t_ // tb):
        cols = slice(j * tb, (j + 1) * tb)
        tied_j = tied[:, cols]
        ct = jnp.dot(jnp.where(tied_j, 1.0, 0.0).astype(BF16), tri, preferred_element_type=F32) + carry_t
        carry_t = ct[:, tb - 1:tb]
        sel_j = jnp.where(above[:, cols], 1.0, jnp.where(tied_j & (ct <= need), 1.0, 0.0))
        cs = jnp.dot(sel_j.astype(BF16), tri, preferred_element_type=F32) + carry_s
        carry_s = cs[:, tb - 1:tb]
        slot_ref[0, :, cols] = jnp.where(sel_j > 0.0, cs - 1.0, -1.0).astype(jnp.int32)
        sel_blocks.append(sel_j.astype(BF16))
    sel = jnp.concatenate(sel_blocks, axis=1)
    base_ref[0] = jnp.dot(sel, blockind_ref[...], preferred_element_type=F32).astype(jnp.int32)


def _route(aff):
    b_, e_, t_ = aff.shape
    cap = EC_CAPACITY_FACTOR * t_ // e_
    tb = TOKEN_BLOCK
    nb = t_ // tb
    tri = jnp.asarray(np.triu(np.ones((tb, tb), np.float32)), BF16)
    starts = np.arange(128) * tb
    blockind = (np.arange(t_)[:, None] < starts[None, :]) & (np.arange(128)[None, :] <= nb)
    slot, base = pl.pallas_call(
        functools.partial(_route_kernel, cap=cap),
        grid=(b_,),
        in_specs=[pl.BlockSpec((1, e_, t_), lambda b: (b, 0, 0)),
                  pl.BlockSpec((tb, tb), lambda b: (0, 0)),
                  pl.BlockSpec((t_, 128), lambda b: (0, 0))],
        out_specs=[pl.BlockSpec((1, e_, t_), lambda b: (b, 0, 0)), pl.BlockSpec((1, e_, 128), lambda b: (b, 0, 0))],
        out_shape=[jax.ShapeDtypeStruct((b_, e_, t_), jnp.int32), jax.ShapeDtypeStruct((b_, e_, 128), jnp.int32)],
        compiler_params=_params("arbitrary"),
        name="route",
    )(aff, tri, jnp.asarray(blockind.astype(np.float32), BF16))
    return slot, base[:, :, :nb + 1].reshape(-1)


def _window_plan(tbl_ref, b, e, tb, ne, nb):
    idx = (b * ne + e) * (nb + 1) + tb
    base, end = tbl_ref[idx], tbl_ref[idx + 1]
    start = (base >> 4) << 4
    n_win = jnp.where(end > base, (end - start + SLOT_WINDOW - 1) >> 6, 0)
    return start, n_win


def _gather_kernel(tbl_ref, h_ref, slot_ref, xg_ref, *, ne, nb, cap):
    b, tb = pl.program_id(0), pl.program_id(2)

    @pl.when(tb == 0)
    def _():
        xg_ref[...] = jnp.zeros(xg_ref.shape, xg_ref.dtype)

    h = h_ref[0]
    n_tok = h.shape[0]
    for e in range(ne):
        srow = slot_ref[0, e:e + 1, :]
        start, n_win = _window_plan(tbl_ref, b, e, tb, ne, nb)

        def body(k, carry, e=e, srow=srow, start=start):
            lo = start + k * SLOT_WINDOW
            w0 = pl.multiple_of(jnp.minimum(lo, cap - SLOT_WINDOW), SLOT_ALIGN)
            ids = w0 + lax.broadcasted_iota(jnp.int32, (SLOT_WINDOW, n_tok), 0)
            onehot = jnp.where(ids >= lo, jnp.where(ids == srow, 1.0, 0.0), 0.0).astype(BF16)
            rows = jnp.dot(onehot, h, preferred_element_type=F32)
            win = xg_ref.at[0, e, pl.ds(w0, SLOT_WINDOW), :]
            win[...] = win[...] + rows.astype(BF16)
            return carry

        lax.fori_loop(0, n_win, body, 0)


def _gather(h, slot, tbl):
    b_, t_, d_ = h.shape
    e_ = slot.shape[1]
    cap = EC_CAPACITY_FACTOR * t_ // e_
    tb = TOKEN_BLOCK
    nb = t_ // tb
    dh = d_ // 2
    return pl.pallas_call(
        functools.partial(_gather_kernel, ne=e_, nb=nb, cap=cap),
        grid_spec=pltpu.PrefetchScalarGridSpec(
            num_scalar_prefetch=1,
            grid=(b_, 2, nb),
            in_specs=[pl.BlockSpec((1, tb, dh), lambda b, c, i, tbl: (b, i, c)),
                      pl.BlockSpec((1, e_, tb), lambda b, c, i, tbl: (b, 0, i))],
            out_specs=pl.BlockSpec((1, e_, cap, dh), lambda b, c, i, tbl: (b, 0, 0, c)),
        ),
        out_shape=jax.ShapeDtypeStruct((b_, e_, cap, d_), BF16),
        compiler_params=_params("arbitrary", "arbitrary", "arbitrary", vmem=V7X_VMEM_LIMIT),
        name="moe_gather",
    )(tbl, h, slot)


def _expert_kernel(xg_ref, wg_ref, wu_ref, wd_ref, y_ref, acc_ref):
    f = pl.program_id(1)
    wg = wg_ref[0].astype(BF16)
    wu = wu_ref[0].astype(BF16)
    wd = wd_ref[0].astype(BF16)
    n_b, _, cap, _ = xg_ref.shape
    for b in range(n_b):
        for r in range(cap // FFN_ROWS):
            rows = pl.ds(r * FFN_ROWS, FFN_ROWS)
            acc_rows = pl.ds((b * cap) + r * FFN_ROWS, FFN_ROWS)
            xr = xg_ref[b, 0, rows, :]
            a = jnp.dot(xr, wg, preferred_element_type=F32)
            u = jnp.dot(xr, wu, preferred_element_type=F32)
            part = jnp.dot((_silu(a) * u).astype(BF16), wd, preferred_element_type=F32)

            @pl.when(f == 0)
            def _():
                acc_ref[acc_rows, :] = part

            @pl.when(f > 0)
            def _():
                acc_ref[acc_rows, :] = acc_ref[acc_rows, :] + part

    @pl.when(f == pl.num_programs(1) - 1)
    def _():
        for b in range(n_b):
            y_ref[b, 0] = acc_ref[b * cap:(b + 1) * cap, :].astype(BF16)


def _experts(xg, w_gate, w_up, w_down):
    b_, e_, cap, d_ = xg.shape
    f_ = w_gate.shape[-1]
    ft = min(FFN_TILE, f_)
    return pl.pallas_call(
        _expert_kernel,
        grid=(e_, f_ // ft),
        in_specs=[pl.BlockSpec((b_, 1, cap, d_), lambda e, f: (0, e, 0, 0)),
                  pl.BlockSpec((1, d_, ft), lambda e, f: (e, 0, f)),
                  pl.BlockSpec((1, d_, ft), lambda e, f: (e, 0, f)),
                  pl.BlockSpec((1, ft, d_), lambda e, f: (e, f, 0))],
        out_specs=pl.BlockSpec((b_, 1, cap, d_), lambda e, f: (0, e, 0, 0)),
        out_shape=jax.ShapeDtypeStruct((b_, e_, cap, d_), BF16),
        scratch_shapes=[pltpu.VMEM((b_ * cap, d_), F32)],
        compiler_params=_params("arbitrary", "arbitrary", vmem=V7X_VMEM_LIMIT),
        name="moe_experts",
    )(xg, w_gate, w_up, w_down)


def _combine_kernel(tbl_ref, y_ref, slot_ref, aff_ref, x_ref, gate_ref, out_ref, acc_ref, *, ne, nb, cap):
    b, tb = pl.program_id(0), pl.program_id(2)
    acc_ref[...] = jnp.zeros(acc_ref.shape, F32)
    n_tok = acc_ref.shape[0]
    for e in range(ne):
        srow = slot_ref[0, e:e + 1, :]
        grow = aff_ref[0, e:e + 1, :]
        start, n_win = _window_plan(tbl_ref, b, e, tb, ne, nb)

        def body(k, carry, e=e, srow=srow, grow=grow, start=start):
            lo = start + k * SLOT_WINDOW
            w0 = pl.multiple_of(jnp.minimum(lo, cap - SLOT_WINDOW), SLOT_ALIGN)
            ids = w0 + lax.broadcasted_iota(jnp.int32, (SLOT_WINDOW, n_tok), 0)
            weights = jnp.where(ids >= lo, jnp.where(ids == srow, grow, 0.0), 0.0).astype(BF16)
            yw = y_ref[0, e, pl.ds(w0, SLOT_WINDOW), :]
            acc_ref[...] = acc_ref[...] + lax.dot_general(weights, yw, TN_DIMS, preferred_element_type=F32)
            return carry

        lax.fori_loop(0, n_win, body, 0)
    out_ref[0] = x_ref[0] + gate_ref[0] * acc_ref[...]


def _combine(y, slot, aff, tbl, x, gate):
    b_, t_, d_ = x.shape
    e_, cap = y.shape[1], y.shape[2]
    tb = TOKEN_BLOCK
    nb = t_ // tb
    dh = d_ // 2
    route_spec = pl.BlockSpec((1, e_, tb), lambda b, c, i, tbl: (b, 0, i))
    row = pl.BlockSpec((1, tb, dh), lambda b, c, i, tbl: (b, i, c))
    return pl.pallas_call(
        functools.partial(_combine_kernel, ne=e_, nb=nb, cap=cap),
        grid_spec=pltpu.PrefetchScalarGridSpec(
            num_scalar_prefetch=1,
            grid=(b_, 2, nb),
            in_specs=[pl.BlockSpec((1, e_, cap, dh), lambda b, c, i, tbl: (b, 0, 0, c)),
                      route_spec, route_spec, row,
                      pl.BlockSpec((1, 1, dh), lambda b, c, i, tbl: (b, 0, c))],
            out_specs=row,
            scratch_shapes=[pltpu.VMEM((tb, dh), F32)],
        ),
        out_shape=jax.ShapeDtypeStruct((b_, t_, d_), F32),
        compiler_params=_params("arbitrary", "arbitrary", "arbitrary", vmem=V7X_VMEM_LIMIT),
        name="moe_combine",
    )(tbl, y, slot, aff, x, gate)


def _moe(x, h, aff, gate, w_gate, w_up, w_down):
    slot, tbl = _route(aff)
    xg = _gather(h, slot, tbl)
    y = _experts(xg, w_gate, w_up, w_down)
    return _combine(y, slot, aff, tbl, x, gate)


def _final_norm_kernel(x_ref, gain_ref, out_ref):
    x = x_ref[0]
    out_ref[0] = x * lax.rsqrt(jnp.mean(x * x, axis=-1, keepdims=True) + EPS) * gain_ref[...]


def _final_norm(x, gain):
    b_, t_, d_ = x.shape
    tm = ROW_TILE
    row = pl.BlockSpec((1, tm, d_), lambda b, i: (b, i, 0))
    return pl.pallas_call(
        _final_norm_kernel,
        grid=(b_, t_ // tm),
        in_specs=[row, pl.BlockSpec((1, d_), lambda b, i: (0, 0))],
        out_specs=row,
        out_shape=jax.ShapeDtypeStruct((b_, t_, d_), F32),
        compiler_params=_params("arbitrary", "arbitrary"),
        name="final_norm",
    )(x, gain)


def kernel(x, c, ctx, c_ctx, ada_w, ada_b, norm_mix, norm_ffn, norm_final, hg_w_in, hg_lb_logits, hg_norm, hg_w_out, sc_w_in, sc_conv, sc_w_out, moe_router, moe_w_gate, moe_w_up, moe_w_down):
    b_, t_, d_ = x.shape
    depth = ada_w.shape[0]
    n_ada = ada_w.shape[-1] // d_
    n_heads = d_ // HEAD_DIM
    assert depth == 2 and n_ada == 6 and b_ + 1 <= 8
    assert t_ % GLA_CHUNK == 0 and ctx.shape[1] % GLA_CHUNK == 0 and ROW_TILE % GRID_W == 0

    cond = jnp.concatenate([c, c_ctx[None], jnp.zeros((8 - b_ - 1, d_), F32)], axis=0)
    mod = _ada_vectors(cond, ada_w, ada_b, n_ada)
    vec = lambda i, j: mod[i, j, :b_][:, None, :]
    cvec = lambda i, j: mod[i, j, b_][None, None, :]
    rowtab, coltab, lower = _tables(hg_lb_logits, t_)
    row_of = lambda a, i: a[i][None, :]
    router_t = lambda i: jnp.swapaxes(moe_router[i], 0, 1)

    w_in = hg_w_in[0].astype(BF16)
    lb0 = row_of(lower, 0)
    gain0 = row_of(norm_mix, 0)
    qc, vc, lfc_f, lfc_b = _hgrn_in(ctx, None, gain0, cvec(0, 0), cvec(0, 1), lb0, w_in, with_gate=False)
    zeros = jnp.zeros((b_, n_heads, HEAD_DIM, HEAD_DIM), F32)
    _, _, s_f, s_b = _gla_bidir(qc, vc, lfc_f, lfc_b, zeros, zeros)
    x0, q, v, lf_f, lf_b, g = _hgrn_in(x, (rowtab, coltab), gain0, vec(0, 0), vec(0, 1), lb0, w_in, with_gate=True)
    o_f, o_b, _, _ = _gla_bidir(q, v, lf_f, lf_b, s_f, s_b)
    x1, h, aff = _mixer_out(_hgrn_out_kernel, "hgrn_out", [o_f, o_b, g], [], x0,
                            [row_of(hg_norm, 0), hg_w_out[0].astype(BF16)],
                            vec(0, 2), row_of(norm_ffn, 0), vec(0, 3), vec(0, 4), router_t(0))
    x2 = _moe(x1, h, aff, vec(0, 5), moe_w_gate[0], moe_w_up[0], moe_w_down[0])

    bg, cu = _conv_in(x2, row_of(norm_mix, 1), vec(1, 0), vec(1, 1), sc_w_in[0].astype(BF16))
    x3, h, aff = _mixer_out(_conv_out_kernel, "conv_out", [cu, bg], [cu, cu], x2,
                            [sc_conv[0], sc_w_out[0].astype(BF16)],
                            vec(1, 2), row_of(norm_ffn, 1), vec(1, 3), vec(1, 4), router_t(1))
    x4 = _moe(x3, h, aff, vec(1, 5), moe_w_gate[1], moe_w_up[1], moe_w_down[1])
    return _final_norm(x4, norm_final[None, :])
```

```python
import functools
import math

import numpy as np
import jax
import jax.numpy as jnp
from jax import lax
from jax.experimental import pallas as pl
from jax.experimental.pallas import tpu as pltpu

F32 = jnp.float32
BF16 = jnp.bfloat16

EPS = 1e-6
POS_TEMP = 10000.0
GRID_W = 64
HEAD_DIM = 128
EC_CAPACITY_FACTOR = 2
GLA_CHUNK = 256
GLA_LEVELS = 8
ROW_TILE = 256
TOKEN_BLOCK = 256
SLOT_WINDOW = 64
SLOT_ALIGN = 16
FFN_TILE = 512
FFN_ROWS = 512
V7X_VMEM_LIMIT = 56 * 1024 * 1024

NT_DIMS = (((1,), (1,)), ((), ()))
TN_DIMS = (((0,), (0,)), ((), ()))


def _params(*sem, vmem=None):
    return pltpu.CompilerParams(dimension_semantics=sem, vmem_limit_bytes=vmem)


def _split3(x):
    hi = x.astype(BF16)
    r = x - hi.astype(F32)
    mid = r.astype(BF16)
    lo = (r - mid.astype(F32)).astype(BF16)
    return hi, mid, lo


def _dot_f32(a, b, dims):
    a0, a1, a2 = _split3(a)
    b0, b1, b2 = _split3(b)
    d = lambda x, y: lax.dot_general(x, y, dims, preferred_element_type=F32)
    return ((d(a0, b2) + d(a2, b0) + d(a1, b1)) + (d(a0, b1) + d(a1, b0))) + d(a0, b0)


def _sigmoid(x):
    return 1.0 / (1.0 + jnp.exp(-x))


def _silu(x):
    return x * _sigmoid(x)


def _modulate(x, gain, shift, scale):
    y = x * lax.rsqrt(jnp.mean(x * x, axis=-1, keepdims=True) + EPS)
    return (y * gain) * (1.0 + scale) + shift


def _ada_kernel(cond_ref, w_ref, b_ref, out_ref):
    s = _silu(cond_ref[...])
    out_ref[0, 0] = _dot_f32(s, w_ref[0], (((1,), (0,)), ((), ()))) + b_ref[0, 0]


def _ada_vectors(cond, ada_w, ada_b, n_ada):
    depth, d_, _ = ada_w.shape
    return pl.pallas_call(
        _ada_kernel,
        grid=(depth, n_ada),
        in_specs=[pl.BlockSpec((8, d_), lambda i, j: (0, 0)),
                  pl.BlockSpec((1, d_, d_), lambda i, j: (i, 0, j)),
                  pl.BlockSpec((1, 1, 1, d_), lambda i, j: (i, j, 0, 0))],
        out_specs=pl.BlockSpec((1, 1, 8, d_), lambda i, j: (i, j, 0, 0)),
        out_shape=jax.ShapeDtypeStruct((depth, n_ada, 8, d_), F32),
        compiler_params=_params("arbitrary", "arbitrary"),
        name="ada_vectors",
    )(cond, ada_w, ada_b.reshape(depth, n_ada, 1, d_))


def _tables_kernel(lb_logits_ref, rowtab_ref, coltab_ref, lb_ref, *, n_freq):
    def table(n_pos):
        p = lax.broadcasted_iota(jnp.int32, (n_pos, n_freq), 0).astype(F32)
        j = lax.broadcasted_iota(jnp.int32, (n_pos, n_freq), 1).astype(F32)
        omega = jnp.exp(j * (-math.log(POS_TEMP) / n_freq))
        ang = p * omega
        return jnp.concatenate([jnp.sin(ang), jnp.cos(ang)], axis=-1)

    rowtab_ref[...] = table(rowtab_ref.shape[0])
    coltab_ref[...] = table(coltab_ref.shape[0])
    logits = lb_logits_ref[...]
    e = jnp.exp(logits - jnp.max(logits, axis=0, keepdims=True))
    sm = e / jnp.sum(e, axis=0, keepdims=True)
    acc = sm[0:1]
    lb_ref[0:1] = acc
    for i in range(1, lb_ref.shape[0]):
        acc = acc + sm[i:i + 1]
        lb_ref[i:i + 1] = acc


def _tables(lb_logits, n_tokens):
    n_lb, d_ = lb_logits.shape
    n_freq = d_ // 4
    rows = n_tokens // GRID_W
    return pl.pallas_call(
        functools.partial(_tables_kernel, n_freq=n_freq),
        out_shape=[jax.ShapeDtypeStruct((rows, 2 * n_freq), F32),
                   jax.ShapeDtypeStruct((GRID_W, 2 * n_freq), F32),
                   jax.ShapeDtypeStruct((n_lb, d_), F32)],
        name="pos_tables",
    )(lb_logits)


def _hgrn_in_kernel(*refs, with_pos, with_gate):
    it = iter(refs)
    x_ref = next(it)
    if with_pos:
        rowtab_ref, coltab_ref = next(it), next(it)
    gain_ref, shift_ref, scale_ref, lb_ref, w_ref = next(it), next(it), next(it), next(it), next(it)
    if with_pos:
        x0_ref = next(it)
    q_ref, v_ref, lff_ref, lfb_ref = next(it), next(it), next(it), next(it)
    g_ref = next(it) if with_gate else None

    x = x_ref[0]
    d_ = x.shape[-1]
    if with_pos:
        rt = rowtab_ref[0]
        ct = coltab_ref[...]
        n_rows = rt.shape[0]
        pos_row = jnp.concatenate([jnp.broadcast_to(rt[r:r + 1], (GRID_W, rt.shape[1])) for r in range(n_rows)], axis=0)
        pos_col = jnp.concatenate([ct] * n_rows, axis=0)
        x = x + jnp.concatenate([pos_row, pos_col], axis=-1)
        x0_ref[0] = x
    h = _modulate(x, gain_ref[...], shift_ref[0], scale_ref[0]).astype(BF16)
    lb = lb_ref[...]

    def part(p):
        return jnp.dot(h, w_ref[:, p * d_:(p + 1) * d_], preferred_element_type=F32)

    q_ref[0] = (part(0) * HEAD_DIM ** -0.5).astype(BF16)
    v_ref[0] = part(1).astype(BF16)
    lff_ref[0] = jnp.log(lb + (1.0 - lb) * _sigmoid(part(2)))
    lfb_ref[0] = jnp.log(lb + (1.0 - lb) * _sigmoid(part(3)))
    if with_gate:
        g_ref[0] = part(4).astype(BF16)


def _hgrn_in(x, tabs, gain, shift, scale, lb, w_in, *, with_gate):
    b_, t_, d_ = x.shape
    tm = ROW_TILE
    with_pos = tabs is not None
    per_sample = lambda a: pl.BlockSpec((1, 1, d_), (lambda b, i: (b, 0, 0)) if a.shape[0] > 1 else (lambda b, i: (0, 0, 0)))
    row = pl.BlockSpec((1, tm, d_), lambda b, i: (b, i, 0))
    vec = pl.BlockSpec((1, d_), lambda b, i: (0, 0))
    args, in_specs = [x], [row]
    if with_pos:
        rowtab, coltab = tabs
        rows_per_tile = tm // GRID_W
        args += [rowtab.reshape(rowtab.shape[0] // rows_per_tile, rows_per_tile, rowtab.shape[1]), coltab]
        in_specs += [pl.BlockSpec((1, rows_per_tile, rowtab.shape[1]), lambda b, i: (i, 0, 0)),
                     pl.BlockSpec(coltab.shape, lambda b, i: (0, 0))]
    args += [gain, shift, scale, lb, w_in]
    in_specs += [vec, per_sample(shift), per_sample(scale), vec, pl.BlockSpec(w_in.shape, lambda b, i: (0, 0))]
    out_shape, out_specs = [], []
    if with_pos:
        out_shape.append(jax.ShapeDtypeStruct((b_, t_, d_), F32))
        out_specs.append(row)
    out_shape += [jax.ShapeDtypeStruct((b_, t_, d_), BF16)] * 2 + [jax.ShapeDtypeStruct((b_, t_, d_), F32)] * 2
    out_specs += [row] * 4
    if with_gate:
        out_shape.append(jax.ShapeDtypeStruct((b_, t_, d_), BF16))
        out_specs.append(row)
    return pl.pallas_call(
        functools.partial(_hgrn_in_kernel, with_pos=with_pos, with_gate=with_gate),
        grid=(b_, t_ // tm),
        in_specs=in_specs, out_specs=out_specs, out_shape=out_shape,
        compiler_params=_params("arbitrary", "arbitrary", vmem=V7X_VMEM_LIMIT),
        name="hgrn_in_latent" if with_pos else "hgrn_in_context",
    )(*args)


def _gla_consts(reverse):
    c = GLA_CHUNK
    idx = np.arange(c)
    rank = (c - 1 - idx) if reverse else idx
    tri = (rank[None, :] <= rank[:, None]).astype(np.float32)
    lvl = np.full((c, c), -1, np.int32)
    rt, rs = rank[:, None], rank[None, :]
    lvl[rt == rs] = 0
    sgn = np.zeros((GLA_LEVELS, c, HEAD_DIM), np.float32)
    for level in range(1, GLA_LEVELS + 1):
        blk, half = 1 << level, 1 << (level - 1)
        lvl[(rt // blk == rs // blk) & ((rt % blk) >= half) & ((rs % blk) < half)] = level
        sgn[level - 1] = np.where((rank % blk) >= half, 1.0, -1.0)[:, None]
    return (jnp.asarray(np.concatenate([tri, tri, tri], axis=1), BF16), jnp.asarray(lvl),
            jnp.asarray(sgn.reshape(GLA_LEVELS * c, HEAD_DIM)))


def _boundary_rows(level, reverse):
    c = GLA_CHUNK
    blk, half = 1 << level, 1 << (level - 1)
    rows = []
    for i in range(c):
        rank = (c - 1 - i) if reverse else i
        brank = (rank // blk) * blk + half - 1
        rows.append((c - 1 - brank) if reverse else brank)
    return rows


def _gla_chunk(q_ref, v_ref, lf_ref, st_ref, cum_ref, tri3_ref, lvl_ref, sgn_ref, o_ref, reverse):
    c = GLA_CHUNK
    lf = lf_ref[0]
    q = q_ref[0].astype(F32)
    v = v_ref[0]
    kk = 1.0 - jnp.exp(lf)
    parts = jnp.concatenate(_split3(lf), axis=0)
    cum = jnp.dot(tri3_ref[...], parts, preferred_element_type=F32)
    cum_ref[...] = cum
    lvl = lvl_ref[...]
    s0 = lax.dot_general(q.astype(BF16), kk.astype(BF16), NT_DIMS, preferred_element_type=F32)
    scores = jnp.where(lvl == 0, s0, 0.0)
    row8 = lax.broadcasted_iota(jnp.int32, (8, HEAD_DIM), 0)
    for level in range(1, GLA_LEVELS + 1):
        sgn = sgn_ref[(level - 1) * c:level * c, :]
        later = sgn > 0.0
        if level == 1:
            g = jnp.where(later, lf, 0.0)
        else:
            brow = _boundary_rows(level, reverse)
            pieces = []
            for g8 in range(c // 8):
                rows = brow[g8 * 8:(g8 + 1) * 8]
                first = jnp.broadcast_to(cum_ref[rows[0]:rows[0] + 1, :], (8, HEAD_DIM))
                if level == 2:
                    second = jnp.broadcast_to(cum_ref[rows[7]:rows[7] + 1, :], (8, HEAD_DIM))
                    pieces.append(jnp.where(row8 < 4, first, second))
                else:
                    pieces.append(first)
            g = (cum - jnp.concatenate(pieces, axis=0)) * sgn
        xe = (jnp.where(later, q, kk) * jnp.exp(g)).astype(BF16)
        sl = lax.dot_general(xe, xe, NT_DIMS, preferred_element_type=F32)
        scores = jnp.where(lvl == level, sl, scores)
    st = st_ref[...]
    qe = (q * jnp.exp(cum)).astype(BF16)
    o = jnp.dot(scores.astype(BF16), v, preferred_element_type=F32)
    o_ref[0] = o + lax.dot_general(qe, st.astype(BF16), NT_DIMS, preferred_element_type=F32)
    last_row = 0 if reverse else c - 1
    last = cum_ref[last_row:last_row + 1, :]
    ke = (kk * jnp.exp(last - cum)).astype(BF16)
    st_ref[...] = st * jnp.exp(last) + lax.dot_general(v, ke, TN_DIMS, preferred_element_type=F32)


def _gla_kernel(qf_ref, vf_ref, lff_ref, qb_ref, vb_ref, lfb_ref, s0f_ref, s0b_ref,
                tri3f_ref, lvlf_ref, sgnf_ref, tri3b_ref, lvlb_ref, sgnb_ref,
                of_ref, ob_ref, sff_ref, sfb_ref, stf_ref, stb_ref, cum_ref):
    j = pl.program_id(2)

    @pl.when(j == 0)
    def _():
        stf_ref[...] = s0f_ref[0, 0]
        stb_ref[...] = s0b_ref[0, 0]

    _gla_chunk(qf_ref, vf_ref, lff_ref, stf_ref, cum_ref, tri3f_ref, lvlf_ref, sgnf_ref, of_ref, False)
    _gla_chunk(qb_ref, vb_ref, lfb_ref, stb_ref, cum_ref, tri3b_ref, lvlb_ref, sgnb_ref, ob_ref, True)

    @pl.when(j == pl.num_programs(2) - 1)
    def _():
        sff_ref[0, 0] = stf_ref[...]
        sfb_ref[0, 0] = stb_ref[...]


def _gla_bidir(q, v, lf_f, lf_b, s0f, s0b):
    b_, t_, d_ = q.shape
    h_ = d_ // HEAD_DIM
    c = GLA_CHUNK
    n = t_ // c
    fwd = lambda b, h, j: (b, j, h)
    bwd = lambda b, h, j: (b, n - 1 - j, h)
    st = lambda b, h, j: (b, h, 0, 0)
    const = lambda b, h, j: (0, 0)
    blk = lambda im: pl.BlockSpec((1, c, HEAD_DIM), im)
    st_spec = pl.BlockSpec((1, 1, HEAD_DIM, HEAD_DIM), st)
    cspecs = [pl.BlockSpec((c, 3 * c), const), pl.BlockSpec((c, c), const),
              pl.BlockSpec((GLA_LEVELS * c, HEAD_DIM), const)]
    return pl.pallas_call(
        _gla_kernel,
        grid=(b_, h_, n),
        in_specs=[blk(fwd), blk(fwd), blk(fwd), blk(bwd), blk(bwd), blk(bwd), st_spec, st_spec] + cspecs + cspecs,
        out_specs=[blk(fwd), blk(bwd), st_spec, st_spec],
        out_shape=[jax.ShapeDtypeStruct((b_, t_, d_), F32)] * 2
                  + [jax.ShapeDtypeStruct((b_, h_, HEAD_DIM, HEAD_DIM), F32)] * 2,
        scratch_shapes=[pltpu.VMEM((HEAD_DIM, HEAD_DIM), F32), pltpu.VMEM((HEAD_DIM, HEAD_DIM), F32),
                        pltpu.VMEM((c, HEAD_DIM), F32)],
        compiler_params=_params("arbitrary", "arbitrary", "arbitrary"),
        name="gla_bidir",
    )(q, v, lf_f, q, v, lf_b, s0f, s0b, *_gla_consts(False), *_gla_consts(True))


def _ffn_prologue(x, gain_ref, shift_ref, scale_ref, wr_ref, h_ref, aff_ref):
    hf = _modulate(x, gain_ref[...], shift_ref[0], scale_ref[0])
    h_ref[0] = hf.astype(BF16)
    logits = _dot_f32(wr_ref[...], hf, NT_DIMS)
    e = jnp.exp(logits - jnp.max(logits, axis=0, keepdims=True))
    aff_ref[0] = e / jnp.sum(e, axis=0, keepdims=True)


def _hgrn_out_kernel(of_ref, ob_ref, g_ref, x_ref, hnorm_ref, w_ref, gate_ref,
                     gain_ref, shift_ref, scale_ref, wr_ref, x1_ref, h_ref, aff_ref):
    o = of_ref[0] + ob_ref[0]
    hn = hnorm_ref[...]
    heads = []
    for h in range(o.shape[-1] // HEAD_DIM):
        oh = o[:, h * HEAD_DIM:(h + 1) * HEAD_DIM]
        heads.append(oh * lax.rsqrt(jnp.mean(oh * oh, axis=-1, keepdims=True) + EPS) * hn)
    y = jnp.concatenate(heads, axis=-1) * _silu(g_ref[0].astype(F32))
    y = jnp.dot(y.astype(BF16), w_ref[...], preferred_element_type=F32)
    x1 = x_ref[0] + gate_ref[0] * y
    x1_ref[0] = x1
    _ffn_prologue(x1, gain_ref, shift_ref, scale_ref, wr_ref, h_ref, aff_ref)


def _conv_out_kernel(cu_ref, bg_ref, prev_ref, next_ref, x_ref, wc_ref, w_ref, gate_ref,
                     gain_ref, shift_ref, scale_ref, wr_ref, x1_ref, h_ref, aff_ref):
    i = pl.program_id(1)
    cu = cu_ref[0].astype(F32)
    tm = cu.shape[0]
    rid = lax.broadcasted_iota(jnp.int32, cu.shape, 0)
    before = jnp.where(i == 0, 0.0, prev_ref[0, 7:8, :].astype(F32))
    after = jnp.where(i == pl.num_programs(1) - 1, 0.0, next_ref[0, 0:1, :].astype(F32))
    left = jnp.where(rid == 0, before, pltpu.roll(cu, 1, 0))
    right = jnp.where(rid == tm - 1, after, pltpu.roll(cu, tm - 1, 0))
    wc = wc_ref[...]
    y = left * wc[0:1] + cu * wc[1:2] + right * wc[2:3]
    y = (bg_ref[0].astype(F32) * y).astype(BF16)
    y = jnp.dot(y, w_ref[...], preferred_element_type=F32)
    x1 = x_ref[0] + gate_ref[0] * y
    x1_ref[0] = x1
    _ffn_prologue(x1, gain_ref, shift_ref, scale_ref, wr_ref, h_ref, aff_ref)


def _mixer_out(kernel, name, row_args, halo_args, x, small_args, gate, gain, shift, scale, w_router_t):
    b_, t_, d_ = x.shape
    e_ = w_router_t.shape[0]
    tm = ROW_TILE
    row = pl.BlockSpec((1, tm, d_), lambda b, i: (b, i, 0))
    per_sample = pl.BlockSpec((1, 1, d_), lambda b, i: (b, 0, 0))
    whole = lambda a: pl.BlockSpec(a.shape, lambda b, i: (0,) * a.ndim)
    n8 = t_ // 8
    halo_specs = [pl.BlockSpec((1, 8, d_), lambda b, i: (b, jnp.maximum(i * (tm // 8) - 1, 0), 0)),
                  pl.BlockSpec((1, 8, d_), lambda b, i: (b, jnp.minimum((i + 1) * (tm // 8), n8 - 1), 0))]
    return pl.pallas_call(
        kernel,
        grid=(b_, t_ // tm),
        in_specs=[row] * len(row_args) + halo_specs[:len(halo_args)] + [row]
                 + [whole(a) for a in small_args] + [per_sample, whole(gain), per_sample, per_sample, whole(w_router_t)],
        out_specs=[row, row, pl.BlockSpec((1, e_, tm), lambda b, i: (b, 0, i))],
        out_shape=[jax.ShapeDtypeStruct((b_, t_, d_), F32), jax.ShapeDtypeStruct((b_, t_, d_), BF16),
                   jax.ShapeDtypeStruct((b_, e_, t_), F32)],
        compiler_params=_params("arbitrary", "arbitrary", vmem=V7X_VMEM_LIMIT),
        name=name,
    )(*row_args, *halo_args, x, *small_args, gate, gain, shift, scale, w_router_t)


def _conv_in_kernel(x_ref, gain_ref, shift_ref, scale_ref, w_ref, bg_ref, cu_ref):
    x = x_ref[0]
    d_ = x.shape[-1]
    h = _modulate(x, gain_ref[...], shift_ref[0], scale_ref[0]).astype(BF16)
    part = lambda p: jnp.dot(h, w_ref[:, p * d_:(p + 1) * d_], preferred_element_type=F32)
    bg_ref[0] = part(0).astype(BF16)
    cu_ref[0] = (part(1) * part(2)).astype(BF16)


def _conv_in(x, gain, shift, scale, w_in):
    b_, t_, d_ = x.shape
    tm = ROW_TILE
    row = pl.BlockSpec((1, tm, d_), lambda b, i: (b, i, 0))
    per_sample = pl.BlockSpec((1, 1, d_), lambda b, i: (b, 0, 0))
    return pl.pallas_call(
        _conv_in_kernel,
        grid=(b_, t_ // tm),
        in_specs=[row, pl.BlockSpec((1, d_), lambda b, i: (0, 0)), per_sample, per_sample,
                  pl.BlockSpec(w_in.shape, lambda b, i: (0, 0))],
        out_specs=[row, row],
        out_shape=[jax.ShapeDtypeStruct((b_, t_, d_), BF16)] * 2,
        compiler_params=_params("arbitrary", "arbitrary", vmem=V7X_VMEM_LIMIT),
        name="conv_in",
    )(x, gain, shift, scale, w_in)


def _route_kernel(aff_ref, tri_ref, blockind_ref, slot_ref, base_ref, dense_ref, *, cap):
    aff = aff_ref[0]
    e_, t_ = aff.shape

    def as_float(word):
        return pltpu.bitcast(word, F32)

    def count_ge(th):
        return jnp.sum(jnp.where(aff >= th, 1.0, 0.0), axis=1, keepdims=True)

    def search(_, carry):
        lo, hi = carry
        mid = lo + ((hi - lo + 1) >> 1)
        ok = count_ge(as_float(mid)) >= cap
        return jnp.where(ok, mid, lo), jnp.where(ok, hi, mid - 1)

    lo0 = jnp.zeros((e_, 1), jnp.int32)
    hi0 = jnp.full((e_, 1), 0x7F7FFFFF, jnp.int32)
    kth, _ = lax.fori_loop(0, 32, search, (lo0, hi0))
    above = aff >= as_float(kth + 1)
    tied = jnp.logical_and(aff >= as_float(kth), jnp.logical_not(above))
    need = cap - jnp.sum(jnp.where(above, 1.0, 0.0), axis=1, keepdims=True)
    tri = tri_ref[...]
    tb = tri.shape[0]
    carry_t = jnp.zeros((e_, 1), F32)
    carry_s = jnp.zeros((e_, 1), F32)
    sel_blocks = []
    for j in range(t_ // tb):
        cols = slice(j * tb, (j + 1) * tb)
        tied_j = tied[:, cols]
        ct = jnp.dot(jnp.where(tied_j, 1.0, 0.0).astype(BF16), tri, preferred_element_type=F32) + carry_t
        carry_t = ct[:, tb - 1:tb]
        sel_j = jnp.where(above[:, cols], 1.0, jnp.where(tied_j & (ct <= need), 1.0, 0.0))
        cs = jnp.dot(sel_j.astype(BF16), tri, preferred_element_type=F32) + carry_s
        carry_s = cs[:, tb - 1:tb]
        slot_ref[0, :, cols] = jnp.where(sel_j > 0.0, cs - 1.0, -1.0).astype(jnp.int32)
        sel_blocks.append(sel_j.astype(BF16))
    sel = jnp.concatenate(sel_blocks, axis=1)
    counts = jnp.dot(sel, blockind_ref[...], preferred_element_type=F32)
    base, end = counts[:, :128], counts[:, 128:]
    base_ref[0] = base.astype(jnp.int32)
    span = end - jnp.floor(base * (1.0 / SLOT_ALIGN)) * SLOT_ALIGN
    dense = jnp.max(span, axis=0, keepdims=True) <= SLOT_WINDOW
    dense_ref[0] = jnp.where(dense, 1, 0).astype(jnp.int32)


def _route(aff):
    b_, e_, t_ = aff.shape
    cap = EC_CAPACITY_FACTOR * t_ // e_
    tb = TOKEN_BLOCK
    nb = t_ // tb
    tri = jnp.asarray(np.triu(np.ones((tb, tb), np.float32)), BF16)
    tok, col = np.arange(t_)[:, None], np.arange(128)[None, :]
    blockind = np.concatenate([(tok < col * tb) & (col <= nb), (tok < (col + 1) * tb) & (col < nb)], axis=1)
    slot, base, dense = pl.pallas_call(
        functools.partial(_route_kernel, cap=cap),
        grid=(b_,),
        in_specs=[pl.BlockSpec((1, e_, t_), lambda b: (b, 0, 0)),
                  pl.BlockSpec((tb, tb), lambda b: (0, 0)),
                  pl.BlockSpec((t_, 256), lambda b: (0, 0))],
        out_specs=[pl.BlockSpec((1, e_, t_), lambda b: (b, 0, 0)), pl.BlockSpec((1, e_, 128), lambda b: (b, 0, 0)),
                   pl.BlockSpec((1, 1, 128), lambda b: (b, 0, 0))],
        out_shape=[jax.ShapeDtypeStruct((b_, e_, t_), jnp.int32), jax.ShapeDtypeStruct((b_, e_, 128), jnp.int32),
                   jax.ShapeDtypeStruct((b_, 1, 128), jnp.int32)],
        compiler_params=_params("arbitrary"),
        name="route",
    )(aff, tri, jnp.asarray(blockind.astype(np.float32), BF16))
    return slot, base[:, :, :nb + 1].reshape(-1), dense[:, 0, :nb].reshape(-1)


def _window_plan(tbl_ref, b, e, tb, ne, nb):
    idx = (b * ne + e) * (nb + 1) + tb
    base, end = tbl_ref[idx], tbl_ref[idx + 1]
    start = (base >> 4) << 4
    n_win = jnp.where(end > base, (end - start + SLOT_WINDOW - 1) >> 6, 0)
    return start, n_win


def _dense_start(tbl_ref, b, e, tb, ne, nb, cap):
    base = tbl_ref[(b * ne + e) * (nb + 1) + tb]
    return pl.multiple_of(jnp.minimum((base >> 4) << 4, cap - SLOT_WINDOW), SLOT_ALIGN)


def _gather_kernel(tbl_ref, dense_ref, h_ref, slot_ref, xg_ref, *, ne, nb, cap):
    b, tb = pl.program_id(0), pl.program_id(2)

    @pl.when(tb == 0)
    def _():
        xg_ref[...] = jnp.zeros(xg_ref.shape, xg_ref.dtype)

    h = h_ref[0]
    n_tok = h.shape[0]
    ids0 = lax.broadcasted_iota(jnp.int32, (SLOT_WINDOW, n_tok), 0)
    dense = dense_ref[b * nb + tb] > 0

    @pl.when(dense)
    def _():
        starts = [_dense_start(tbl_ref, b, e, tb, ne, nb, cap) for e in range(ne)]
        onehot = jnp.concatenate(
            [jnp.where(ids0 == slot_ref[0, e:e + 1, :] - starts[e], 1.0, 0.0).astype(BF16) for e in range(ne)], axis=0)
        rows = jnp.dot(onehot, h, preferred_element_type=F32).astype(BF16)
        for e in range(ne):
            win = xg_ref.at[0, e, pl.ds(starts[e], SLOT_WINDOW), :]
            win[...] = win[...] + rows[e * SLOT_WINDOW:(e + 1) * SLOT_WINDOW]

    @pl.when(jnp.logical_not(dense))
    def _():
        for e in range(ne):
            srow = slot_ref[0, e:e + 1, :]
            start, n_win = _window_plan(tbl_ref, b, e, tb, ne, nb)

            def body(k, carry, e=e, srow=srow, start=start):
                lo = start + k * SLOT_WINDOW
                w0 = pl.multiple_of(jnp.minimum(lo, cap - SLOT_WINDOW), SLOT_ALIGN)
                ids = w0 + ids0
                onehot = jnp.where(ids >= lo, jnp.where(ids == srow, 1.0, 0.0), 0.0).astype(BF16)
                rows = jnp.dot(onehot, h, preferred_element_type=F32)
                win = xg_ref.at[0, e, pl.ds(w0, SLOT_WINDOW), :]
                win[...] = win[...] + rows.astype(BF16)
                return carry

            lax.fori_loop(0, n_win, body, 0)


def _gather(h, slot, tbl, dense):
    b_, t_, d_ = h.shape
    e_ = slot.shape[1]
    cap = EC_CAPACITY_FACTOR * t_ // e_
    tb = TOKEN_BLOCK
    nb = t_ // tb
    dh = d_ // 2
    return pl.pallas_call(
        functools.partial(_gather_kernel, ne=e_, nb=nb, cap=cap),
        grid_spec=pltpu.PrefetchScalarGridSpec(
            num_scalar_prefetch=2,
            grid=(b_, 2, nb),
            in_specs=[pl.BlockSpec((1, tb, dh), lambda b, c, i, *_: (b, i, c)),
                      pl.BlockSpec((1, e_, tb), lambda b, c, i, *_: (b, 0, i))],
            out_specs=pl.BlockSpec((1, e_, cap, dh), lambda b, c, i, *_: (b, 0, 0, c)),
        ),
        out_shape=jax.ShapeDtypeStruct((b_, e_, cap, d_), BF16),
        compiler_params=_params("arbitrary", "arbitrary", "arbitrary", vmem=V7X_VMEM_LIMIT),
        name="moe_gather",
    )(tbl, dense, h, slot)


def _expert_kernel(xg_ref, wg_ref, wu_ref, wd_ref, y_ref, acc_ref):
    f = pl.program_id(1)
    wg = wg_ref[0, 0].astype(BF16)
    wu = wu_ref[0, 0].astype(BF16)
    wd = wd_ref[0, 0].astype(BF16)
    n_b, _, cap, _ = xg_ref.shape

    @pl.when(f == 0)
    def _():
        acc_ref[...] = jnp.zeros(acc_ref.shape, F32)

    for b in range(n_b):
        for r in range(cap // FFN_ROWS):
            rows = pl.ds(r * FFN_ROWS, FFN_ROWS)
            acc_rows = pl.ds((b * cap) + r * FFN_ROWS, FFN_ROWS)
            xr = xg_ref[b, 0, rows, :]
            a = jnp.dot(xr, wg, preferred_element_type=F32)
            u = jnp.dot(xr, wu, preferred_element_type=F32)
            part = jnp.dot((_silu(a) * u).astype(BF16), wd, preferred_element_type=F32)
            acc_ref[acc_rows, :] = acc_ref[acc_rows, :] + part

    @pl.when(f == pl.num_programs(1) - 1)
    def _():
        for b in range(n_b):
            y_ref[b, 0] = acc_ref[b * cap:(b + 1) * cap, :].astype(BF16)


def _experts(xg, layer, w_gate, w_up, w_down):
    b_, e_, cap, d_ = xg.shape
    f_ = w_gate.shape[-1]
    ft = min(FFN_TILE, f_)
    return pl.pallas_call(
        _expert_kernel,
        grid=(e_, f_ // ft),
        in_specs=[pl.BlockSpec((b_, 1, cap, d_), lambda e, f: (0, e, 0, 0)),
                  pl.BlockSpec((1, 1, d_, ft), lambda e, f: (layer, e, 0, f)),
                  pl.BlockSpec((1, 1, d_, ft), lambda e, f: (layer, e, 0, f)),
                  pl.BlockSpec((1, 1, ft, d_), lambda e, f: (layer, e, f, 0))],
        out_specs=pl.BlockSpec((b_, 1, cap, d_), lambda e, f: (0, e, 0, 0)),
        out_shape=jax.ShapeDtypeStruct((b_, e_, cap, d_), BF16),
        scratch_shapes=[pltpu.VMEM((b_ * cap, d_), F32)],
        compiler_params=_params("arbitrary", "arbitrary", vmem=V7X_VMEM_LIMIT),
        name="moe_experts",
    )(xg, w_gate, w_up, w_down)


def _combine_kernel(tbl_ref, dense_ref, y_ref, slot_ref, aff_ref, x_ref, gate_ref, out_ref, acc_ref, *, ne, nb, cap):
    b, tb = pl.program_id(0), pl.program_id(2)
    n_tok = acc_ref.shape[0]
    ids0 = lax.broadcasted_iota(jnp.int32, (SLOT_WINDOW, n_tok), 0)
    dense = dense_ref[b * nb + tb] > 0

    @pl.when(dense)
    def _():
        starts = [_dense_start(tbl_ref, b, e, tb, ne, nb, cap) for e in range(ne)]
        weights = jnp.concatenate(
            [jnp.where(ids0 == slot_ref[0, e:e + 1, :] - starts[e], aff_ref[0, e:e + 1, :], 0.0).astype(BF16)
             for e in range(ne)], axis=0)
        yw = jnp.concatenate([y_ref[0, e, pl.ds(starts[e], SLOT_WINDOW), :] for e in range(ne)], axis=0)
        acc_ref[...] = lax.dot_general(weights, yw, TN_DIMS, preferred_element_type=F32)

    @pl.when(jnp.logical_not(dense))
    def _():
        acc_ref[...] = jnp.zeros(acc_ref.shape, F32)
        for e in range(ne):
            srow = slot_ref[0, e:e + 1, :]
            grow = aff_ref[0, e:e + 1, :]
            start, n_win = _window_plan(tbl_ref, b, e, tb, ne, nb)

            def body(k, carry, e=e, srow=srow, grow=grow, start=start):
                lo = start + k * SLOT_WINDOW
                w0 = pl.multiple_of(jnp.minimum(lo, cap - SLOT_WINDOW), SLOT_ALIGN)
                ids = w0 + ids0
                weights = jnp.where(ids >= lo, jnp.where(ids == srow, grow, 0.0), 0.0).astype(BF16)
                yw = y_ref[0, e, pl.ds(w0, SLOT_WINDOW), :]
                acc_ref[...] = acc_ref[...] + lax.dot_general(weights, yw, TN_DIMS, preferred_element_type=F32)
                return carry

            lax.fori_loop(0, n_win, body, 0)

    out_ref[0] = x_ref[0] + gate_ref[0] * acc_ref[...]


def _combine(y, slot, aff, tbl, dense, x, gate):
    b_, t_, d_ = x.shape
    e_, cap = y.shape[1], y.shape[2]
    tb = TOKEN_BLOCK
    nb = t_ // tb
    dh = d_ // 2
    route_spec = pl.BlockSpec((1, e_, tb), lambda b, c, i, *_: (b, 0, i))
    row = pl.BlockSpec((1, tb, dh), lambda b, c, i, *_: (b, i, c))
    return pl.pallas_call(
        functools.partial(_combine_kernel, ne=e_, nb=nb, cap=cap),
        grid_spec=pltpu.PrefetchScalarGridSpec(
            num_scalar_prefetch=2,
            grid=(b_, 2, nb),
            in_specs=[pl.BlockSpec((1, e_, cap, dh), lambda b, c, i, *_: (b, 0, 0, c)),
                      route_spec, route_spec, row,
                      pl.BlockSpec((1, 1, dh), lambda b, c, i, *_: (b, 0, c))],
            out_specs=row,
            scratch_shapes=[pltpu.VMEM((tb, dh), F32)],
        ),
        out_shape=jax.ShapeDtypeStruct((b_, t_, d_), F32),
        compiler_params=_params("arbitrary", "arbitrary", "arbitrary", vmem=V7X_VMEM_LIMIT),
        name="moe_combine",
    )(tbl, dense, y, slot, aff, x, gate)


def _moe(x, h, aff, gate, layer, w_gate, w_up, w_down):
    slot, tbl, dense = _route(aff)
    xg = _gather(h, slot, tbl, dense)
    y = _experts(xg, layer, w_gate, w_up, w_down)
    return _combine(y, slot, aff, tbl, dense, x, gate)


def _final_norm_kernel(x_ref, gain_ref, out_ref):
    x = x_ref[0]
    out_ref[0] = x * lax.rsqrt(jnp.mean(x * x, axis=-1, keepdims=True) + EPS) * gain_ref[...]


def _final_norm(x, gain):
    b_, t_, d_ = x.shape
    tm = ROW_TILE
    row = pl.BlockSpec((1, tm, d_), lambda b, i: (b, i, 0))
    return pl.pallas_call(
        _final_norm_kernel,
        grid=(b_, t_ // tm),
        in_specs=[row, pl.BlockSpec((1, d_), lambda b, i: (0, 0))],
        out_specs=row,
        out_shape=jax.ShapeDtypeStruct((b_, t_, d_), F32),
        compiler_params=_params("arbitrary", "arbitrary"),
        name="final_norm",
    )(x, gain)


def kernel(x, c, ctx, c_ctx, ada_w, ada_b, norm_mix, norm_ffn, norm_final, hg_w_in, hg_lb_logits, hg_norm, hg_w_out, sc_w_in, sc_conv, sc_w_out, moe_router, moe_w_gate, moe_w_up, moe_w_down):
    b_, t_, d_ = x.shape
    depth = ada_w.shape[0]
    n_ada = ada_w.shape[-1] // d_
    n_heads = d_ // HEAD_DIM
    assert depth == 2 and n_ada == 6 and b_ + 1 <= 8
    assert t_ % GLA_CHUNK == 0 and ctx.shape[1] % GLA_CHUNK == 0 and ROW_TILE % GRID_W == 0

    cond = jnp.concatenate([c, c_ctx[None], jnp.zeros((8 - b_ - 1, d_), F32)], axis=0)
    mod = _ada_vectors(cond, ada_w, ada_b, n_ada)
    vec = lambda i, j: mod[i, j, :b_][:, None, :]
    cvec = lambda i, j: mod[i, j, b_][None, None, :]
    rowtab, coltab, lower = _tables(hg_lb_logits, t_)
    row_of = lambda a, i: a[i][None, :]
    router_t = lambda i: jnp.swapaxes(moe_router[i], 0, 1)

    w_in = hg_w_in[0].astype(BF16)
    lb0 = row_of(lower, 0)
    gain0 = row_of(norm_mix, 0)
    qc, vc, lfc_f, lfc_b = _hgrn_in(ctx, None, gain0, cvec(0, 0), cvec(0, 1), lb0, w_in, with_gate=False)
    zeros = jnp.zeros((b_, n_heads, HEAD_DIM, HEAD_DIM), F32)
    _, _, s_f, s_b = _gla_bidir(qc, vc, lfc_f, lfc_b, zeros, zeros)
    x0, q, v, lf_f, lf_b, g = _hgrn_in(x, (rowtab, coltab), gain0, vec(0, 0), vec(0, 1), lb0, w_in, with_gate=True)
    o_f, o_b, _, _ = _gla_bidir(q, v, lf_f, lf_b, s_f, s_b)
    x1, h, aff = _mixer_out(_hgrn_out_kernel, "hgrn_out", [o_f, o_b, g], [], x0,
                            [row_of(hg_norm, 0), hg_w_out[0].astype(BF16)],
                            vec(0, 2), row_of(norm_ffn, 0), vec(0, 3), vec(0, 4), router_t(0))
    x2 = _moe(x1, h, aff, vec(0, 5), 0, moe_w_gate, moe_w_up, moe_w_down)

    bg, cu = _conv_in(x2, row_of(norm_mix, 1), vec(1, 0), vec(1, 1), sc_w_in[0].astype(BF16))
    x3, h, aff = _mixer_out(_conv_out_kernel, "conv_out", [cu, bg], [cu, cu], x2,
                            [sc_conv[0], sc_w_out[0].astype(BF16)],
                            vec(1, 2), row_of(norm_ffn, 1), vec(1, 3), vec(1, 4), router_t(1))
    x4 = _moe(x3, h, aff, vec(1, 5), 1, moe_w_gate, moe_w_up, moe_w_down)
    return _final_norm(x4, norm_final[None, :])
```

```python
import functools
import math

import numpy as np
import jax
import jax.numpy as jnp
from jax import lax
from jax.experimental import pallas as pl
from jax.experimental.pallas import tpu as pltpu

F32 = jnp.float32
BF16 = jnp.bfloat16

EPS = 1e-6
POS_TEMP = 10000.0
GRID_W = 64
HEAD_DIM = 128
EC_CAPACITY_FACTOR = 2
GLA_CHUNK = 256
GLA_LEVELS = 8
ROW_TILE = 512
SUB_ROWS = 256
TOKEN_BLOCK = 256
SLOT_WINDOW = 64
DENSE_WINDOW = 128
SLOT_ALIGN = 16
FFN_TILE = 512
FFN_ROWS = 512
V7X_VMEM_LIMIT = 56 * 1024 * 1024

NT_DIMS = (((1,), (1,)), ((), ()))
TN_DIMS = (((0,), (0,)), ((), ()))


def _params(*sem, vmem=None):
    return pltpu.CompilerParams(dimension_semantics=sem, vmem_limit_bytes=vmem)


def _split3(x):
    hi = x.astype(BF16)
    r = x - hi.astype(F32)
    mid = r.astype(BF16)
    lo = (r - mid.astype(F32)).astype(BF16)
    return hi, mid, lo


def _dot_f32(a, b, dims):
    a0, a1, a2 = _split3(a)
    b0, b1, b2 = _split3(b)
    d = lambda x, y: lax.dot_general(x, y, dims, preferred_element_type=F32)
    return ((d(a0, b2) + d(a2, b0) + d(a1, b1)) + (d(a0, b1) + d(a1, b0))) + d(a0, b0)


def _dot_bf16x3(a, b, dims):
    a0, b0 = a.astype(BF16), b.astype(BF16)
    a1 = (a - a0.astype(F32)).astype(BF16)
    b1 = (b - b0.astype(F32)).astype(BF16)
    d = lambda x, y: lax.dot_general(x, y, dims, preferred_element_type=F32)
    return (d(a0, b1) + d(a1, b0)) + d(a0, b0)


def _sub_tiles(n_rows):
    sub = min(SUB_ROWS, n_rows)
    return [slice(s, s + sub) for s in range(0, n_rows, sub)]


def _sigmoid(x):
    return 1.0 / (1.0 + jnp.exp(-x))


def _silu(x):
    return x * _sigmoid(x)


def _modulate(x, gain, shift, scale):
    y = x * lax.rsqrt(jnp.mean(x * x, axis=-1, keepdims=True) + EPS)
    return (y * gain) * (1.0 + scale) + shift


def _ada_kernel(cond_ref, w_ref, b_ref, out_ref):
    s = _silu(cond_ref[...])
    out_ref[0, 0] = _dot_f32(s, w_ref[0], (((1,), (0,)), ((), ()))) + b_ref[0, 0]


def _ada_vectors(cond, ada_w, ada_b, n_ada):
    depth, d_, _ = ada_w.shape
    return pl.pallas_call(
        _ada_kernel,
        grid=(depth, n_ada),
        in_specs=[pl.BlockSpec((8, d_), lambda i, j: (0, 0)),
                  pl.BlockSpec((1, d_, d_), lambda i, j: (i, 0, j)),
                  pl.BlockSpec((1, 1, 1, d_), lambda i, j: (i, j, 0, 0))],
        out_specs=pl.BlockSpec((1, 1, 8, d_), lambda i, j: (i, j, 0, 0)),
        out_shape=jax.ShapeDtypeStruct((depth, n_ada, 8, d_), F32),
        compiler_params=_params("arbitrary", "arbitrary"),
        name="ada_vectors",
    )(cond, ada_w, ada_b.reshape(depth, n_ada, 1, d_))


def _tables_kernel(lb_logits_ref, rowtab_ref, coltab_ref, lb_ref, *, n_freq):
    def table(n_pos):
        p = lax.broadcasted_iota(jnp.int32, (n_pos, n_freq), 0).astype(F32)
        j = lax.broadcasted_iota(jnp.int32, (n_pos, n_freq), 1).astype(F32)
        omega = jnp.exp(j * (-math.log(POS_TEMP) / n_freq))
        ang = p * omega
        return jnp.concatenate([jnp.sin(ang), jnp.cos(ang)], axis=-1)

    rowtab_ref[...] = table(rowtab_ref.shape[0])
    coltab_ref[...] = table(coltab_ref.shape[0])
    logits = lb_logits_ref[...]
    e = jnp.exp(logits - jnp.max(logits, axis=0, keepdims=True))
    sm = e / jnp.sum(e, axis=0, keepdims=True)
    acc = sm[0:1]
    lb_ref[0:1] = acc
    for i in range(1, lb_ref.shape[0]):
        acc = acc + sm[i:i + 1]
        lb_ref[i:i + 1] = acc


def _tables(lb_logits, n_tokens):
    n_lb, d_ = lb_logits.shape
    n_freq = d_ // 4
    rows = n_tokens // GRID_W
    return pl.pallas_call(
        functools.partial(_tables_kernel, n_freq=n_freq),
        out_shape=[jax.ShapeDtypeStruct((rows, 2 * n_freq), F32),
                   jax.ShapeDtypeStruct((GRID_W, 2 * n_freq), F32),
                   jax.ShapeDtypeStruct((n_lb, d_), F32)],
        name="pos_tables",
    )(lb_logits)


def _hgrn_in_kernel(*refs, with_pos, with_gate):
    it = iter(refs)
    x_ref = next(it)
    if with_pos:
        rowtab_ref, coltab_ref = next(it), next(it)
    gain_ref, shift_ref, scale_ref, lb_ref, w_ref = next(it), next(it), next(it), next(it), next(it)
    if with_pos:
        x0_ref = next(it)
    q_ref, v_ref, lff_ref, lfb_ref = next(it), next(it), next(it), next(it)
    g_ref = next(it) if with_gate else None

    d_ = x_ref.shape[-1]
    lb = lb_ref[...]
    for rows in _sub_tiles(x_ref.shape[1]):
        x = x_ref[0, rows]
        if with_pos:
            grid_rows = range(rows.start // GRID_W, rows.stop // GRID_W)
            pos_row = jnp.concatenate(
                [jnp.broadcast_to(rowtab_ref[0, r:r + 1, :], (GRID_W, rowtab_ref.shape[-1])) for r in grid_rows], axis=0)
            pos_col = jnp.concatenate([coltab_ref[...]] * len(grid_rows), axis=0)
            x = x + jnp.concatenate([pos_row, pos_col], axis=-1)
            x0_ref[0, rows] = x
        h = _modulate(x, gain_ref[...], shift_ref[0], scale_ref[0]).astype(BF16)
        part = lambda p: jnp.dot(h, w_ref[:, p * d_:(p + 1) * d_], preferred_element_type=F32)
        q_ref[0, rows] = (part(0) * HEAD_DIM ** -0.5).astype(BF16)
        v_ref[0, rows] = part(1).astype(BF16)
        lff_ref[0, rows] = jnp.log(lb + (1.0 - lb) * _sigmoid(part(2)))
        lfb_ref[0, rows] = jnp.log(lb + (1.0 - lb) * _sigmoid(part(3)))
        if with_gate:
            g_ref[0, rows] = part(4).astype(BF16)


def _hgrn_in(x, tabs, gain, shift, scale, lb, w_in, *, with_gate):
    b_, t_, d_ = x.shape
    tm = min(ROW_TILE, t_)
    with_pos = tabs is not None
    per_sample = lambda a: pl.BlockSpec((1, 1, d_), (lambda b, i: (b, 0, 0)) if a.shape[0] > 1 else (lambda b, i: (0, 0, 0)))
    row = pl.BlockSpec((1, tm, d_), lambda b, i: (b, i, 0))
    vec = pl.BlockSpec((1, d_), lambda b, i: (0, 0))
    args, in_specs = [x], [row]
    if with_pos:
        rowtab, coltab = tabs
        rows_per_tile = tm // GRID_W
        args += [rowtab.reshape(rowtab.shape[0] // rows_per_tile, rows_per_tile, rowtab.shape[1]), coltab]
        in_specs += [pl.BlockSpec((1, rows_per_tile, rowtab.shape[1]), lambda b, i: (i, 0, 0)),
                     pl.BlockSpec(coltab.shape, lambda b, i: (0, 0))]
    args += [gain, shift, scale, lb, w_in]
    in_specs += [vec, per_sample(shift), per_sample(scale), vec, pl.BlockSpec(w_in.shape, lambda b, i: (0, 0))]
    out_shape, out_specs = [], []
    if with_pos:
        out_shape.append(jax.ShapeDtypeStruct((b_, t_, d_), F32))
        out_specs.append(row)
    out_shape += [jax.ShapeDtypeStruct((b_, t_, d_), BF16)] * 2 + [jax.ShapeDtypeStruct((b_, t_, d_), F32)] * 2
    out_specs += [row] * 4
    if with_gate:
        out_shape.append(jax.ShapeDtypeStruct((b_, t_, d_), BF16))
        out_specs.append(row)
    return pl.pallas_call(
        functools.partial(_hgrn_in_kernel, with_pos=with_pos, with_gate=with_gate),
        grid=(b_, t_ // tm),
        in_specs=in_specs, out_specs=out_specs, out_shape=out_shape,
        compiler_params=_params("arbitrary", "arbitrary", vmem=V7X_VMEM_LIMIT),
        name="hgrn_in_latent" if with_pos else "hgrn_in_context",
    )(*args)


def _gla_consts(reverse):
    c = GLA_CHUNK
    idx = np.arange(c)
    rank = (c - 1 - idx) if reverse else idx
    tri = (rank[None, :] <= rank[:, None]).astype(np.float32)
    hc = c // 2
    hrank = rank[:hc] - rank[:hc].min()
    lvl = np.full((hc, hc), -1, np.int32)
    rt, rs = hrank[:, None], hrank[None, :]
    lvl[rt == rs] = 0
    for level in range(1, GLA_LEVELS):
        blk, half = 1 << level, 1 << (level - 1)
        lvl[(rt // blk == rs // blk) & ((rt % blk) >= half) & ((rs % blk) < half)] = level
    return jnp.asarray(tri, BF16), jnp.asarray(lvl)


def _later_group(level, reverse, group):
    rank = (GLA_CHUNK - 1 - 8 * group) if reverse else 8 * group
    return (rank % (1 << level)) >= (1 << (level - 1))


def _boundary_rows(level, reverse):
    c = GLA_CHUNK
    blk, half = 1 << level, 1 << (level - 1)
    rows = []
    for i in range(c):
        rank = (c - 1 - i) if reverse else i
        brank = (rank // blk) * blk + half - 1
        rows.append((c - 1 - brank) if reverse else brank)
    return rows


def _gla_chunk(q_ref, v_ref, lf_ref, st_ref, cum_ref, tri_ref, lvl_ref, o_ref, reverse):
    c = GLA_CHUNK
    hc, ng = c // 2, c // 8
    halves = (slice(0, hc), slice(hc, c))
    lf = lf_ref[0]
    q = q_ref[0].astype(F32)
    v = v_ref[0]
    kk = 1.0 - jnp.exp(lf)
    hi, mid, lo = _split3(lf)
    tri = tri_ref[...]
    two = jnp.dot(tri, jnp.concatenate([hi, mid], axis=1), preferred_element_type=F32)
    cum = (jnp.dot(tri, lo, preferred_element_type=F32) + two[:, HEAD_DIM:]) + two[:, :HEAD_DIM]
    cum_ref[...] = cum
    lvl = lvl_ref[...]
    bcast = lambda r: jnp.broadcast_to(cum_ref[r:r + 1, :], (8, HEAD_DIM))

    qb, kb = q.astype(BF16), kk.astype(BF16)
    tiles = [jnp.where(lvl == 0, lax.dot_general(qb[hs], kb[hs], NT_DIMS, preferred_element_type=F32), 0.0)
             for hs in halves]
    row8 = lax.broadcasted_iota(jnp.int32, (8, HEAD_DIM), 0)
    rank8 = (7 - row8) if reverse else row8
    for level in range(1, 4):
        sgn8 = jnp.where(((rank8 >> (level - 1)) & 1) == 1, 1.0, -1.0)
        sgn = jnp.concatenate([sgn8] * ng, axis=0)
        later = sgn > 0.0
        if level == 1:
            g = jnp.where(later, lf, 0.0)
        else:
            brow = _boundary_rows(level, reverse)
            pieces = []
            for grp in range(ng):
                first = bcast(brow[8 * grp])
                pieces.append(jnp.where(row8 < 4, first, bcast(brow[8 * grp + 7])) if level == 2 else first)
            g = (cum - jnp.concatenate(pieces, axis=0)) * sgn
        xe = (jnp.where(later, q, kk) * jnp.exp(g)).astype(BF16)
        for h, hs in enumerate(halves):
            s = lax.dot_general(xe[hs], xe[hs], NT_DIMS, preferred_element_type=F32)
            tiles[h] = jnp.where(lvl == level, s, tiles[h])

    groups = lambda a: [a[8 * i:8 * i + 8] for i in range(a.shape[0] // 8)]
    tile_rows = [groups(t) for t in tiles]
    lvl_rows = groups(lvl)
    q_rows, k_rows, cum_rows = groups(q), groups(kk), groups(cum)
    for level in range(4, GLA_LEVELS):
        brow = _boundary_rows(level, reverse)
        cache = {}
        g_rows, x_rows = [], []
        for grp in range(ng):
            cb = cache.setdefault(brow[8 * grp], bcast(brow[8 * grp]))
            later = _later_group(level, reverse, grp)
            g_rows.append(cum_rows[grp] - cb if later else cb - cum_rows[grp])
            x_rows.append(q_rows[grp] if later else k_rows[grp])
        xe = jnp.concatenate(x_rows, axis=0) * jnp.exp(jnp.concatenate(g_rows, axis=0))
        xe_rows = groups(xe)
        for h, hs in enumerate(halves):
            later_groups = [grp for grp in range(h * ng // 2, (h + 1) * ng // 2) if _later_group(level, reverse, grp)]
            qc = jnp.concatenate([xe_rows[grp] for grp in later_groups], axis=0).astype(BF16)
            s = lax.dot_general(qc, xe[hs].astype(BF16), NT_DIMS, preferred_element_type=F32)
            for i, grp in enumerate(later_groups):
                local = grp - h * ng // 2
                tile_rows[h][local] = jnp.where(lvl_rows[local] == level, s[8 * i:8 * i + 8], tile_rows[h][local])

    early, late = (1, 0) if reverse else (0, 1)
    cb = cum_ref[_boundary_rows(GLA_LEVELS, reverse)[0]:_boundary_rows(GLA_LEVELS, reverse)[0] + 1, :]
    ql = (q[halves[late]] * jnp.exp(cum[halves[late]] - cb)).astype(BF16)
    ke = (kk[halves[early]] * jnp.exp(cb - cum[halves[early]])).astype(BF16)
    cross = lax.dot_general(ql, ke, NT_DIMS, preferred_element_type=F32)
    t_a, t_b = (jnp.concatenate(rows, axis=0) for rows in tile_rows)
    zero = jnp.zeros((hc, hc), F32)
    if reverse:
        scores = jnp.concatenate([jnp.concatenate([t_a, cross], axis=1), jnp.concatenate([zero, t_b], axis=1)], axis=0)
    else:
        scores = jnp.concatenate([jnp.concatenate([t_a, zero], axis=1), jnp.concatenate([cross, t_b], axis=1)], axis=0)

    st = st_ref[...]
    qe = (q * jnp.exp(cum)).astype(BF16)
    o = jnp.dot(scores.astype(BF16), v, preferred_element_type=F32)
    o_ref[0] = o + lax.dot_general(qe, st.astype(BF16), NT_DIMS, preferred_element_type=F32)
    last_row = 0 if reverse else c - 1
    last = cum_ref[last_row:last_row + 1, :]
    ke_all = (kk * jnp.exp(last - cum)).astype(BF16)
    st_ref[...] = st * jnp.exp(last) + lax.dot_general(v, ke_all, TN_DIMS, preferred_element_type=F32)


def _gla_kernel(qf_ref, vf_ref, lff_ref, qb_ref, vb_ref, lfb_ref, s0f_ref, s0b_ref,
                trif_ref, lvlf_ref, trib_ref, lvlb_ref,
                of_ref, ob_ref, sff_ref, sfb_ref, stf_ref, stb_ref, cum_ref):
    j = pl.program_id(2)

    @pl.when(j == 0)
    def _():
        stf_ref[...] = s0f_ref[0, 0]
        stb_ref[...] = s0b_ref[0, 0]

    _gla_chunk(qf_ref, vf_ref, lff_ref, stf_ref, cum_ref, trif_ref, lvlf_ref, of_ref, False)
    _gla_chunk(qb_ref, vb_ref, lfb_ref, stb_ref, cum_ref, trib_ref, lvlb_ref, ob_ref, True)

    @pl.when(j == pl.num_programs(2) - 1)
    def _():
        sff_ref[0, 0] = stf_ref[...]
        sfb_ref[0, 0] = stb_ref[...]


def _gla_bidir(q, v, lf_f, lf_b, s0f, s0b):
    b_, t_, d_ = q.shape
    h_ = d_ // HEAD_DIM
    c = GLA_CHUNK
    n = t_ // c
    fwd = lambda b, h, j: (b, j, h)
    bwd = lambda b, h, j: (b, n - 1 - j, h)
    st = lambda b, h, j: (b, h, 0, 0)
    const = lambda b, h, j: (0, 0)
    blk = lambda im: pl.BlockSpec((1, c, HEAD_DIM), im)
    st_spec = pl.BlockSpec((1, 1, HEAD_DIM, HEAD_DIM), st)
    cspecs = [pl.BlockSpec((c, c), const), pl.BlockSpec((c // 2, c // 2), const)]
    return pl.pallas_call(
        _gla_kernel,
        grid=(b_, h_, n),
        in_specs=[blk(fwd), blk(fwd), blk(fwd), blk(bwd), blk(bwd), blk(bwd), st_spec, st_spec] + cspecs + cspecs,
        out_specs=[blk(fwd), blk(bwd), st_spec, st_spec],
        out_shape=[jax.ShapeDtypeStruct((b_, t_, d_), F32)] * 2
                  + [jax.ShapeDtypeStruct((b_, h_, HEAD_DIM, HEAD_DIM), F32)] * 2,
        scratch_shapes=[pltpu.VMEM((HEAD_DIM, HEAD_DIM), F32), pltpu.VMEM((HEAD_DIM, HEAD_DIM), F32),
                        pltpu.VMEM((c, HEAD_DIM), F32)],
        compiler_params=_params("arbitrary", "arbitrary", "arbitrary"),
        name="gla_bidir",
    )(q, v, lf_f, q, v, lf_b, s0f, s0b, *_gla_consts(False), *_gla_consts(True))


def _mixer_epilogue(y, rows, x_ref, w_ref, gate_ref, gain_ref, shift_ref, scale_ref, wr_ref, x1_ref, h_ref, aff_ref):
    y = jnp.dot(y.astype(BF16), w_ref[...], preferred_element_type=F32)
    x1 = x_ref[0, rows] + gate_ref[0] * y
    x1_ref[0, rows] = x1
    hf = _modulate(x1, gain_ref[...], shift_ref[0], scale_ref[0])
    h_ref[0, rows] = hf.astype(BF16)
    logits = _dot_bf16x3(wr_ref[...], hf, NT_DIMS)
    e = jnp.exp(logits - jnp.max(logits, axis=0, keepdims=True))
    aff_ref[0, :, rows] = e / jnp.sum(e, axis=0, keepdims=True)


def _hgrn_out_kernel(of_ref, ob_ref, g_ref, x_ref, hnorm_ref, w_ref, *rest):
    hn = hnorm_ref[...]
    for rows in _sub_tiles(x_ref.shape[1]):
        o = of_ref[0, rows] + ob_ref[0, rows]
        heads = []
        for h in range(o.shape[-1] // HEAD_DIM):
            oh = o[:, h * HEAD_DIM:(h + 1) * HEAD_DIM]
            heads.append(oh * lax.rsqrt(jnp.mean(oh * oh, axis=-1, keepdims=True) + EPS) * hn)
        y = jnp.concatenate(heads, axis=-1) * _silu(g_ref[0, rows].astype(F32))
        _mixer_epilogue(y, rows, x_ref, w_ref, *rest)


def _conv_out_kernel(cu_ref, bg_ref, prev_ref, next_ref, x_ref, wc_ref, w_ref, *rest):
    i = pl.program_id(1)
    cu = cu_ref[0].astype(F32)
    tm = cu.shape[0]
    rid = lax.broadcasted_iota(jnp.int32, cu.shape, 0)
    before = jnp.where(i == 0, 0.0, prev_ref[0, 7:8, :].astype(F32))
    after = jnp.where(i == pl.num_programs(1) - 1, 0.0, next_ref[0, 0:1, :].astype(F32))
    left = jnp.where(rid == 0, before, pltpu.roll(cu, 1, 0))
    right = jnp.where(rid == tm - 1, after, pltpu.roll(cu, tm - 1, 0))
    wc = wc_ref[...]
    conv = left * wc[0:1] + cu * wc[1:2] + right * wc[2:3]
    for rows in _sub_tiles(tm):
        _mixer_epilogue(bg_ref[0, rows].astype(F32) * conv[rows], rows, x_ref, w_ref, *rest)


def _mixer_out(kernel, name, row_args, halo_args, x, small_args, gate, gain, shift, scale, w_router_t):
    b_, t_, d_ = x.shape
    e_ = w_router_t.shape[0]
    tm = min(ROW_TILE, t_)
    row = pl.BlockSpec((1, tm, d_), lambda b, i: (b, i, 0))
    per_sample = pl.BlockSpec((1, 1, d_), lambda b, i: (b, 0, 0))
    whole = lambda a: pl.BlockSpec(a.shape, lambda b, i: (0,) * a.ndim)
    n8 = t_ // 8
    halo_specs = [pl.BlockSpec((1, 8, d_), lambda b, i: (b, jnp.maximum(i * (tm // 8) - 1, 0), 0)),
                  pl.BlockSpec((1, 8, d_), lambda b, i: (b, jnp.minimum((i + 1) * (tm // 8), n8 - 1), 0))]
    return pl.pallas_call(
        kernel,
        grid=(b_, t_ // tm),
        in_specs=[row] * len(row_args) + halo_specs[:len(halo_args)] + [row]
                 + [whole(a) for a in small_args] + [per_sample, whole(gain), per_sample, per_sample, whole(w_router_t)],
        out_specs=[row, row, pl.BlockSpec((1, e_, tm), lambda b, i: (b, 0, i))],
        out_shape=[jax.ShapeDtypeStruct((b_, t_, d_), F32), jax.ShapeDtypeStruct((b_, t_, d_), BF16),
                   jax.ShapeDtypeStruct((b_, e_, t_), F32)],
        compiler_params=_params("arbitrary", "arbitrary", vmem=V7X_VMEM_LIMIT),
        name=name,
    )(*row_args, *halo_args, x, *small_args, gate, gain, shift, scale, w_router_t)


def _conv_in_kernel(x_ref, gain_ref, shift_ref, scale_ref, w_ref, bg_ref, cu_ref):
    d_ = x_ref.shape[-1]
    for rows in _sub_tiles(x_ref.shape[1]):
        h = _modulate(x_ref[0, rows], gain_ref[...], shift_ref[0], scale_ref[0]).astype(BF16)
        part = lambda p: jnp.dot(h, w_ref[:, p * d_:(p + 1) * d_], preferred_element_type=F32)
        bg_ref[0, rows] = part(0).astype(BF16)
        cu_ref[0, rows] = (part(1) * part(2)).astype(BF16)


def _conv_in(x, gain, shift, scale, w_in):
    b_, t_, d_ = x.shape
    tm = min(ROW_TILE, t_)
    row = pl.BlockSpec((1, tm, d_), lambda b, i: (b, i, 0))
    per_sample = pl.BlockSpec((1, 1, d_), lambda b, i: (b, 0, 0))
    return pl.pallas_call(
        _conv_in_kernel,
        grid=(b_, t_ // tm),
        in_specs=[row, pl.BlockSpec((1, d_), lambda b, i: (0, 0)), per_sample, per_sample,
                  pl.BlockSpec(w_in.shape, lambda b, i: (0, 0))],
        out_specs=[row, row],
        out_shape=[jax.ShapeDtypeStruct((b_, t_, d_), BF16)] * 2,
        compiler_params=_params("arbitrary", "arbitrary", vmem=V7X_VMEM_LIMIT),
        name="conv_in",
    )(x, gain, shift, scale, w_in)


def _route_kernel(aff_ref, tri_ref, blockind_ref, slot_ref, base_ref, dense_ref, *, cap):
    aff = aff_ref[0]
    e_, t_ = aff.shape

    def as_float(word):
        return pltpu.bitcast(word, F32)

    def count_ge(th):
        return jnp.sum(jnp.where(aff >= th, 1.0, 0.0), axis=1, keepdims=True)

    def search(_, carry):
        lo, hi = carry
        mid = lo + ((hi - lo + 1) >> 1)
        ok = count_ge(as_float(mid)) >= cap
        return jnp.where(ok, mid, lo), jnp.where(ok, hi, mid - 1)

    lo0 = jnp.zeros((e_, 1), jnp.int32)
    hi0 = jnp.full((e_, 1), 0x7F7FFFFF, jnp.int32)
    kth, _ = lax.fori_loop(0, 32, search, (lo0, hi0))
    above = aff >= as_float(kth + 1)
    tied = jnp.logical_and(aff >= as_float(kth), jnp.logical_not(above))
    need = cap - jnp.sum(jnp.where(above, 1.0, 0.0), axis=1, keepdims=True)
    tri = tri_ref[...]
    tb = tri.shape[0]
    carry_t = jnp.zeros((e_, 1), F32)
    carry_s = jnp.zeros((e_, 1), F32)
    sel_blocks = []
    for j in range(t_ // tb):
        cols = slice(j * tb, (j + 1) * tb)
        tied_j = tied[:, cols]
        ct = jnp.dot(jnp.where(tied_j, 1.0, 0.0).astype(BF16), tri, preferred_element_type=F32) + carry_t
        carry_t = ct[:, tb - 1:tb]
        sel_j = jnp.where(above[:, cols], 1.0, jnp.where(tied_j & (ct <= need), 1.0, 0.0))
        cs = jnp.dot(sel_j.astype(BF16), tri, preferred_element_type=F32) + carry_s
        carry_s = cs[:, tb - 1:tb]
        slot_ref[0, :, cols] = jnp.where(sel_j > 0.0, cs - 1.0, -1.0).astype(jnp.int32)
        sel_blocks.append(sel_j.astype(BF16))
    sel = jnp.concatenate(sel_blocks, axis=1)
    counts = jnp.dot(sel, blockind_ref[...], preferred_element_type=F32)
    base, end = counts[:, :128], counts[:, 128:]
    base_ref[0] = base.astype(jnp.int32)
    span = end - jnp.floor(base * (1.0 / SLOT_ALIGN)) * SLOT_ALIGN
    dense = jnp.max(span, axis=0, keepdims=True) <= DENSE_WINDOW
    dense_ref[0] = jnp.where(dense, 1, 0).astype(jnp.int32)


def _route(aff):
    b_, e_, t_ = aff.shape
    cap = EC_CAPACITY_FACTOR * t_ // e_
    tb = TOKEN_BLOCK
    nb = t_ // tb
    tri = jnp.asarray(np.triu(np.ones((tb, tb), np.float32)), BF16)
    tok, col = np.arange(t_)[:, None], np.arange(128)[None, :]
    blockind = np.concatenate([(tok < col * tb) & (col <= nb), (tok < (col + 1) * tb) & (col < nb)], axis=1)
    slot, base, dense = pl.pallas_call(
        functools.partial(_route_kernel, cap=cap),
        grid=(b_,),
        in_specs=[pl.BlockSpec((1, e_, t_), lambda b: (b, 0, 0)),
                  pl.BlockSpec((tb, tb), lambda b: (0, 0)),
                  pl.BlockSpec((t_, 256), lambda b: (0, 0))],
        out_specs=[pl.BlockSpec((1, e_, t_), lambda b: (b, 0, 0)), pl.BlockSpec((1, e_, 128), lambda b: (b, 0, 0)),
                   pl.BlockSpec((1, 1, 128), lambda b: (b, 0, 0))],
        out_shape=[jax.ShapeDtypeStruct((b_, e_, t_), jnp.int32), jax.ShapeDtypeStruct((b_, e_, 128), jnp.int32),
                   jax.ShapeDtypeStruct((b_, 1, 128), jnp.int32)],
        compiler_params=_params("arbitrary"),
        name="route",
    )(aff, tri, jnp.asarray(blockind.astype(np.float32), BF16))
    return slot, base[:, :, :nb + 1].reshape(-1), dense[:, 0, :nb].reshape(-1)


def _window_plan(tbl_ref, b, e, tb, ne, nb):
    idx = (b * ne + e) * (nb + 1) + tb
    base, end = tbl_ref[idx], tbl_ref[idx + 1]
    start = (base >> 4) << 4
    n_win = jnp.where(end > base, (end - start + SLOT_WINDOW - 1) >> 6, 0)
    return start, n_win


def _dense_start(tbl_ref, b, e, tb, ne, nb, cap):
    base = tbl_ref[(b * ne + e) * (nb + 1) + tb]
    return pl.multiple_of(jnp.minimum((base >> 4) << 4, cap - DENSE_WINDOW), SLOT_ALIGN)


def _gather_kernel(tbl_ref, dense_ref, h_ref, slot_ref, xg_ref, *, ne, nb, cap):
    b, tb = pl.program_id(0), pl.program_id(2)

    @pl.when(tb == 0)
    def _():
        xg_ref[...] = jnp.zeros(xg_ref.shape, xg_ref.dtype)

    h = h_ref[0]
    n_tok = h.shape[0]
    ids0 = lax.broadcasted_iota(jnp.int32, (SLOT_WINDOW, n_tok), 0)
    dense = dense_ref[b * nb + tb] > 0

    @pl.when(dense)
    def _():
        starts = [_dense_start(tbl_ref, b, e, tb, ne, nb, cap) for e in range(ne)]
        ids = lax.broadcasted_iota(jnp.int32, (DENSE_WINDOW, n_tok), 0)
        onehot = jnp.concatenate(
            [jnp.where(ids == slot_ref[0, e:e + 1, :] - starts[e], 1.0, 0.0).astype(BF16) for e in range(ne)], axis=0)
        rows = jnp.dot(onehot, h, preferred_element_type=F32).astype(BF16)
        for e in range(ne):
            win = xg_ref.at[0, e, pl.ds(starts[e], DENSE_WINDOW), :]
            win[...] = win[...] + rows[e * DENSE_WINDOW:(e + 1) * DENSE_WINDOW]

    @pl.when(jnp.logical_not(dense))
    def _():
        for e in range(ne):
            srow = slot_ref[0, e:e + 1, :]
            start, n_win = _window_plan(tbl_ref, b, e, tb, ne, nb)

            def body(k, carry, e=e, srow=srow, start=start):
                lo = start + k * SLOT_WINDOW
                w0 = pl.multiple_of(jnp.minimum(lo, cap - SLOT_WINDOW), SLOT_ALIGN)
                ids = w0 + ids0
                onehot = jnp.where(ids >= lo, jnp.where(ids == srow, 1.0, 0.0), 0.0).astype(BF16)
                rows = jnp.dot(onehot, h, preferred_element_type=F32)
                win = xg_ref.at[0, e, pl.ds(w0, SLOT_WINDOW), :]
                win[...] = win[...] + rows.astype(BF16)
                return carry

            lax.fori_loop(0, n_win, body, 0)


def _gather(h, slot, tbl, dense):
    b_, t_, d_ = h.shape
    e_ = slot.shape[1]
    cap = EC_CAPACITY_FACTOR * t_ // e_
    tb = TOKEN_BLOCK
    nb = t_ // tb
    dh = d_ // 2
    return pl.pallas_call(
        functools.partial(_gather_kernel, ne=e_, nb=nb, cap=cap),
        grid_spec=pltpu.PrefetchScalarGridSpec(
            num_scalar_prefetch=2,
            grid=(b_, 2, nb),
            in_specs=[pl.BlockSpec((1, tb, dh), lambda b, c, i, *_: (b, i, c)),
                      pl.BlockSpec((1, e_, tb), lambda b, c, i, *_: (b, 0, i))],
            out_specs=pl.BlockSpec((1, e_, cap, dh), lambda b, c, i, *_: (b, 0, 0, c)),
        ),
        out_shape=jax.ShapeDtypeStruct((b_, e_, cap, d_), BF16),
        compiler_params=_params("arbitrary", "arbitrary", "arbitrary", vmem=V7X_VMEM_LIMIT),
        name="moe_gather",
    )(tbl, dense, h, slot)


def _expert_kernel(xg_ref, wg_ref, wu_ref, wd_ref, y_ref, acc_ref):
    f = pl.program_id(1)
    wg = wg_ref[0, 0].astype(BF16)
    wu = wu_ref[0, 0].astype(BF16)
    wd = wd_ref[0, 0].astype(BF16)
    n_b, _, cap, _ = xg_ref.shape

    @pl.when(f == 0)
    def _():
        acc_ref[...] = jnp.zeros(acc_ref.shape, F32)

    for b in range(n_b):
        for r in range(cap // FFN_ROWS):
            rows = pl.ds(r * FFN_ROWS, FFN_ROWS)
            acc_rows = pl.ds((b * cap) + r * FFN_ROWS, FFN_ROWS)
            xr = xg_ref[b, 0, rows, :]
            a = jnp.dot(xr, wg, preferred_element_type=F32)
            u = jnp.dot(xr, wu, preferred_element_type=F32)
            part = jnp.dot((_silu(a) * u).astype(BF16), wd, preferred_element_type=F32)
            acc_ref[acc_rows, :] = acc_ref[acc_rows, :] + part

    @pl.when(f == pl.num_programs(1) - 1)
    def _():
        for b in range(n_b):
            y_ref[b, 0] = acc_ref[b * cap:(b + 1) * cap, :].astype(BF16)


def _experts(xg, layer, w_gate, w_up, w_down):
    b_, e_, cap, d_ = xg.shape
    f_ = w_gate.shape[-1]
    ft = min(FFN_TILE, f_)
    return pl.pallas_call(
        _expert_kernel,
        grid=(e_, f_ // ft),
        in_specs=[pl.BlockSpec((b_, 1, cap, d_), lambda e, f: (0, e, 0, 0)),
                  pl.BlockSpec((1, 1, d_, ft), lambda e, f: (layer, e, 0, f)),
                  pl.BlockSpec((1, 1, d_, ft), lambda e, f: (layer, e, 0, f)),
                  pl.BlockSpec((1, 1, ft, d_), lambda e, f: (layer, e, f, 0))],
        out_specs=pl.BlockSpec((b_, 1, cap, d_), lambda e, f: (0, e, 0, 0)),
        out_shape=jax.ShapeDtypeStruct((b_, e_, cap, d_), BF16),
        scratch_shapes=[pltpu.VMEM((b_ * cap, d_), F32)],
        compiler_params=_params("arbitrary", "arbitrary", vmem=V7X_VMEM_LIMIT),
        name="moe_experts",
    )(xg, w_gate, w_up, w_down)


def _combine_kernel(tbl_ref, dense_ref, y_ref, slot_ref, aff_ref, x_ref, gate_ref, out_ref, acc_ref, *, ne, nb, cap):
    b, tb = pl.program_id(0), pl.program_id(2)
    n_tok = acc_ref.shape[0]
    ids0 = lax.broadcasted_iota(jnp.int32, (SLOT_WINDOW, n_tok), 0)
    dense = dense_ref[b * nb + tb] > 0

    @pl.when(dense)
    def _():
        starts = [_dense_start(tbl_ref, b, e, tb, ne, nb, cap) for e in range(ne)]
        ids = lax.broadcasted_iota(jnp.int32, (DENSE_WINDOW, n_tok), 0)
        weights = jnp.concatenate(
            [jnp.where(ids == slot_ref[0, e:e + 1, :] - starts[e], aff_ref[0, e:e + 1, :], 0.0).astype(BF16)
             for e in range(ne)], axis=0)
        yw = jnp.concatenate([y_ref[0, e, pl.ds(starts[e], DENSE_WINDOW), :] for e in range(ne)], axis=0)
        acc_ref[...] = lax.dot_general(weights, yw, TN_DIMS, preferred_element_type=F32)

    @pl.when(jnp.logical_not(dense))
    def _():
        acc_ref[...] = jnp.zeros(acc_ref.shape, F32)
        for e in range(ne):
            srow = slot_ref[0, e:e + 1, :]
            grow = aff_ref[0, e:e + 1, :]
            start, n_win = _window_plan(tbl_ref, b, e, tb, ne, nb)

            def body(k, carry, e=e, srow=srow, grow=grow, start=start):
                lo = start + k * SLOT_WINDOW
                w0 = pl.multiple_of(jnp.minimum(lo, cap - SLOT_WINDOW), SLOT_ALIGN)
                ids = w0 + ids0
                weights = jnp.where(ids >= lo, jnp.where(ids == srow, grow, 0.0), 0.0).astype(BF16)
                yw = y_ref[0, e, pl.ds(w0, SLOT_WINDOW), :]
                acc_ref[...] = acc_ref[...] + lax.dot_general(weights, yw, TN_DIMS, preferred_element_type=F32)
                return carry

            lax.fori_loop(0, n_win, body, 0)

    out_ref[0] = x_ref[0] + gate_ref[0] * acc_ref[...]


def _combine(y, slot, aff, tbl, dense, x, gate):
    b_, t_, d_ = x.shape
    e_, cap = y.shape[1], y.shape[2]
    tb = TOKEN_BLOCK
    nb = t_ // tb
    dh = d_ // 2
    route_spec = pl.BlockSpec((1, e_, tb), lambda b, c, i, *_: (b, 0, i))
    row = pl.BlockSpec((1, tb, dh), lambda b, c, i, *_: (b, i, c))
    return pl.pallas_call(
        functools.partial(_combine_kernel, ne=e_, nb=nb, cap=cap),
        grid_spec=pltpu.PrefetchScalarGridSpec(
            num_scalar_prefetch=2,
            grid=(b_, 2, nb),
            in_specs=[pl.BlockSpec((1, e_, cap, dh), lambda b, c, i, *_: (b, 0, 0, c)),
                      route_spec, route_spec, row,
                      pl.BlockSpec((1, 1, dh), lambda b, c, i, *_: (b, 0, c))],
            out_specs=row,
            scratch_shapes=[pltpu.VMEM((tb, dh), F32)],
        ),
        out_shape=jax.ShapeDtypeStruct((b_, t_, d_), F32),
        compiler_params=_params("arbitrary", "arbitrary", "arbitrary", vmem=V7X_VMEM_LIMIT),
        name="moe_combine",
    )(tbl, dense, y, slot, aff, x, gate)


def _moe(x, h, aff, gate, layer, w_gate, w_up, w_down):
    slot, tbl, dense = _route(aff)
    xg = _gather(h, slot, tbl, dense)
    y = _experts(xg, layer, w_gate, w_up, w_down)
    return _combine(y, slot, aff, tbl, dense, x, gate)


def _final_norm_kernel(x_ref, gain_ref, out_ref):
    x = x_ref[0]
    out_ref[0] = x * lax.rsqrt(jnp.mean(x * x, axis=-1, keepdims=True) + EPS) * gain_ref[...]


def _final_norm(x, gain):
    b_, t_, d_ = x.shape
    tm = ROW_TILE
    row = pl.BlockSpec((1, tm, d_), lambda b, i: (b, i, 0))
    return pl.pallas_call(
        _final_norm_kernel,
        grid=(b_, t_ // tm),
        in_specs=[row, pl.BlockSpec((1, d_), lambda b, i: (0, 0))],
        out_specs=row,
        out_shape=jax.ShapeDtypeStruct((b_, t_, d_), F32),
        compiler_params=_params("arbitrary", "arbitrary"),
        name="final_norm",
    )(x, gain)


def kernel(x, c, ctx, c_ctx, ada_w, ada_b, norm_mix, norm_ffn, norm_final, hg_w_in, hg_lb_logits, hg_norm, hg_w_out, sc_w_in, sc_conv, sc_w_out, moe_router, moe_w_gate, moe_w_up, moe_w_down):
    b_, t_, d_ = x.shape
    depth = ada_w.shape[0]
    n_ada = ada_w.shape[-1] // d_
    n_heads = d_ // HEAD_DIM
    assert depth == 2 and n_ada == 6 and b_ + 1 <= 8
    assert t_ % GLA_CHUNK == 0 and ctx.shape[1] % GLA_CHUNK == 0 and ROW_TILE % GRID_W == 0

    cond = jnp.concatenate([c, c_ctx[None], jnp.zeros((8 - b_ - 1, d_), F32)], axis=0)
    mod = _ada_vectors(cond, ada_w, ada_b, n_ada)
    vec = lambda i, j: mod[i, j, :b_][:, None, :]
    cvec = lambda i, j: mod[i, j, b_][None, None, :]
    rowtab, coltab, lower = _tables(hg_lb_logits, t_)
    row_of = lambda a, i: a[i][None, :]
    router_t = lambda i: jnp.swapaxes(moe_router[i], 0, 1)

    w_in = hg_w_in[0].astype(BF16)
    lb0 = row_of(lower, 0)
    gain0 = row_of(norm_mix, 0)
    qc, vc, lfc_f, lfc_b = _hgrn_in(ctx, None, gain0, cvec(0, 0), cvec(0, 1), lb0, w_in, with_gate=False)
    zeros = jnp.zeros((b_, n_heads, HEAD_DIM, HEAD_DIM), F32)
    _, _, s_f, s_b = _gla_bidir(qc, vc, lfc_f, lfc_b, zeros, zeros)
    x0, q, v, lf_f, lf_b, g = _hgrn_in(x, (rowtab, coltab), gain0, vec(0, 0), vec(0, 1), lb0, w_in, with_gate=True)
    o_f, o_b, _, _ = _gla_bidir(q, v, lf_f, lf_b, s_f, s_b)
    x1, h, aff = _mixer_out(_hgrn_out_kernel, "hgrn_out", [o_f, o_b, g], [], x0,
                            [row_of(hg_norm, 0), hg_w_out[0].astype(BF16)],
                            vec(0, 2), row_of(norm_ffn, 0), vec(0, 3), vec(0, 4), router_t(0))
    x2 = _moe(x1, h, aff, vec(0, 5), 0, moe_w_gate, moe_w_up, moe_w_down)

    bg, cu = _conv_in(x2, row_of(norm_mix, 1), vec(1, 0), vec(1, 1), sc_w_in[0].astype(BF16))
    x3, h, aff = _mixer_out(_conv_out_kernel, "conv_out", [cu, bg], [cu, cu], x2,
                            [sc_conv[0], sc_w_out[0].astype(BF16)],
                            vec(1, 2), row_of(norm_ffn, 1), vec(1, 3), vec(1, 4), router_t(1))
    x4 = _moe(x3, h, aff, vec(1, 5), 1, moe_w_gate, moe_w_up, moe_w_down)
    return _final_norm(x4, norm_final[None, :])
```

```python
import functools
import math

import numpy as np
import jax
import jax.numpy as jnp
from jax import lax
from jax.experimental import pallas as pl
from jax.experimental.pallas import tpu as pltpu

F32 = jnp.float32
BF16 = jnp.bfloat16

EPS = 1e-6
POS_TEMP = 10000.0
GRID_W = 64
HEAD_DIM = 128
EC_CAPACITY_FACTOR = 2
GLA_CHUNK = 256
GLA_LEVELS = 8
GLA_HEADS_PER_STEP = 2
GLA_MERGED = 5
GLA_MAX_EXPONENT = 80.0
ROW_TILE = 512
SUB_ROWS = 256
TOKEN_BLOCK = 256
SLOT_WINDOW = 64
DENSE_WINDOW = 128
SLOT_ALIGN = 16
FFN_TILE = 512
FFN_ROWS = 512
V7X_VMEM_LIMIT = 56 * 1024 * 1024

NT_DIMS = (((1,), (1,)), ((), ()))
TN_DIMS = (((0,), (0,)), ((), ()))


def _params(*sem, vmem=None):
    return pltpu.CompilerParams(dimension_semantics=sem, vmem_limit_bytes=vmem)


def _split3(x):
    hi = x.astype(BF16)
    r = x - hi.astype(F32)
    mid = r.astype(BF16)
    lo = (r - mid.astype(F32)).astype(BF16)
    return hi, mid, lo


def _dot_f32(a, b, dims):
    a0, a1, a2 = _split3(a)
    b0, b1, b2 = _split3(b)
    d = lambda x, y: lax.dot_general(x, y, dims, preferred_element_type=F32)
    return ((d(a0, b2) + d(a2, b0) + d(a1, b1)) + (d(a0, b1) + d(a1, b0))) + d(a0, b0)


def _dot_bf16x3(a, b, dims):
    a0, b0 = a.astype(BF16), b.astype(BF16)
    a1 = (a - a0.astype(F32)).astype(BF16)
    b1 = (b - b0.astype(F32)).astype(BF16)
    d = lambda x, y: lax.dot_general(x, y, dims, preferred_element_type=F32)
    return (d(a0, b1) + d(a1, b0)) + d(a0, b0)


def _sub_tiles(n_rows):
    sub = min(SUB_ROWS, n_rows)
    return [slice(s, s + sub) for s in range(0, n_rows, sub)]


def _sigmoid(x):
    return 1.0 / (1.0 + jnp.exp(-x))


def _silu(x):
    return x * _sigmoid(x)


def _modulate(x, gain, shift, scale):
    y = x * lax.rsqrt(jnp.mean(x * x, axis=-1, keepdims=True) + EPS)
    return (y * gain) * (1.0 + scale) + shift


def _ada_kernel(cond_ref, w_ref, b_ref, out_ref):
    s = _silu(cond_ref[...])
    out_ref[0, 0] = _dot_f32(s, w_ref[0], (((1,), (0,)), ((), ()))) + b_ref[0, 0]


def _ada_vectors(cond, ada_w, ada_b, n_ada):
    depth, d_, _ = ada_w.shape
    return pl.pallas_call(
        _ada_kernel,
        grid=(depth, n_ada),
        in_specs=[pl.BlockSpec((8, d_), lambda i, j: (0, 0)),
                  pl.BlockSpec((1, d_, d_), lambda i, j: (i, 0, j)),
                  pl.BlockSpec((1, 1, 1, d_), lambda i, j: (i, j, 0, 0))],
        out_specs=pl.BlockSpec((1, 1, 8, d_), lambda i, j: (i, j, 0, 0)),
        out_shape=jax.ShapeDtypeStruct((depth, n_ada, 8, d_), F32),
        compiler_params=_params("arbitrary", "arbitrary"),
        name="ada_vectors",
    )(cond, ada_w, ada_b.reshape(depth, n_ada, 1, d_))


def _tables_kernel(lb_logits_ref, rowtab_ref, coltab_ref, lb_ref, *, n_freq):
    def table(n_pos):
        p = lax.broadcasted_iota(jnp.int32, (n_pos, n_freq), 0).astype(F32)
        j = lax.broadcasted_iota(jnp.int32, (n_pos, n_freq), 1).astype(F32)
        omega = jnp.exp(j * (-math.log(POS_TEMP) / n_freq))
        ang = p * omega
        return jnp.concatenate([jnp.sin(ang), jnp.cos(ang)], axis=-1)

    rowtab_ref[...] = table(rowtab_ref.shape[0])
    coltab_ref[...] = table(coltab_ref.shape[0])
    logits = lb_logits_ref[...]
    e = jnp.exp(logits - jnp.max(logits, axis=0, keepdims=True))
    sm = e / jnp.sum(e, axis=0, keepdims=True)
    acc = sm[0:1]
    lb_ref[0:1] = acc
    for i in range(1, lb_ref.shape[0]):
        acc = acc + sm[i:i + 1]
        lb_ref[i:i + 1] = acc


def _tables(lb_logits, n_tokens):
    n_lb, d_ = lb_logits.shape
    n_freq = d_ // 4
    rows = n_tokens // GRID_W
    return pl.pallas_call(
        functools.partial(_tables_kernel, n_freq=n_freq),
        out_shape=[jax.ShapeDtypeStruct((rows, 2 * n_freq), F32),
                   jax.ShapeDtypeStruct((GRID_W, 2 * n_freq), F32),
                   jax.ShapeDtypeStruct((n_lb, d_), F32)],
        name="pos_tables",
    )(lb_logits)


def _hgrn_in_kernel(*refs, with_pos, with_gate):
    it = iter(refs)
    x_ref = next(it)
    if with_pos:
        rowtab_ref, coltab_ref = next(it), next(it)
    gain_ref, shift_ref, scale_ref, lb_ref, w_ref = next(it), next(it), next(it), next(it), next(it)
    if with_pos:
        x0_ref = next(it)
    q_ref, v_ref, lff_ref, lfb_ref = next(it), next(it), next(it), next(it)
    g_ref = next(it) if with_gate else None

    d_ = x_ref.shape[-1]
    lb = lb_ref[...]
    for rows in _sub_tiles(x_ref.shape[1]):
        x = x_ref[0, rows]
        if with_pos:
            grid_rows = range(rows.start // GRID_W, rows.stop // GRID_W)
            pos_row = jnp.concatenate(
                [jnp.broadcast_to(rowtab_ref[0, r:r + 1, :], (GRID_W, rowtab_ref.shape[-1])) for r in grid_rows], axis=0)
            pos_col = jnp.concatenate([coltab_ref[...]] * len(grid_rows), axis=0)
            x = x + jnp.concatenate([pos_row, pos_col], axis=-1)
            x0_ref[0, rows] = x
        h = _modulate(x, gain_ref[...], shift_ref[0], scale_ref[0]).astype(BF16)
        part = lambda p: jnp.dot(h, w_ref[:, p * d_:(p + 1) * d_], preferred_element_type=F32)
        q_ref[0, rows] = (part(0) * HEAD_DIM ** -0.5).astype(BF16)
        v_ref[0, rows] = part(1).astype(BF16)
        lff_ref[0, rows] = jnp.log(lb + (1.0 - lb) * _sigmoid(part(2)))
        lfb_ref[0, rows] = jnp.log(lb + (1.0 - lb) * _sigmoid(part(3)))
        if with_gate:
            g_ref[0, rows] = part(4).astype(BF16)


def _hgrn_in(x, tabs, gain, shift, scale, lb, w_in, *, with_gate):
    b_, t_, d_ = x.shape
    tm = min(ROW_TILE, t_)
    with_pos = tabs is not None
    per_sample = lambda a: pl.BlockSpec((1, 1, d_), (lambda b, i: (b, 0, 0)) if a.shape[0] > 1 else (lambda b, i: (0, 0, 0)))
    row = pl.BlockSpec((1, tm, d_), lambda b, i: (b, i, 0))
    vec = pl.BlockSpec((1, d_), lambda b, i: (0, 0))
    args, in_specs = [x], [row]
    if with_pos:
        rowtab, coltab = tabs
        rows_per_tile = tm // GRID_W
        args += [rowtab.reshape(rowtab.shape[0] // rows_per_tile, rows_per_tile, rowtab.shape[1]), coltab]
        in_specs += [pl.BlockSpec((1, rows_per_tile, rowtab.shape[1]), lambda b, i: (i, 0, 0)),
                     pl.BlockSpec(coltab.shape, lambda b, i: (0, 0))]
    args += [gain, shift, scale, lb, w_in]
    in_specs += [vec, per_sample(shift), per_sample(scale), vec, pl.BlockSpec(w_in.shape, lambda b, i: (0, 0))]
    out_shape, out_specs = [], []
    if with_pos:
        out_shape.append(jax.ShapeDtypeStruct((b_, t_, d_), F32))
        out_specs.append(row)
    out_shape += [jax.ShapeDtypeStruct((b_, t_, d_), BF16)] * 2 + [jax.ShapeDtypeStruct((b_, t_, d_), F32)] * 2
    out_specs += [row] * 4
    if with_gate:
        out_shape.append(jax.ShapeDtypeStruct((b_, t_, d_), BF16))
        out_specs.append(row)
    return pl.pallas_call(
        functools.partial(_hgrn_in_kernel, with_pos=with_pos, with_gate=with_gate),
        grid=(b_, t_ // tm),
        in_specs=in_specs, out_specs=out_specs, out_shape=out_shape,
        compiler_params=_params("arbitrary", "arbitrary", vmem=V7X_VMEM_LIMIT),
        name="hgrn_in_latent" if with_pos else "hgrn_in_context",
    )(*args)


def _gla_consts(reverse):
    c = GLA_CHUNK
    idx = np.arange(c)
    rank = (c - 1 - idx) if reverse else idx
    tri = (rank[None, :] <= rank[:, None]).astype(np.float32)
    hc = c // 2
    hrank = rank[:hc] - rank[:hc].min()
    lvl = np.full((hc, hc), -1, np.int32)
    rt, rs = hrank[:, None], hrank[None, :]
    lvl[rt == rs] = 0
    for level in range(1, GLA_LEVELS):
        blk, half = 1 << level, 1 << (level - 1)
        lvl[(rt // blk == rs // blk) & ((rt % blk) >= half) & ((rs % blk) < half)] = level
    return jnp.asarray(tri, BF16), jnp.asarray(lvl)


def _later_group(level, reverse, group):
    rank = (GLA_CHUNK - 1 - 8 * group) if reverse else 8 * group
    return (rank % (1 << level)) >= (1 << (level - 1))


def _boundary_rows(level, reverse):
    c = GLA_CHUNK
    blk, half = 1 << level, 1 << (level - 1)
    rows = []
    for i in range(c):
        rank = (c - 1 - i) if reverse else i
        brank = (rank // blk) * blk + half - 1
        rows.append((c - 1 - brank) if reverse else brank)
    return rows


def _gla_low_levels(q, kk, lf, cum, lvl, bcast, halves, reverse):
    ng = q.shape[0] // 8
    qb, kb = q.astype(BF16), kk.astype(BF16)
    tiles = [jnp.where(lvl == 0, lax.dot_general(qb[hs], kb[hs], NT_DIMS, preferred_element_type=F32), 0.0)
             for hs in halves]
    row8 = lax.broadcasted_iota(jnp.int32, (8, HEAD_DIM), 0)
    rank8 = (7 - row8) if reverse else row8
    for level in range(1, 4):
        sgn8 = jnp.where(((rank8 >> (level - 1)) & 1) == 1, 1.0, -1.0)
        sgn = jnp.concatenate([sgn8] * ng, axis=0)
        later = sgn > 0.0
        if level == 1:
            g = jnp.where(later, lf, 0.0)
        else:
            brow = _boundary_rows(level, reverse)
            pieces = []
            for grp in range(ng):
                first = bcast(brow[8 * grp])
                pieces.append(jnp.where(row8 < 4, first, bcast(brow[8 * grp + 7])) if level == 2 else first)
            g = (cum - jnp.concatenate(pieces, axis=0)) * sgn
        xe = (jnp.where(later, q, kk) * jnp.exp(g)).astype(BF16)
        for h, hs in enumerate(halves):
            s = lax.dot_general(xe[hs], xe[hs], NT_DIMS, preferred_element_type=F32)
            tiles[h] = jnp.where(lvl == level, s, tiles[h])
    return tiles


def _gla_chunk(q_ref, v_ref, lf_ref, lanes, st_ref, cum_ref, tri_ref, lvl_ref, o_ref, reverse, merged):
    c = GLA_CHUNK
    hc, ng = c // 2, c // 8
    halves = (slice(0, hc), slice(hc, c))
    lf = lf_ref[0, :, lanes]
    q = q_ref[0, :, lanes].astype(F32)
    v = v_ref[0, :, lanes]
    kk = 1.0 - jnp.exp(lf)
    hi = lf.astype(BF16)
    lo = (lf - hi.astype(F32)).astype(BF16)
    two = jnp.dot(tri_ref[...], jnp.concatenate([hi, lo], axis=1), preferred_element_type=F32)
    cum = two[:, HEAD_DIM:] + two[:, :HEAD_DIM]
    cum_ref[...] = cum
    lvl = lvl_ref[...]
    bcast = lambda r: jnp.broadcast_to(cum_ref[r:r + 1, :], (8, HEAD_DIM))
    groups = lambda a: [a[8 * i:8 * i + 8] for i in range(a.shape[0] // 8)]

    if merged:
        blk = 1 << merged
        cache, pieces = {}, []
        for grp in range(ng):
            rank = (c - 1 - 8 * grp) if reverse else 8 * grp
            first = (rank // blk) * blk
            row = (c - 1 - first) if reverse else first
            pieces.append(cache.setdefault(row, bcast(row)))
        d = cum - jnp.concatenate(pieces, axis=0)
        xq = (q * jnp.exp(d)).astype(BF16)
        xk = (kk * jnp.exp(-d)).astype(BF16)
        inside = jnp.logical_and(lvl >= 0, lvl <= merged)
        tiles = [jnp.where(inside, lax.dot_general(xq[hs], xk[hs], NT_DIMS, preferred_element_type=F32), 0.0)
                 for hs in halves]
    else:
        tiles = _gla_low_levels(q, kk, lf, cum, lvl, bcast, halves, reverse)

    tile_rows = [groups(t) for t in tiles]
    lvl_rows = groups(lvl)
    q_rows, k_rows, cum_rows = groups(q), groups(kk), groups(cum)
    for level in range(max(4, merged + 1), GLA_LEVELS):
        brow = _boundary_rows(level, reverse)
        cache = {}
        g_rows, x_rows = [], []
        for grp in range(ng):
            cb = cache.setdefault(brow[8 * grp], bcast(brow[8 * grp]))
            later = _later_group(level, reverse, grp)
            g_rows.append(cum_rows[grp] - cb if later else cb - cum_rows[grp])
            x_rows.append(q_rows[grp] if later else k_rows[grp])
        xe = jnp.concatenate(x_rows, axis=0) * jnp.exp(jnp.concatenate(g_rows, axis=0))
        xe_rows = groups(xe)
        for h, hs in enumerate(halves):
            later_groups = [grp for grp in range(h * ng // 2, (h + 1) * ng // 2) if _later_group(level, reverse, grp)]
            qc = jnp.concatenate([xe_rows[grp] for grp in later_groups], axis=0).astype(BF16)
            s = lax.dot_general(qc, xe[hs].astype(BF16), NT_DIMS, preferred_element_type=F32)
            for i, grp in enumerate(later_groups):
                local = grp - h * ng // 2
                tile_rows[h][local] = jnp.where(lvl_rows[local] == level, s[8 * i:8 * i + 8], tile_rows[h][local])

    early, late = (1, 0) if reverse else (0, 1)
    cb = cum_ref[_boundary_rows(GLA_LEVELS, reverse)[0]:_boundary_rows(GLA_LEVELS, reverse)[0] + 1, :]
    ql = (q[halves[late]] * jnp.exp(cum[halves[late]] - cb)).astype(BF16)
    ke = (kk[halves[early]] * jnp.exp(cb - cum[halves[early]])).astype(BF16)
    cross = lax.dot_general(ql, ke, NT_DIMS, preferred_element_type=F32)
    t_a, t_b = (jnp.concatenate(rows, axis=0) for rows in tile_rows)
    zero = jnp.zeros((hc, hc), F32)
    if reverse:
        scores = jnp.concatenate([jnp.concatenate([t_a, cross], axis=1), jnp.concatenate([zero, t_b], axis=1)], axis=0)
    else:
        scores = jnp.concatenate([jnp.concatenate([t_a, zero], axis=1), jnp.concatenate([cross, t_b], axis=1)], axis=0)

    st = st_ref[...]
    qe = (q * jnp.exp(cum)).astype(BF16)
    o = jnp.dot(scores.astype(BF16), v, preferred_element_type=F32)
    o_ref[0, :, lanes] = o + lax.dot_general(qe, st.astype(BF16), NT_DIMS, preferred_element_type=F32)
    last_row = 0 if reverse else c - 1
    last = cum_ref[last_row:last_row + 1, :]
    ke_all = (kk * jnp.exp(last - cum)).astype(BF16)
    st_ref[...] = st * jnp.exp(last) + lax.dot_general(v, ke_all, TN_DIMS, preferred_element_type=F32)


def _gla_kernel(qf_ref, vf_ref, lff_ref, qb_ref, vb_ref, lfb_ref, s0f_ref, s0b_ref,
                trif_ref, lvlf_ref, trib_ref, lvlb_ref,
                of_ref, ob_ref, sff_ref, sfb_ref, stf_ref, stb_ref, cumf_ref, cumb_ref, *, merged):
    j = pl.program_id(2)

    @pl.when(j == 0)
    def _():
        stf_ref[...] = s0f_ref[0]
        stb_ref[...] = s0b_ref[0]

    for k in range(stf_ref.shape[0]):
        lanes = slice(k * HEAD_DIM, (k + 1) * HEAD_DIM)
        _gla_chunk(qf_ref, vf_ref, lff_ref, lanes, stf_ref.at[k], cumf_ref.at[k], trif_ref, lvlf_ref, of_ref,
                   False, merged)
        _gla_chunk(qb_ref, vb_ref, lfb_ref, lanes, stb_ref.at[k], cumb_ref.at[k], trib_ref, lvlb_ref, ob_ref,
                   True, merged)

    @pl.when(j == pl.num_programs(2) - 1)
    def _():
        sff_ref[0] = stf_ref[...]
        sfb_ref[0] = stb_ref[...]


def _gla_bidir(q, v, lf_f, lf_b, s0f, s0b, lb):
    worst = (2 ** GLA_MERGED - 1) * jnp.max(-jnp.log(lb))
    run = lambda merged: (lambda *a: _gla_call(*a, merged=merged))
    return lax.cond(worst < GLA_MAX_EXPONENT, run(GLA_MERGED), run(0), q, v, lf_f, lf_b, s0f, s0b)


def _gla_call(q, v, lf_f, lf_b, s0f, s0b, *, merged):
    b_, t_, d_ = q.shape
    h_ = d_ // HEAD_DIM
    hp = GLA_HEADS_PER_STEP
    c = GLA_CHUNK
    n = t_ // c
    fwd = lambda b, h, j: (b, j, h)
    bwd = lambda b, h, j: (b, n - 1 - j, h)
    st = lambda b, h, j: (b, h, 0, 0)
    const = lambda b, h, j: (0, 0)
    blk = lambda im: pl.BlockSpec((1, c, hp * HEAD_DIM), im)
    st_spec = pl.BlockSpec((1, hp, HEAD_DIM, HEAD_DIM), st)
    cspecs = [pl.BlockSpec((c, c), const), pl.BlockSpec((c // 2, c // 2), const)]
    state = pltpu.VMEM((hp, HEAD_DIM, HEAD_DIM), F32)
    cum = pltpu.VMEM((hp, c, HEAD_DIM), F32)
    return pl.pallas_call(
        functools.partial(_gla_kernel, merged=merged),
        grid=(b_, h_ // hp, n),
        in_specs=[blk(fwd), blk(fwd), blk(fwd), blk(bwd), blk(bwd), blk(bwd), st_spec, st_spec] + cspecs + cspecs,
        out_specs=[blk(fwd), blk(bwd), st_spec, st_spec],
        out_shape=[jax.ShapeDtypeStruct((b_, t_, d_), F32)] * 2
                  + [jax.ShapeDtypeStruct((b_, h_, HEAD_DIM, HEAD_DIM), F32)] * 2,
        scratch_shapes=[state, state, cum, cum],
        compiler_params=_params("arbitrary", "arbitrary", "arbitrary"),
        name="gla_merged" if merged else "gla_split",
    )(q, v, lf_f, q, v, lf_b, s0f, s0b, *_gla_consts(False), *_gla_consts(True))


def _mixer_epilogue(y, rows, x_ref, w_ref, gate_ref, gain_ref, shift_ref, scale_ref, wr_ref, x1_ref, h_ref, aff_ref):
    y = jnp.dot(y.astype(BF16), w_ref[...], preferred_element_type=F32)
    x1 = x_ref[0, rows] + gate_ref[0] * y
    x1_ref[0, rows] = x1
    hf = _modulate(x1, gain_ref[...], shift_ref[0], scale_ref[0])
    h_ref[0, rows] = hf.astype(BF16)
    logits = _dot_bf16x3(wr_ref[...], hf, NT_DIMS)
    e = jnp.exp(logits - jnp.max(logits, axis=0, keepdims=True))
    aff_ref[0, :, rows] = e / jnp.sum(e, axis=0, keepdims=True)


def _hgrn_out_kernel(of_ref, ob_ref, g_ref, x_ref, hnorm_ref, w_ref, *rest):
    hn = hnorm_ref[...]
    for rows in _sub_tiles(x_ref.shape[1]):
        o = of_ref[0, rows] + ob_ref[0, rows]
        heads = []
        for h in range(o.shape[-1] // HEAD_DIM):
            oh = o[:, h * HEAD_DIM:(h + 1) * HEAD_DIM]
            heads.append(oh * lax.rsqrt(jnp.mean(oh * oh, axis=-1, keepdims=True) + EPS) * hn)
        y = jnp.concatenate(heads, axis=-1) * _silu(g_ref[0, rows].astype(F32))
        _mixer_epilogue(y, rows, x_ref, w_ref, *rest)


def _conv_out_kernel(cu_ref, bg_ref, prev_ref, next_ref, x_ref, wc_ref, w_ref, *rest):
    i = pl.program_id(1)
    cu = cu_ref[0].astype(F32)
    tm = cu.shape[0]
    rid = lax.broadcasted_iota(jnp.int32, cu.shape, 0)
    before = jnp.where(i == 0, 0.0, prev_ref[0, 7:8, :].astype(F32))
    after = jnp.where(i == pl.num_programs(1) - 1, 0.0, next_ref[0, 0:1, :].astype(F32))
    left = jnp.where(rid == 0, before, pltpu.roll(cu, 1, 0))
    right = jnp.where(rid == tm - 1, after, pltpu.roll(cu, tm - 1, 0))
    wc = wc_ref[...]
    conv = left * wc[0:1] + cu * wc[1:2] + right * wc[2:3]
    for rows in _sub_tiles(tm):
        _mixer_epilogue(bg_ref[0, rows].astype(F32) * conv[rows], rows, x_ref, w_ref, *rest)


def _mixer_out(kernel, name, row_args, halo_args, x, small_args, gate, gain, shift, scale, w_router_t):
    b_, t_, d_ = x.shape
    e_ = w_router_t.shape[0]
    tm = min(ROW_TILE, t_)
    row = pl.BlockSpec((1, tm, d_), lambda b, i: (b, i, 0))
    per_sample = pl.BlockSpec((1, 1, d_), lambda b, i: (b, 0, 0))
    whole = lambda a: pl.BlockSpec(a.shape, lambda b, i: (0,) * a.ndim)
    n8 = t_ // 8
    halo_specs = [pl.BlockSpec((1, 8, d_), lambda b, i: (b, jnp.maximum(i * (tm // 8) - 1, 0), 0)),
                  pl.BlockSpec((1, 8, d_), lambda b, i: (b, jnp.minimum((i + 1) * (tm // 8), n8 - 1), 0))]
    return pl.pallas_call(
        kernel,
        grid=(b_, t_ // tm),
        in_specs=[row] * len(row_args) + halo_specs[:len(halo_args)] + [row]
                 + [whole(a) for a in small_args] + [per_sample, whole(gain), per_sample, per_sample, whole(w_router_t)],
        out_specs=[row, row, pl.BlockSpec((1, e_, tm), lambda b, i: (b, 0, i))],
        out_shape=[jax.ShapeDtypeStruct((b_, t_, d_), F32), jax.ShapeDtypeStruct((b_, t_, d_), BF16),
                   jax.ShapeDtypeStruct((b_, e_, t_), F32)],
        compiler_params=_params("arbitrary", "arbitrary", vmem=V7X_VMEM_LIMIT),
        name=name,
    )(*row_args, *halo_args, x, *small_args, gate, gain, shift, scale, w_router_t)


def _conv_in_kernel(x_ref, gain_ref, shift_ref, scale_ref, w_ref, bg_ref, cu_ref):
    d_ = x_ref.shape[-1]
    for rows in _sub_tiles(x_ref.shape[1]):
        h = _modulate(x_ref[0, rows], gain_ref[...], shift_ref[0], scale_ref[0]).astype(BF16)
        part = lambda p: jnp.dot(h, w_ref[:, p * d_:(p + 1) * d_], preferred_element_type=F32)
        bg_ref[0, rows] = part(0).astype(BF16)
        cu_ref[0, rows] = (part(1) * part(2)).astype(BF16)


def _conv_in(x, gain, shift, scale, w_in):
    b_, t_, d_ = x.shape
    tm = min(ROW_TILE, t_)
    row = pl.BlockSpec((1, tm, d_), lambda b, i: (b, i, 0))
    per_sample = pl.BlockSpec((1, 1, d_), lambda b, i: (b, 0, 0))
    return pl.pallas_call(
        _conv_in_kernel,
        grid=(b_, t_ // tm),
        in_specs=[row, pl.BlockSpec((1, d_), lambda b, i: (0, 0)), per_sample, per_sample,
                  pl.BlockSpec(w_in.shape, lambda b, i: (0, 0))],
        out_specs=[row, row],
        out_shape=[jax.ShapeDtypeStruct((b_, t_, d_), BF16)] * 2,
        compiler_params=_params("arbitrary", "arbitrary", vmem=V7X_VMEM_LIMIT),
        name="conv_in",
    )(x, gain, shift, scale, w_in)


def _route_kernel(aff_ref, tri_ref, blockind_ref, slot_ref, base_ref, dense_ref, *, cap):
    aff = aff_ref[0]
    e_, t_ = aff.shape

    def as_float(word):
        return pltpu.bitcast(word, F32)

    def count_ge(th):
        return jnp.sum(jnp.where(aff >= th, 1.0, 0.0), axis=1, keepdims=True)

    def search(_, carry):
        lo, hi = carry
        mid = lo + ((hi - lo + 1) >> 1)
        ok = count_ge(as_float(mid)) >= cap
        return jnp.where(ok, mid, lo), jnp.where(ok, hi, mid - 1)

    lo0 = jnp.zeros((e_, 1), jnp.int32)
    hi0 = jnp.full((e_, 1), 0x7F7FFFFF, jnp.int32)
    kth, _ = lax.fori_loop(0, 32, search, (lo0, hi0))
    above = aff >= as_float(kth + 1)
    tied = jnp.logical_and(aff >= as_float(kth), jnp.logical_not(above))
    need = cap - jnp.sum(jnp.where(above, 1.0, 0.0), axis=1, keepdims=True)
    tri = tri_ref[...]
    tb = tri.shape[0]
    carry_t = jnp.zeros((e_, 1), F32)
    carry_s = jnp.zeros((e_, 1), F32)
    sel_blocks = []
    for j in range(t_ // tb):
        cols = slice(j * tb, (j + 1) * tb)
        tied_j = tied[:, cols]
        ct = jnp.dot(jnp.where(tied_j, 1.0, 0.0).astype(BF16), tri, preferred_element_type=F32) + carry_t
        carry_t = ct[:, tb - 1:tb]
        sel_j = jnp.where(above[:, cols], 1.0, jnp.where(tied_j & (ct <= need), 1.0, 0.0))
        cs = jnp.dot(sel_j.astype(BF16), tri, preferred_element_type=F32) + carry_s
        carry_s = cs[:, tb - 1:tb]
        slot_ref[0, :, cols] = jnp.where(sel_j > 0.0, cs - 1.0, -1.0).astype(jnp.int32)
        sel_blocks.append(sel_j.astype(BF16))
    sel = jnp.concatenate(sel_blocks, axis=1)
    counts = jnp.dot(sel, blockind_ref[...], preferred_element_type=F32)
    base, end = counts[:, :128], counts[:, 128:]
    base_ref[0] = base.astype(jnp.int32)
    span = end - jnp.floor(base * (1.0 / SLOT_ALIGN)) * SLOT_ALIGN
    dense = jnp.max(span, axis=0, keepdims=True) <= DENSE_WINDOW
    dense_ref[0] = jnp.where(dense, 1, 0).astype(jnp.int32)


def _route(aff):
    b_, e_, t_ = aff.shape
    cap = EC_CAPACITY_FACTOR * t_ // e_
    tb = TOKEN_BLOCK
    nb = t_ // tb
    tri = jnp.asarray(np.triu(np.ones((tb, tb), np.float32)), BF16)
    tok, col = np.arange(t_)[:, None], np.arange(128)[None, :]
    blockind = np.concatenate([(tok < col * tb) & (col <= nb), (tok < (col + 1) * tb) & (col < nb)], axis=1)
    slot, base, dense = pl.pallas_call(
        functools.partial(_route_kernel, cap=cap),
        grid=(b_,),
        in_specs=[pl.BlockSpec((1, e_, t_), lambda b: (b, 0, 0)),
                  pl.BlockSpec((tb, tb), lambda b: (0, 0)),
                  pl.BlockSpec((t_, 256), lambda b: (0, 0))],
        out_specs=[pl.BlockSpec((1, e_, t_), lambda b: (b, 0, 0)), pl.BlockSpec((1, e_, 128), lambda b: (b, 0, 0)),
                   pl.BlockSpec((1, 1, 128), lambda b: (b, 0, 0))],
        out_shape=[jax.ShapeDtypeStruct((b_, e_, t_), jnp.int32), jax.ShapeDtypeStruct((b_, e_, 128), jnp.int32),
                   jax.ShapeDtypeStruct((b_, 1, 128), jnp.int32)],
        compiler_params=_params("arbitrary"),
        name="route",
    )(aff, tri, jnp.asarray(blockind.astype(np.float32), BF16))
    return slot, base[:, :, :nb + 1].reshape(-1), dense[:, 0, :nb].reshape(-1)


def _window_plan(tbl_ref, b, e, tb, ne, nb):
    idx = (b * ne + e) * (nb + 1) + tb
    base, end = tbl_ref[idx], tbl_ref[idx + 1]
    start = (base >> 4) << 4
    n_win = jnp.where(end > base, (end - start + SLOT_WINDOW - 1) >> 6, 0)
    return start, n_win


def _dense_start(tbl_ref, b, e, tb, ne, nb, cap):
    base = tbl_ref[(b * ne + e) * (nb + 1) + tb]
    return pl.multiple_of(jnp.minimum((base >> 4) << 4, cap - DENSE_WINDOW), SLOT_ALIGN)


def _gather_kernel(tbl_ref, dense_ref, h_ref, slot_ref, xg_ref, *, ne, nb, cap):
    b, tb = pl.program_id(0), pl.program_id(2)

    @pl.when(tb == 0)
    def _():
        xg_ref[...] = jnp.zeros(xg_ref.shape, xg_ref.dtype)

    h = h_ref[0]
    n_tok = h.shape[0]
    ids0 = lax.broadcasted_iota(jnp.int32, (SLOT_WINDOW, n_tok), 0)
    dense = dense_ref[b * nb + tb] > 0

    @pl.when(dense)
    def _():
        starts = [_dense_start(tbl_ref, b, e, tb, ne, nb, cap) for e in range(ne)]
        ids = lax.broadcasted_iota(jnp.int32, (DENSE_WINDOW, n_tok), 0)
        onehot = jnp.concatenate(
            [jnp.where(ids == slot_ref[0, e:e + 1, :] - starts[e], 1.0, 0.0).astype(BF16) for e in range(ne)], axis=0)
        rows = jnp.dot(onehot, h, preferred_element_type=F32).astype(BF16)
        for e in range(ne):
            win = xg_ref.at[0, e, pl.ds(starts[e], DENSE_WINDOW), :]
            win[...] = win[...] + rows[e * DENSE_WINDOW:(e + 1) * DENSE_WINDOW]

    @pl.when(jnp.logical_not(dense))
    def _():
        for e in range(ne):
            srow = slot_ref[0, e:e + 1, :]
            start, n_win = _window_plan(tbl_ref, b, e, tb, ne, nb)

            def body(k, carry, e=e, srow=srow, start=start):
                lo = start + k * SLOT_WINDOW
                w0 = pl.multiple_of(jnp.minimum(lo, cap - SLOT_WINDOW), SLOT_ALIGN)
                ids = w0 + ids0
                onehot = jnp.where(ids >= lo, jnp.where(ids == srow, 1.0, 0.0), 0.0).astype(BF16)
                rows = jnp.dot(onehot, h, preferred_element_type=F32)
                win = xg_ref.at[0, e, pl.ds(w0, SLOT_WINDOW), :]
                win[...] = win[...] + rows.astype(BF16)
                return carry

            lax.fori_loop(0, n_win, body, 0)


def _gather(h, slot, tbl, dense):
    b_, t_, d_ = h.shape
    e_ = slot.shape[1]
    cap = EC_CAPACITY_FACTOR * t_ // e_
    tb = TOKEN_BLOCK
    nb = t_ // tb
    dh = d_ // 2
    return pl.pallas_call(
        functools.partial(_gather_kernel, ne=e_, nb=nb, cap=cap),
        grid_spec=pltpu.PrefetchScalarGridSpec(
            num_scalar_prefetch=2,
            grid=(b_, 2, nb),
            in_specs=[pl.BlockSpec((1, tb, dh), lambda b, c, i, *_: (b, i, c)),
                      pl.BlockSpec((1, e_, tb), lambda b, c, i, *_: (b, 0, i))],
            out_specs=pl.BlockSpec((1, e_, cap, dh), lambda b, c, i, *_: (b, 0, 0, c)),
        ),
        out_shape=jax.ShapeDtypeStruct((b_, e_, cap, d_), BF16),
        compiler_params=_params("arbitrary", "arbitrary", "arbitrary", vmem=V7X_VMEM_LIMIT),
        name="moe_gather",
    )(tbl, dense, h, slot)


def _expert_kernel(xg_ref, wg_ref, wu_ref, wd_ref, y_ref, acc_ref):
    f = pl.program_id(1)
    wg = wg_ref[0, 0].astype(BF16)
    wu = wu_ref[0, 0].astype(BF16)
    wd = wd_ref[0, 0].astype(BF16)
    n_b, _, cap, _ = xg_ref.shape

    @pl.when(jnp.logical_and(pl.program_id(0) == 0, f == 0))
    def _():
        acc_ref[...] = jnp.zeros(acc_ref.shape, F32)

    for b in range(n_b):
        for r in range(cap // FFN_ROWS):
            rows = pl.ds(r * FFN_ROWS, FFN_ROWS)
            acc_rows = pl.ds((b * cap) + r * FFN_ROWS, FFN_ROWS)
            xr = xg_ref[b, 0, rows, :]
            a = jnp.dot(xr, wg, preferred_element_type=F32)
            u = jnp.dot(xr, wu, preferred_element_type=F32)
            part = jnp.dot((_silu(a) * u).astype(BF16), wd, preferred_element_type=F32)
            total = jnp.where(f == 0, 0.0, acc_ref[acc_rows, :]) + part
            acc_ref[acc_rows, :] = total
            y_ref[b, 0, rows, :] = total.astype(BF16)


def _experts(xg, layer, w_gate, w_up, w_down):
    b_, e_, cap, d_ = xg.shape
    f_ = w_gate.shape[-1]
    ft = min(FFN_TILE, f_)
    return pl.pallas_call(
        _expert_kernel,
        grid=(e_, f_ // ft),
        in_specs=[pl.BlockSpec((b_, 1, cap, d_), lambda e, f: (0, e, 0, 0)),
                  pl.BlockSpec((1, 1, d_, ft), lambda e, f: (layer, e, 0, f)),
                  pl.BlockSpec((1, 1, d_, ft), lambda e, f: (layer, e, 0, f)),
                  pl.BlockSpec((1, 1, ft, d_), lambda e, f: (layer, e, f, 0))],
        out_specs=pl.BlockSpec((b_, 1, cap, d_), lambda e, f: (0, e, 0, 0)),
        out_shape=jax.ShapeDtypeStruct((b_, e_, cap, d_), BF16),
        scratch_shapes=[pltpu.VMEM((b_ * cap, d_), F32)],
        compiler_params=_params("arbitrary", "arbitrary", vmem=V7X_VMEM_LIMIT),
        name="moe_experts",
    )(xg, w_gate, w_up, w_down)


def _combine_kernel(tbl_ref, dense_ref, y_ref, slot_ref, aff_ref, x_ref, gate_ref, out_ref, acc_ref, *, ne, nb, cap):
    b, tb = pl.program_id(0), pl.program_id(2)
    n_tok = acc_ref.shape[0]
    ids0 = lax.broadcasted_iota(jnp.int32, (SLOT_WINDOW, n_tok), 0)
    dense = dense_ref[b * nb + tb] > 0

    @pl.when(dense)
    def _():
        starts = [_dense_start(tbl_ref, b, e, tb, ne, nb, cap) for e in range(ne)]
        ids = lax.broadcasted_iota(jnp.int32, (DENSE_WINDOW, n_tok), 0)
        weights = jnp.concatenate(
            [jnp.where(ids == slot_ref[0, e:e + 1, :] - starts[e], aff_ref[0, e:e + 1, :], 0.0).astype(BF16)
             for e in range(ne)], axis=0)
        yw = jnp.concatenate([y_ref[0, e, pl.ds(starts[e], DENSE_WINDOW), :] for e in range(ne)], axis=0)
        acc_ref[...] = lax.dot_general(weights, yw, TN_DIMS, preferred_element_type=F32)

    @pl.when(jnp.logical_not(dense))
    def _():
        acc_ref[...] = jnp.zeros(acc_ref.shape, F32)
        for e in range(ne):
            srow = slot_ref[0, e:e + 1, :]
            grow = aff_ref[0, e:e + 1, :]
            start, n_win = _window_plan(tbl_ref, b, e, tb, ne, nb)

            def body(k, carry, e=e, srow=srow, grow=grow, start=start):
                lo = start + k * SLOT_WINDOW
                w0 = pl.multiple_of(jnp.minimum(lo, cap - SLOT_WINDOW), SLOT_ALIGN)
                ids = w0 + ids0
                weights = jnp.where(ids >= lo, jnp.where(ids == srow, grow, 0.0), 0.0).astype(BF16)
                yw = y_ref[0, e, pl.ds(w0, SLOT_WINDOW), :]
                acc_ref[...] = acc_ref[...] + lax.dot_general(weights, yw, TN_DIMS, preferred_element_type=F32)
                return carry

            lax.fori_loop(0, n_win, body, 0)

    out_ref[0] = x_ref[0] + gate_ref[0] * acc_ref[...]


def _combine(y, slot, aff, tbl, dense, x, gate):
    b_, t_, d_ = x.shape
    e_, cap = y.shape[1], y.shape[2]
    tb = TOKEN_BLOCK
    nb = t_ // tb
    dh = d_ // 2
    route_spec = pl.BlockSpec((1, e_, tb), lambda b, c, i, *_: (b, 0, i))
    row = pl.BlockSpec((1, tb, dh), lambda b, c, i, *_: (b, i, c))
    return pl.pallas_call(
        functools.partial(_combine_kernel, ne=e_, nb=nb, cap=cap),
        grid_spec=pltpu.PrefetchScalarGridSpec(
            num_scalar_prefetch=2,
            grid=(b_, 2, nb),
            in_specs=[pl.BlockSpec((1, e_, cap, dh), lambda b, c, i, *_: (b, 0, 0, c)),
                      route_spec, route_spec, row,
                      pl.BlockSpec((1, 1, dh), lambda b, c, i, *_: (b, 0, c))],
            out_specs=row,
            scratch_shapes=[pltpu.VMEM((tb, dh), F32)],
        ),
        out_shape=jax.ShapeDtypeStruct((b_, t_, d_), F32),
        compiler_params=_params("arbitrary", "arbitrary", "arbitrary", vmem=V7X_VMEM_LIMIT),
        name="moe_combine",
    )(tbl, dense, y, slot, aff, x, gate)


def _moe(x, h, aff, gate, layer, w_gate, w_up, w_down):
    slot, tbl, dense = _route(aff)
    xg = _gather(h, slot, tbl, dense)
    y = _experts(xg, layer, w_gate, w_up, w_down)
    return _combine(y, slot, aff, tbl, dense, x, gate)


def _final_norm_kernel(x_ref, gain_ref, out_ref):
    x = x_ref[0]
    out_ref[0] = x * lax.rsqrt(jnp.mean(x * x, axis=-1, keepdims=True) + EPS) * gain_ref[...]


def _final_norm(x, gain):
    b_, t_, d_ = x.shape
    tm = ROW_TILE
    row = pl.BlockSpec((1, tm, d_), lambda b, i: (b, i, 0))
    return pl.pallas_call(
        _final_norm_kernel,
        grid=(b_, t_ // tm),
        in_specs=[row, pl.BlockSpec((1, d_), lambda b, i: (0, 0))],
        out_specs=row,
        out_shape=jax.ShapeDtypeStruct((b_, t_, d_), F32),
        compiler_params=_params("arbitrary", "arbitrary"),
        name="final_norm",
    )(x, gain)


def kernel(x, c, ctx, c_ctx, ada_w, ada_b, norm_mix, norm_ffn, norm_final, hg_w_in, hg_lb_logits, hg_norm, hg_w_out, sc_w_in, sc_conv, sc_w_out, moe_router, moe_w_gate, moe_w_up, moe_w_down):
    b_, t_, d_ = x.shape
    depth = ada_w.shape[0]
    n_ada = ada_w.shape[-1] // d_
    n_heads = d_ // HEAD_DIM
    assert depth == 2 and n_ada == 6 and b_ + 1 <= 8
    assert t_ % GLA_CHUNK == 0 and ctx.shape[1] % GLA_CHUNK == 0 and ROW_TILE % GRID_W == 0

    cond = jnp.concatenate([c, c_ctx[None], jnp.zeros((8 - b_ - 1, d_), F32)], axis=0)
    mod = _ada_vectors(cond, ada_w, ada_b, n_ada)
    vec = lambda i, j: mod[i, j, :b_][:, None, :]
    cvec = lambda i, j: mod[i, j, b_][None, None, :]
    rowtab, coltab, lower = _tables(hg_lb_logits, t_)
    row_of = lambda a, i: a[i][None, :]
    router_t = lambda i: jnp.swapaxes(moe_router[i], 0, 1)

    w_in = hg_w_in[0].astype(BF16)
    lb0 = row_of(lower, 0)
    gain0 = row_of(norm_mix, 0)
    qc, vc, lfc_f, lfc_b = _hgrn_in(ctx, None, gain0, cvec(0, 0), cvec(0, 1), lb0, w_in, with_gate=False)
    zeros = jnp.zeros((b_, n_heads, HEAD_DIM, HEAD_DIM), F32)
    _, _, s_f, s_b = _gla_bidir(qc, vc, lfc_f, lfc_b, zeros, zeros, lb0)
    x0, q, v, lf_f, lf_b, g = _hgrn_in(x, (rowtab, coltab), gain0, vec(0, 0), vec(0, 1), lb0, w_in, with_gate=True)
    o_f, o_b, _, _ = _gla_bidir(q, v, lf_f, lf_b, s_f, s_b, lb0)
    x1, h, aff = _mixer_out(_hgrn_out_kernel, "hgrn_out", [o_f, o_b, g], [], x0,
                            [row_of(hg_norm, 0), hg_w_out[0].astype(BF16)],
                            vec(0, 2), row_of(norm_ffn, 0), vec(0, 3), vec(0, 4), router_t(0))
    x2 = _moe(x1, h, aff, vec(0, 5), 0, moe_w_gate, moe_w_up, moe_w_down)

    bg, cu = _conv_in(x2, row_of(norm_mix, 1), vec(1, 0), vec(1, 1), sc_w_in[0].astype(BF16))
    x3, h, aff = _mixer_out(_conv_out_kernel, "conv_out", [cu, bg], [cu, cu], x2,
                            [sc_conv[0], sc_w_out[0].astype(BF16)],
                            vec(1, 2), row_of(norm_ffn, 1), vec(1, 3), vec(1, 4), router_t(1))
    x4 = _moe(x3, h, aff, vec(1, 5), 1, moe_w_gate, moe_w_up, moe_w_down)
    return _final_norm(x4, norm_final[None, :])
```

```python
import functools
import math

import numpy as np
import jax
import jax.numpy as jnp
from jax import lax
from jax.experimental import pallas as pl
from jax.experimental.pallas import tpu as pltpu

F32 = jnp.float32
BF16 = jnp.bfloat16

EPS = 1e-6
POS_TEMP = 10000.0
GRID_W = 64
HEAD_DIM = 128
EC_CAPACITY_FACTOR = 2
GLA_CHUNK = 256
GLA_LEVELS = 8
GLA_HEADS_PER_STEP = 4
GLA_MERGED = 5
GLA_MAX_EXPONENT = 80.0
ROW_TILE = 512
SUB_ROWS = 256
TOKEN_BLOCK = 256
BLOCKS_PER_STEP = 2
SLOT_WINDOW = 64
DENSE_WINDOW = 128
SLOT_ALIGN = 16
FFN_TILE = 512
FFN_ROWS = 512
V7X_VMEM_LIMIT = 56 * 1024 * 1024

NT_DIMS = (((1,), (1,)), ((), ()))
TN_DIMS = (((0,), (0,)), ((), ()))


def _params(*sem, vmem=None):
    return pltpu.CompilerParams(dimension_semantics=sem, vmem_limit_bytes=vmem)


def _dot_bf16x3(a, b, dims):
    a0, b0 = a.astype(BF16), b.astype(BF16)
    a1 = (a - a0.astype(F32)).astype(BF16)
    b1 = (b - b0.astype(F32)).astype(BF16)
    d = lambda x, y: lax.dot_general(x, y, dims, preferred_element_type=F32)
    return (d(a0, b1) + d(a1, b0)) + d(a0, b0)


def _sub_tiles(n_rows):
    sub = min(SUB_ROWS, n_rows)
    return [slice(s, s + sub) for s in range(0, n_rows, sub)]


def _sigmoid(x):
    return 1.0 / (1.0 + jnp.exp(-x))


def _silu(x):
    return x * _sigmoid(x)


def _modulate(x, gain, shift, scale):
    y = x * lax.rsqrt(jnp.mean(x * x, axis=-1, keepdims=True) + EPS)
    return (y * gain) * (1.0 + scale) + shift


def _ada_kernel(cond_ref, w_ref, b_ref, out_ref):
    s = _silu(cond_ref[...])
    out_ref[0, 0] = _dot_bf16x3(s, w_ref[0], (((1,), (0,)), ((), ()))) + b_ref[0, 0]


def _ada_vectors(cond, ada_w, ada_b, n_ada):
    depth, d_, _ = ada_w.shape
    return pl.pallas_call(
        _ada_kernel,
        grid=(depth, n_ada),
        in_specs=[pl.BlockSpec((8, d_), lambda i, j: (0, 0)),
                  pl.BlockSpec((1, d_, d_), lambda i, j: (i, 0, j)),
                  pl.BlockSpec((1, 1, 1, d_), lambda i, j: (i, j, 0, 0))],
        out_specs=pl.BlockSpec((1, 1, 8, d_), lambda i, j: (i, j, 0, 0)),
        out_shape=jax.ShapeDtypeStruct((depth, n_ada, 8, d_), F32),
        compiler_params=_params("arbitrary", "arbitrary"),
        name="ada_vectors",
    )(cond, ada_w, ada_b.reshape(depth, n_ada, 1, d_))


def _tables_kernel(lb_logits_ref, rowtab_ref, coltab_ref, lb_ref, *, n_freq):
    def table(n_pos):
        p = lax.broadcasted_iota(jnp.int32, (n_pos, n_freq), 0).astype(F32)
        j = lax.broadcasted_iota(jnp.int32, (n_pos, n_freq), 1).astype(F32)
        omega = jnp.exp(j * (-math.log(POS_TEMP) / n_freq))
        ang = p * omega
        return jnp.concatenate([jnp.sin(ang), jnp.cos(ang)], axis=-1)

    rowtab_ref[...] = table(rowtab_ref.shape[0])
    coltab_ref[...] = table(coltab_ref.shape[0])
    logits = lb_logits_ref[...]
    e = jnp.exp(logits - jnp.max(logits, axis=0, keepdims=True))
    sm = e / jnp.sum(e, axis=0, keepdims=True)
    acc = sm[0:1]
    lb_ref[0:1] = acc
    for i in range(1, lb_ref.shape[0]):
        acc = acc + sm[i:i + 1]
        lb_ref[i:i + 1] = acc


def _tables(lb_logits, n_tokens):
    n_lb, d_ = lb_logits.shape
    n_freq = d_ // 4
    rows = n_tokens // GRID_W
    return pl.pallas_call(
        functools.partial(_tables_kernel, n_freq=n_freq),
        out_shape=[jax.ShapeDtypeStruct((rows, 2 * n_freq), F32),
                   jax.ShapeDtypeStruct((GRID_W, 2 * n_freq), F32),
                   jax.ShapeDtypeStruct((n_lb, d_), F32)],
        name="pos_tables",
    )(lb_logits)


def _hgrn_in_kernel(*refs, with_pos, with_gate):
    it = iter(refs)
    x_ref = next(it)
    if with_pos:
        rowtab_ref, coltab_ref = next(it), next(it)
    gain_ref, shift_ref, scale_ref, lb_ref, w_ref = next(it), next(it), next(it), next(it), next(it)
    if with_pos:
        x0_ref = next(it)
    q_ref, v_ref, lff_ref, lfb_ref = next(it), next(it), next(it), next(it)
    g_ref = next(it) if with_gate else None

    d_ = x_ref.shape[-1]
    lb = lb_ref[...]
    for rows in _sub_tiles(x_ref.shape[1]):
        x = x_ref[0, rows]
        if with_pos:
            grid_rows = range(rows.start // GRID_W, rows.stop // GRID_W)
            pos_row = jnp.concatenate(
                [jnp.broadcast_to(rowtab_ref[0, r:r + 1, :], (GRID_W, rowtab_ref.shape[-1])) for r in grid_rows], axis=0)
            pos_col = jnp.concatenate([coltab_ref[...]] * len(grid_rows), axis=0)
            x = x + jnp.concatenate([pos_row, pos_col], axis=-1)
            x0_ref[0, rows] = x
        h = _modulate(x, gain_ref[...], shift_ref[0], scale_ref[0]).astype(BF16)
        part = lambda p: jnp.dot(h, w_ref[:, p * d_:(p + 1) * d_], preferred_element_type=F32)
        q_ref[0, rows] = (part(0) * HEAD_DIM ** -0.5).astype(BF16)
        v_ref[0, rows] = part(1).astype(BF16)
        lff_ref[0, rows] = jnp.log(lb + (1.0 - lb) * _sigmoid(part(2)))
        lfb_ref[0, rows] = jnp.log(lb + (1.0 - lb) * _sigmoid(part(3)))
        if with_gate:
            g_ref[0, rows] = part(4).astype(BF16)


def _hgrn_in(x, tabs, gain, shift, scale, lb, w_in, *, with_gate):
    b_, t_, d_ = x.shape
    tm = min(ROW_TILE, t_)
    with_pos = tabs is not None
    per_sample = lambda a: pl.BlockSpec((1, 1, d_), (lambda b, i: (b, 0, 0)) if a.shape[0] > 1 else (lambda b, i: (0, 0, 0)))
    row = pl.BlockSpec((1, tm, d_), lambda b, i: (b, i, 0))
    vec = pl.BlockSpec((1, d_), lambda b, i: (0, 0))
    args, in_specs = [x], [row]
    if with_pos:
        rowtab, coltab = tabs
        rows_per_tile = tm // GRID_W
        args += [rowtab.reshape(rowtab.shape[0] // rows_per_tile, rows_per_tile, rowtab.shape[1]), coltab]
        in_specs += [pl.BlockSpec((1, rows_per_tile, rowtab.shape[1]), lambda b, i: (i, 0, 0)),
                     pl.BlockSpec(coltab.shape, lambda b, i: (0, 0))]
    args += [gain, shift, scale, lb, w_in]
    in_specs += [vec, per_sample(shift), per_sample(scale), vec, pl.BlockSpec(w_in.shape, lambda b, i: (0, 0))]
    out_shape, out_specs = [], []
    if with_pos:
        out_shape.append(jax.ShapeDtypeStruct((b_, t_, d_), F32))
        out_specs.append(row)
    out_shape += [jax.ShapeDtypeStruct((b_, t_, d_), BF16)] * 2 + [jax.ShapeDtypeStruct((b_, t_, d_), F32)] * 2
    out_specs += [row] * 4
    if with_gate:
        out_shape.append(jax.ShapeDtypeStruct((b_, t_, d_), BF16))
        out_specs.append(row)
    return pl.pallas_call(
        functools.partial(_hgrn_in_kernel, with_pos=with_pos, with_gate=with_gate),
        grid=(b_, t_ // tm),
        in_specs=in_specs, out_specs=out_specs, out_shape=out_shape,
        compiler_params=_params("arbitrary", "arbitrary", vmem=V7X_VMEM_LIMIT),
        name="hgrn_in_latent" if with_pos else "hgrn_in_context",
    )(*args)


def _gla_consts(reverse):
    c = GLA_CHUNK
    idx = np.arange(c)
    rank = (c - 1 - idx) if reverse else idx
    tri = (rank[None, :] <= rank[:, None]).astype(np.float32)
    hc = c // 2
    hrank = rank[:hc] - rank[:hc].min()
    lvl = np.full((hc, hc), -1, np.int32)
    rt, rs = hrank[:, None], hrank[None, :]
    lvl[rt == rs] = 0
    for level in range(1, GLA_LEVELS):
        blk, half = 1 << level, 1 << (level - 1)
        lvl[(rt // blk == rs // blk) & ((rt % blk) >= half) & ((rs % blk) < half)] = level
    return jnp.asarray(tri, BF16), jnp.asarray(lvl)


def _later_group(level, reverse, group):
    rank = (GLA_CHUNK - 1 - 8 * group) if reverse else 8 * group
    return (rank % (1 << level)) >= (1 << (level - 1))


def _boundary_rows(level, reverse):
    c = GLA_CHUNK
    blk, half = 1 << level, 1 << (level - 1)
    rows = []
    for i in range(c):
        rank = (c - 1 - i) if reverse else i
        brank = (rank // blk) * blk + half - 1
        rows.append((c - 1 - brank) if reverse else brank)
    return rows


def _gla_low_levels(q, kk, lf, cum, lvl, bcast, halves, reverse):
    ng = q.shape[0] // 8
    qb, kb = q.astype(BF16), kk.astype(BF16)
    tiles = [jnp.where(lvl == 0, lax.dot_general(qb[hs], kb[hs], NT_DIMS, preferred_element_type=F32), 0.0)
             for hs in halves]
    row8 = lax.broadcasted_iota(jnp.int32, (8, HEAD_DIM), 0)
    rank8 = (7 - row8) if reverse else row8
    for level in range(1, 4):
        sgn8 = jnp.where(((rank8 >> (level - 1)) & 1) == 1, 1.0, -1.0)
        sgn = jnp.concatenate([sgn8] * ng, axis=0)
        later = sgn > 0.0
        if level == 1:
            g = jnp.where(later, lf, 0.0)
        else:
            brow = _boundary_rows(level, reverse)
            pieces = []
            for grp in range(ng):
                first = bcast(brow[8 * grp])
                pieces.append(jnp.where(row8 < 4, first, bcast(brow[8 * grp + 7])) if level == 2 else first)
            g = (cum - jnp.concatenate(pieces, axis=0)) * sgn
        xe = (jnp.where(later, q, kk) * jnp.exp(g)).astype(BF16)
        for h, hs in enumerate(halves):
            s = lax.dot_general(xe[hs], xe[hs], NT_DIMS, preferred_element_type=F32)
            tiles[h] = jnp.where(lvl == level, s, tiles[h])
    return tiles


def _gla_chunk(q_ref, v_ref, lf_ref, lanes, st_ref, cum_ref, tri_ref, lvl_ref, o_ref, reverse, merged):
    c = GLA_CHUNK
    hc, ng = c // 2, c // 8
    halves = (slice(0, hc), slice(hc, c))
    lf = lf_ref[0, :, lanes]
    q = q_ref[0, :, lanes].astype(F32)
    v = v_ref[0, :, lanes]
    kk = 1.0 - jnp.exp(lf)
    hi = lf.astype(BF16)
    lo = (lf - hi.astype(F32)).astype(BF16)
    two = jnp.dot(tri_ref[...], jnp.concatenate([hi, lo], axis=1), preferred_element_type=F32)
    cum = two[:, HEAD_DIM:] + two[:, :HEAD_DIM]
    cum_ref[...] = cum
    lvl = lvl_ref[...]
    bcast = lambda r: jnp.broadcast_to(cum_ref[r:r + 1, :], (8, HEAD_DIM))
    groups = lambda a: [a[8 * i:8 * i + 8] for i in range(a.shape[0] // 8)]

    if merged:
        blk = 1 << merged
        cache, pieces = {}, []
        for grp in range(ng):
            rank = (c - 1 - 8 * grp) if reverse else 8 * grp
            first = (rank // blk) * blk
            row = (c - 1 - first) if reverse else first
            pieces.append(cache.setdefault(row, bcast(row)))
        d = cum - jnp.concatenate(pieces, axis=0)
        xq = (q * jnp.exp(d)).astype(BF16)
        xk = (kk * jnp.exp(-d)).astype(BF16)
        inside = jnp.logical_and(lvl >= 0, lvl <= merged)
        tiles = [jnp.where(inside, lax.dot_general(xq[hs], xk[hs], NT_DIMS, preferred_element_type=F32), 0.0)
                 for hs in halves]
    else:
        tiles = _gla_low_levels(q, kk, lf, cum, lvl, bcast, halves, reverse)

    tile_rows = [groups(t) for t in tiles]
    lvl_rows = groups(lvl)
    q_rows, k_rows, cum_rows = groups(q), groups(kk), groups(cum)
    for level in range(max(4, merged + 1), GLA_LEVELS):
        brow = _boundary_rows(level, reverse)
        cache = {}
        g_rows, x_rows = [], []
        for grp in range(ng):
            cb = cache.setdefault(brow[8 * grp], bcast(brow[8 * grp]))
            later = _later_group(level, reverse, grp)
            g_rows.append(cum_rows[grp] - cb if later else cb - cum_rows[grp])
            x_rows.append(q_rows[grp] if later else k_rows[grp])
        xe = jnp.concatenate(x_rows, axis=0) * jnp.exp(jnp.concatenate(g_rows, axis=0))
        xe_rows = groups(xe)
        for h, hs in enumerate(halves):
            later_groups = [grp for grp in range(h * ng // 2, (h + 1) * ng // 2) if _later_group(level, reverse, grp)]
            qc = jnp.concatenate([xe_rows[grp] for grp in later_groups], axis=0).astype(BF16)
            s = lax.dot_general(qc, xe[hs].astype(BF16), NT_DIMS, preferred_element_type=F32)
            for i, grp in enumerate(later_groups):
                local = grp - h * ng // 2
                tile_rows[h][local] = jnp.where(lvl_rows[local] == level, s[8 * i:8 * i + 8], tile_rows[h][local])

    early, late = (1, 0) if reverse else (0, 1)
    cb = cum_ref[_boundary_rows(GLA_LEVELS, reverse)[0]:_boundary_rows(GLA_LEVELS, reverse)[0] + 1, :]
    ql = (q[halves[late]] * jnp.exp(cum[halves[late]] - cb)).astype(BF16)
    ke = (kk[halves[early]] * jnp.exp(cb - cum[halves[early]])).astype(BF16)
    cross = lax.dot_general(ql, ke, NT_DIMS, preferred_element_type=F32)
    t_a, t_b = (jnp.concatenate(rows, axis=0) for rows in tile_rows)
    zero = jnp.zeros((hc, hc), F32)
    if reverse:
        scores = jnp.concatenate([jnp.concatenate([t_a, cross], axis=1), jnp.concatenate([zero, t_b], axis=1)], axis=0)
    else:
        scores = jnp.concatenate([jnp.concatenate([t_a, zero], axis=1), jnp.concatenate([cross, t_b], axis=1)], axis=0)

    st = st_ref[...]
    qe = (q * jnp.exp(cum)).astype(BF16)
    o = jnp.dot(scores.astype(BF16), v, preferred_element_type=F32)
    o_ref[0, :, lanes] = o + lax.dot_general(qe, st.astype(BF16), NT_DIMS, preferred_element_type=F32)
    last_row = 0 if reverse else c - 1
    last = cum_ref[last_row:last_row + 1, :]
    ke_all = (kk * jnp.exp(last - cum)).astype(BF16)
    st_ref[...] = st * jnp.exp(last) + lax.dot_general(v, ke_all, TN_DIMS, preferred_element_type=F32)


def _gla_kernel(qf_ref, vf_ref, lff_ref, qb_ref, vb_ref, lfb_ref, s0f_ref, s0b_ref,
                trif_ref, lvlf_ref, trib_ref, lvlb_ref,
                of_ref, ob_ref, sff_ref, sfb_ref, stf_ref, stb_ref, cumf_ref, cumb_ref, *, merged):
    j = pl.program_id(2)

    @pl.when(j == 0)
    def _():
        stf_ref[...] = s0f_ref[0]
        stb_ref[...] = s0b_ref[0]

    for k in range(stf_ref.shape[0]):
        lanes = slice(k * HEAD_DIM, (k + 1) * HEAD_DIM)
        _gla_chunk(qf_ref, vf_ref, lff_ref, lanes, stf_ref.at[k], cumf_ref.at[k], trif_ref, lvlf_ref, of_ref,
                   False, merged)
        _gla_chunk(qb_ref, vb_ref, lfb_ref, lanes, stb_ref.at[k], cumb_ref.at[k], trib_ref, lvlb_ref, ob_ref,
                   True, merged)

    @pl.when(j == pl.num_programs(2) - 1)
    def _():
        sff_ref[0] = stf_ref[...]
        sfb_ref[0] = stb_ref[...]


def _gla_bidir(q, v, lf_f, lf_b, s0f, s0b, lb):
    worst = (2 ** GLA_MERGED - 1) * jnp.max(-jnp.log(lb))
    run = lambda merged: (lambda *a: _gla_call(*a, merged=merged))
    return lax.cond(worst < GLA_MAX_EXPONENT, run(GLA_MERGED), run(0), q, v, lf_f, lf_b, s0f, s0b)


def _gla_call(q, v, lf_f, lf_b, s0f, s0b, *, merged):
    b_, t_, d_ = q.shape
    h_ = d_ // HEAD_DIM
    hp = math.gcd(GLA_HEADS_PER_STEP, h_)
    c = GLA_CHUNK
    n = t_ // c
    fwd = lambda b, h, j: (b, j, h)
    bwd = lambda b, h, j: (b, n - 1 - j, h)
    st = lambda b, h, j: (b, h, 0, 0)
    const = lambda b, h, j: (0, 0)
    blk = lambda im: pl.BlockSpec((1, c, hp * HEAD_DIM), im)
    st_spec = pl.BlockSpec((1, hp, HEAD_DIM, HEAD_DIM), st)
    cspecs = [pl.BlockSpec((c, c), const), pl.BlockSpec((c // 2, c // 2), const)]
    state = pltpu.VMEM((hp, HEAD_DIM, HEAD_DIM), F32)
    cum = pltpu.VMEM((hp, c, HEAD_DIM), F32)
    return pl.pallas_call(
        functools.partial(_gla_kernel, merged=merged),
        grid=(b_, h_ // hp, n),
        in_specs=[blk(fwd), blk(fwd), blk(fwd), blk(bwd), blk(bwd), blk(bwd), st_spec, st_spec] + cspecs + cspecs,
        out_specs=[blk(fwd), blk(bwd), st_spec, st_spec],
        out_shape=[jax.ShapeDtypeStruct((b_, t_, d_), F32)] * 2
                  + [jax.ShapeDtypeStruct((b_, h_, HEAD_DIM, HEAD_DIM), F32)] * 2,
        scratch_shapes=[state, state, cum, cum],
        compiler_params=_params("arbitrary", "arbitrary", "arbitrary"),
        name="gla_merged" if merged else "gla_split",
    )(q, v, lf_f, q, v, lf_b, s0f, s0b, *_gla_consts(False), *_gla_consts(True))


def _mixer_epilogue(y, rows, x_ref, w_ref, gate_ref, gain_ref, shift_ref, scale_ref, wr_ref, x1_ref, h_ref, aff_ref):
    y = jnp.dot(y.astype(BF16), w_ref[...], preferred_element_type=F32)
    x1 = x_ref[0, rows] + gate_ref[0] * y
    x1_ref[0, rows] = x1
    hf = _modulate(x1, gain_ref[...], shift_ref[0], scale_ref[0])
    h_ref[0, rows] = hf.astype(BF16)
    logits = _dot_bf16x3(wr_ref[...], hf, NT_DIMS)
    e = jnp.exp(logits - jnp.max(logits, axis=0, keepdims=True))
    aff_ref[0, :, rows] = e / jnp.sum(e, axis=0, keepdims=True)


def _hgrn_out_kernel(of_ref, ob_ref, g_ref, x_ref, hnorm_ref, w_ref, *rest):
    hn = hnorm_ref[...]
    for rows in _sub_tiles(x_ref.shape[1]):
        o = of_ref[0, rows] + ob_ref[0, rows]
        heads = []
        for h in range(o.shape[-1] // HEAD_DIM):
            oh = o[:, h * HEAD_DIM:(h + 1) * HEAD_DIM]
            heads.append(oh * lax.rsqrt(jnp.mean(oh * oh, axis=-1, keepdims=True) + EPS) * hn)
        y = jnp.concatenate(heads, axis=-1) * _silu(g_ref[0, rows].astype(F32))
        _mixer_epilogue(y, rows, x_ref, w_ref, *rest)


def _conv_out_kernel(cu_ref, bg_ref, prev_ref, next_ref, x_ref, wc_ref, w_ref, *rest):
    i = pl.program_id(1)
    cu = cu_ref[0].astype(F32)
    tm = cu.shape[0]
    rid = lax.broadcasted_iota(jnp.int32, cu.shape, 0)
    before = jnp.where(i == 0, 0.0, prev_ref[0, 7:8, :].astype(F32))
    after = jnp.where(i == pl.num_programs(1) - 1, 0.0, next_ref[0, 0:1, :].astype(F32))
    left = jnp.where(rid == 0, before, pltpu.roll(cu, 1, 0))
    right = jnp.where(rid == tm - 1, after, pltpu.roll(cu, tm - 1, 0))
    wc = wc_ref[...]
    conv = left * wc[0:1] + cu * wc[1:2] + right * wc[2:3]
    for rows in _sub_tiles(tm):
        _mixer_epilogue(bg_ref[0, rows].astype(F32) * conv[rows], rows, x_ref, w_ref, *rest)


def _mixer_out(kernel, name, row_args, halo_args, x, small_args, gate, gain, shift, scale, w_router_t):
    b_, t_, d_ = x.shape
    e_ = w_router_t.shape[0]
    tm = min(ROW_TILE, t_)
    row = pl.BlockSpec((1, tm, d_), lambda b, i: (b, i, 0))
    per_sample = pl.BlockSpec((1, 1, d_), lambda b, i: (b, 0, 0))
    whole = lambda a: pl.BlockSpec(a.shape, lambda b, i: (0,) * a.ndim)
    n8 = t_ // 8
    halo_specs = [pl.BlockSpec((1, 8, d_), lambda b, i: (b, jnp.maximum(i * (tm // 8) - 1, 0), 0)),
                  pl.BlockSpec((1, 8, d_), lambda b, i: (b, jnp.minimum((i + 1) * (tm // 8), n8 - 1), 0))]
    return pl.pallas_call(
        kernel,
        grid=(b_, t_ // tm),
        in_specs=[row] * len(row_args) + halo_specs[:len(halo_args)] + [row]
                 + [whole(a) for a in small_args] + [per_sample, whole(gain), per_sample, per_sample, whole(w_router_t)],
        out_specs=[row, row, pl.BlockSpec((1, e_, tm), lambda b, i: (b, 0, i))],
        out_shape=[jax.ShapeDtypeStruct((b_, t_, d_), F32), jax.ShapeDtypeStruct((b_, t_, d_), BF16),
                   jax.ShapeDtypeStruct((b_, e_, t_), F32)],
        compiler_params=_params("arbitrary", "arbitrary", vmem=V7X_VMEM_LIMIT),
        name=name,
    )(*row_args, *halo_args, x, *small_args, gate, gain, shift, scale, w_router_t)


def _conv_in_kernel(x_ref, gain_ref, shift_ref, scale_ref, w_ref, bg_ref, cu_ref):
    d_ = x_ref.shape[-1]
    for rows in _sub_tiles(x_ref.shape[1]):
        h = _modulate(x_ref[0, rows], gain_ref[...], shift_ref[0], scale_ref[0]).astype(BF16)
        part = lambda p: jnp.dot(h, w_ref[:, p * d_:(p + 1) * d_], preferred_element_type=F32)
        bg_ref[0, rows] = part(0).astype(BF16)
        cu_ref[0, rows] = (part(1) * part(2)).astype(BF16)


def _conv_in(x, gain, shift, scale, w_in):
    b_, t_, d_ = x.shape
    tm = min(ROW_TILE, t_)
    row = pl.BlockSpec((1, tm, d_), lambda b, i: (b, i, 0))
    per_sample = pl.BlockSpec((1, 1, d_), lambda b, i: (b, 0, 0))
    return pl.pallas_call(
        _conv_in_kernel,
        grid=(b_, t_ // tm),
        in_specs=[row, pl.BlockSpec((1, d_), lambda b, i: (0, 0)), per_sample, per_sample,
                  pl.BlockSpec(w_in.shape, lambda b, i: (0, 0))],
        out_specs=[row, row],
        out_shape=[jax.ShapeDtypeStruct((b_, t_, d_), BF16)] * 2,
        compiler_params=_params("arbitrary", "arbitrary", vmem=V7X_VMEM_LIMIT),
        name="conv_in",
    )(x, gain, shift, scale, w_in)


def _route_kernel(aff_ref, tri_ref, blockind_ref, slot_ref, base_ref, dense_ref, *, cap):
    aff = aff_ref[0]
    e_, t_ = aff.shape

    def as_float(word):
        return pltpu.bitcast(word, F32)

    def count_ge(th):
        return jnp.sum(jnp.where(aff >= th, 1.0, 0.0), axis=1, keepdims=True)

    def search(_, carry):
        lo, hi = carry
        mid = lo + ((hi - lo + 1) >> 1)
        ok = count_ge(as_float(mid)) >= cap
        return jnp.where(ok, mid, lo), jnp.where(ok, hi, mid - 1)

    lo0 = jnp.zeros((e_, 1), jnp.int32)
    hi0 = jnp.full((e_, 1), 0x7F7FFFFF, jnp.int32)
    kth, _ = lax.fori_loop(0, 32, search, (lo0, hi0))
    above = aff >= as_float(kth + 1)
    tied = jnp.logical_and(aff >= as_float(kth), jnp.logical_not(above))
    need = cap - jnp.sum(jnp.where(above, 1.0, 0.0), axis=1, keepdims=True)
    tri = tri_ref[...]
    tb = tri.shape[0]
    carry_t = jnp.zeros((e_, 1), F32)
    carry_s = jnp.zeros((e_, 1), F32)
    sel_blocks = []
    for j in range(t_ // tb):
        cols = slice(j * tb, (j + 1) * tb)
        tied_j = tied[:, cols]
        ct = jnp.dot(jnp.where(tied_j, 1.0, 0.0).astype(BF16), tri, preferred_element_type=F32) + carry_t
        carry_t = ct[:, tb - 1:tb]
        sel_j = jnp.where(above[:, cols], 1.0, jnp.where(tied_j & (ct <= need), 1.0, 0.0))
        cs = jnp.dot(sel_j.astype(BF16), tri, preferred_element_type=F32) + carry_s
        carry_s = cs[:, tb - 1:tb]
        slot_ref[0, :, cols] = jnp.where(sel_j > 0.0, cs - 1.0, -1.0).astype(jnp.int32)
        sel_blocks.append(sel_j.astype(BF16))
    sel = jnp.concatenate(sel_blocks, axis=1)
    counts = jnp.dot(sel, blockind_ref[...], preferred_element_type=F32)
    base, end = counts[:, :128], counts[:, 128:]
    base_ref[0] = base.astype(jnp.int32)
    span = end - jnp.floor(base * (1.0 / SLOT_ALIGN)) * SLOT_ALIGN
    dense = jnp.max(span, axis=0, keepdims=True) <= DENSE_WINDOW
    dense_ref[0] = jnp.where(dense, 1, 0).astype(jnp.int32)


def _route(aff):
    b_, e_, t_ = aff.shape
    cap = EC_CAPACITY_FACTOR * t_ // e_
    tb = TOKEN_BLOCK
    nb = t_ // tb
    tri = jnp.asarray(np.triu(np.ones((tb, tb), np.float32)), BF16)
    tok, col = np.arange(t_)[:, None], np.arange(128)[None, :]
    blockind = np.concatenate([(tok < col * tb) & (col <= nb), (tok < (col + 1) * tb) & (col < nb)], axis=1)
    slot, base, dense = pl.pallas_call(
        functools.partial(_route_kernel, cap=cap),
        grid=(b_,),
        in_specs=[pl.BlockSpec((1, e_, t_), lambda b: (b, 0, 0)),
                  pl.BlockSpec((tb, tb), lambda b: (0, 0)),
                  pl.BlockSpec((t_, 256), lambda b: (0, 0))],
        out_specs=[pl.BlockSpec((1, e_, t_), lambda b: (b, 0, 0)), pl.BlockSpec((1, e_, 128), lambda b: (b, 0, 0)),
                   pl.BlockSpec((1, 1, 128), lambda b: (b, 0, 0))],
        out_shape=[jax.ShapeDtypeStruct((b_, e_, t_), jnp.int32), jax.ShapeDtypeStruct((b_, e_, 128), jnp.int32),
                   jax.ShapeDtypeStruct((b_, 1, 128), jnp.int32)],
        compiler_params=_params("arbitrary"),
        name="route",
    )(aff, tri, jnp.asarray(blockind.astype(np.float32), BF16))
    return slot, base[:, :, :nb + 1].reshape(-1), dense[:, 0, :nb].reshape(-1)


def _window_plan(tbl_ref, b, e, tb, ne, nb):
    idx = (b * ne + e) * (nb + 1) + tb
    base, end = tbl_ref[idx], tbl_ref[idx + 1]
    start = (base >> 4) << 4
    n_win = jnp.where(end > base, (end - start + SLOT_WINDOW - 1) >> 6, 0)
    return start, n_win


def _dense_start(tbl_ref, b, e, tb, ne, nb, cap):
    base = tbl_ref[(b * ne + e) * (nb + 1) + tb]
    return pl.multiple_of(jnp.minimum((base >> 4) << 4, cap - DENSE_WINDOW), SLOT_ALIGN)


def _gather_kernel(tbl_ref, dense_ref, h_ref, slot_ref, xg_ref, *, ne, nb, cap):
    b, step = pl.program_id(0), pl.program_id(2)
    n_tok = TOKEN_BLOCK

    @pl.when(step == 0)
    def _():
        xg_ref[...] = jnp.zeros(xg_ref.shape, xg_ref.dtype)

    def dense_block(sub):
        tb = step * BLOCKS_PER_STEP + sub
        toks = slice(sub * n_tok, (sub + 1) * n_tok)
        starts = [_dense_start(tbl_ref, b, e, tb, ne, nb, cap) for e in range(ne)]
        ids = lax.broadcasted_iota(jnp.int32, (DENSE_WINDOW, n_tok), 0)
        onehot = jnp.concatenate(
            [jnp.where(ids == slot_ref[0, e:e + 1, toks] - starts[e], 1.0, 0.0).astype(BF16) for e in range(ne)],
            axis=0)
        rows = jnp.dot(onehot, h_ref[0, toks, :], preferred_element_type=F32).astype(BF16)
        for e in range(ne):
            win = xg_ref.at[0, e, pl.ds(starts[e], DENSE_WINDOW), :]
            win[...] = win[...] + rows[e * DENSE_WINDOW:(e + 1) * DENSE_WINDOW]

    def windowed_block(sub):
        tb = step * BLOCKS_PER_STEP + sub
        toks = slice(sub * n_tok, (sub + 1) * n_tok)
        h = h_ref[0, toks, :]
        ids0 = lax.broadcasted_iota(jnp.int32, (SLOT_WINDOW, n_tok), 0)
        for e in range(ne):
            srow = slot_ref[0, e:e + 1, toks]
            start, n_win = _window_plan(tbl_ref, b, e, tb, ne, nb)

            def body(k, carry, e=e, srow=srow, start=start):
                lo = start + k * SLOT_WINDOW
                w0 = pl.multiple_of(jnp.minimum(lo, cap - SLOT_WINDOW), SLOT_ALIGN)
                ids = w0 + ids0
                onehot = jnp.where(ids >= lo, jnp.where(ids == srow, 1.0, 0.0), 0.0).astype(BF16)
                rows = jnp.dot(onehot, h, preferred_element_type=F32)
                win = xg_ref.at[0, e, pl.ds(w0, SLOT_WINDOW), :]
                win[...] = win[...] + rows.astype(BF16)
                return carry

            lax.fori_loop(0, n_win, body, 0)

    _per_block_dispatch(dense_ref, b * nb + step * BLOCKS_PER_STEP, dense_block, windowed_block)


def _per_block_dispatch(dense_ref, first, dense_block, windowed_block):
    flags = [dense_ref[first + sub] > 0 for sub in range(BLOCKS_PER_STEP)]
    all_dense = functools.reduce(jnp.logical_and, flags)

    @pl.when(all_dense)
    def _():
        for sub in range(BLOCKS_PER_STEP):
            dense_block(sub)

    @pl.when(jnp.logical_not(all_dense))
    def _():
        for sub in range(BLOCKS_PER_STEP):
            pl.when(flags[sub])(functools.partial(dense_block, sub))
            pl.when(jnp.logical_not(flags[sub]))(functools.partial(windowed_block, sub))


def _gather(h, slot, tbl, dense):
    b_, t_, d_ = h.shape
    e_ = slot.shape[1]
    cap = EC_CAPACITY_FACTOR * t_ // e_
    tb = TOKEN_BLOCK * BLOCKS_PER_STEP
    nb = t_ // TOKEN_BLOCK
    dh = d_ // 2
    return pl.pallas_call(
        functools.partial(_gather_kernel, ne=e_, nb=nb, cap=cap),
        grid_spec=pltpu.PrefetchScalarGridSpec(
            num_scalar_prefetch=2,
            grid=(b_, 2, t_ // tb),
            in_specs=[pl.BlockSpec((1, tb, dh), lambda b, c, i, *_: (b, i, c)),
                      pl.BlockSpec((1, e_, tb), lambda b, c, i, *_: (b, 0, i))],
            out_specs=pl.BlockSpec((1, e_, cap, dh), lambda b, c, i, *_: (b, 0, 0, c)),
        ),
        out_shape=jax.ShapeDtypeStruct((b_, e_, cap, d_), BF16),
        compiler_params=_params("arbitrary", "arbitrary", "arbitrary", vmem=V7X_VMEM_LIMIT),
        name="moe_gather",
    )(tbl, dense, h, slot)


def _expert_kernel(xg_ref, wg_ref, wu_ref, wd_ref, y_ref, acc_ref):
    f = pl.program_id(1)
    n_b, _, cap, _ = xg_ref.shape
    wg = wg_ref[0, 0].astype(BF16)
    wu = wu_ref[0, 0].astype(BF16)
    wd = wd_ref[0, 0].astype(BF16)

    @pl.when(jnp.logical_and(pl.program_id(0) == 0, f == 0))
    def _():
        acc_ref[...] = jnp.zeros(acc_ref.shape, F32)

    for b in range(n_b):
        for r in range(cap // FFN_ROWS):
            rows = pl.ds(r * FFN_ROWS, FFN_ROWS)
            acc_rows = pl.ds((b * cap) + r * FFN_ROWS, FFN_ROWS)
            xr = xg_ref[b, 0, rows, :]
            a = jnp.dot(xr, wg, preferred_element_type=F32)
            u = jnp.dot(xr, wu, preferred_element_type=F32)
            part = jnp.dot((_silu(a) * u).astype(BF16), wd, preferred_element_type=F32)
            total = jnp.where(f == 0, 0.0, acc_ref[acc_rows, :]) + part
            acc_ref[acc_rows, :] = total
            y_ref[b, 0, rows, :] = total.astype(BF16)


def _experts(xg, layer, w_gate, w_up, w_down):
    b_, e_, cap, d_ = xg.shape
    f_ = w_gate.shape[-1]
    ft = min(FFN_TILE, f_)
    return pl.pallas_call(
        _expert_kernel,
        grid=(e_, f_ // ft),
        in_specs=[pl.BlockSpec((b_, 1, cap, d_), lambda e, f: (0, e, 0, 0)),
                  pl.BlockSpec((1, 1, d_, ft), lambda e, f: (layer, e, 0, f)),
                  pl.BlockSpec((1, 1, d_, ft), lambda e, f: (layer, e, 0, f)),
                  pl.BlockSpec((1, 1, ft, d_), lambda e, f: (layer, e, f, 0))],
        out_specs=pl.BlockSpec((b_, 1, cap, d_), lambda e, f: (0, e, 0, 0)),
        out_shape=jax.ShapeDtypeStruct((b_, e_, cap, d_), BF16),
        scratch_shapes=[pltpu.VMEM((b_ * cap, d_), F32)],
        compiler_params=_params("arbitrary", "arbitrary", vmem=V7X_VMEM_LIMIT),
        name="moe_experts",
    )(xg, w_gate, w_up, w_down)


def _combine_kernel(tbl_ref, dense_ref, y_ref, slot_ref, aff_ref, x_ref, gate_ref, out_ref, acc_ref, *, ne, nb, cap):
    b, step = pl.program_id(0), pl.program_id(2)
    n_tok = TOKEN_BLOCK

    def dense_block(sub):
        tb = step * BLOCKS_PER_STEP + sub
        toks = slice(sub * n_tok, (sub + 1) * n_tok)
        starts = [_dense_start(tbl_ref, b, e, tb, ne, nb, cap) for e in range(ne)]
        ids = lax.broadcasted_iota(jnp.int32, (DENSE_WINDOW, n_tok), 0)
        weights = jnp.concatenate(
            [jnp.where(ids == slot_ref[0, e:e + 1, toks] - starts[e], aff_ref[0, e:e + 1, toks], 0.0).astype(BF16)
             for e in range(ne)], axis=0)
        yw = jnp.concatenate([y_ref[0, e, pl.ds(starts[e], DENSE_WINDOW), :] for e in range(ne)], axis=0)
        moe = lax.dot_general(weights, yw, TN_DIMS, preferred_element_type=F32)
        out_ref[0, toks, :] = x_ref[0, toks, :] + gate_ref[0] * moe

    def windowed_block(sub):
        tb = step * BLOCKS_PER_STEP + sub
        toks = slice(sub * n_tok, (sub + 1) * n_tok)
        ids0 = lax.broadcasted_iota(jnp.int32, (SLOT_WINDOW, n_tok), 0)
        acc_ref[...] = jnp.zeros(acc_ref.shape, F32)
        for e in range(ne):
            srow = slot_ref[0, e:e + 1, toks]
            grow = aff_ref[0, e:e + 1, toks]
            start, n_win = _window_plan(tbl_ref, b, e, tb, ne, nb)

            def body(k, carry, e=e, srow=srow, grow=grow, start=start):
                lo = start + k * SLOT_WINDOW
                w0 = pl.multiple_of(jnp.minimum(lo, cap - SLOT_WINDOW), SLOT_ALIGN)
                ids = w0 + ids0
                weights = jnp.where(ids >= lo, jnp.where(ids == srow, grow, 0.0), 0.0).astype(BF16)
                yw = y_ref[0, e, pl.ds(w0, SLOT_WINDOW), :]
                acc_ref[...] = acc_ref[...] + lax.dot_general(weights, yw, TN_DIMS, preferred_element_type=F32)
                return carry

            lax.fori_loop(0, n_win, body, 0)
        out_ref[0, toks, :] = x_ref[0, toks, :] + gate_ref[0] * acc_ref[...]

    _per_block_dispatch(dense_ref, b * nb + step * BLOCKS_PER_STEP, dense_block, windowed_block)


def _combine(y, slot, aff, tbl, dense, x, gate):
    b_, t_, d_ = x.shape
    e_, cap = y.shape[1], y.shape[2]
    tb = TOKEN_BLOCK * BLOCKS_PER_STEP
    nb = t_ // TOKEN_BLOCK
    dh = d_ // 2
    route_spec = pl.BlockSpec((1, e_, tb), lambda b, c, i, *_: (b, 0, i))
    row = pl.BlockSpec((1, tb, dh), lambda b, c, i, *_: (b, i, c))
    return pl.pallas_call(
        functools.partial(_combine_kernel, ne=e_, nb=nb, cap=cap),
        grid_spec=pltpu.PrefetchScalarGridSpec(
            num_scalar_prefetch=2,
            grid=(b_, 2, t_ // tb),
            in_specs=[pl.BlockSpec((1, e_, cap, dh), lambda b, c, i, *_: (b, 0, 0, c)),
                      route_spec, route_spec, row,
                      pl.BlockSpec((1, 1, dh), lambda b, c, i, *_: (b, 0, c))],
            out_specs=row,
            scratch_shapes=[pltpu.VMEM((TOKEN_BLOCK, dh), F32)],
        ),
        out_shape=jax.ShapeDtypeStruct((b_, t_, d_), F32),
        compiler_params=_params("arbitrary", "arbitrary", "arbitrary", vmem=V7X_VMEM_LIMIT),
        name="moe_combine",
    )(tbl, dense, y, slot, aff, x, gate)


def _moe(x, h, aff, gate, layer, w_gate, w_up, w_down):
    slot, tbl, dense = _route(aff)
    xg = _gather(h, slot, tbl, dense)
    y = _experts(xg, layer, w_gate, w_up, w_down)
    return _combine(y, slot, aff, tbl, dense, x, gate)


def _final_norm_kernel(x_ref, gain_ref, out_ref):
    x = x_ref[0]
    out_ref[0] = x * lax.rsqrt(jnp.mean(x * x, axis=-1, keepdims=True) + EPS) * gain_ref[...]


def _final_norm(x, gain):
    b_, t_, d_ = x.shape
    tm = ROW_TILE
    row = pl.BlockSpec((1, tm, d_), lambda b, i: (b, i, 0))
    return pl.pallas_call(
        _final_norm_kernel,
        grid=(b_, t_ // tm),
        in_specs=[row, pl.BlockSpec((1, d_), lambda b, i: (0, 0))],
        out_specs=row,
        out_shape=jax.ShapeDtypeStruct((b_, t_, d_), F32),
        compiler_params=_params("arbitrary", "arbitrary"),
        name="final_norm",
    )(x, gain)


def kernel(x, c, ctx, c_ctx, ada_w, ada_b, norm_mix, norm_ffn, norm_final, hg_w_in, hg_lb_logits, hg_norm, hg_w_out, sc_w_in, sc_conv, sc_w_out, moe_router, moe_w_gate, moe_w_up, moe_w_down):
    b_, t_, d_ = x.shape
    depth = ada_w.shape[0]
    n_ada = ada_w.shape[-1] // d_
    n_heads = d_ // HEAD_DIM
    assert depth == 2 and n_ada == 6 and b_ + 1 <= 8
    assert t_ % GLA_CHUNK == 0 and ctx.shape[1] % GLA_CHUNK == 0 and ROW_TILE % GRID_W == 0

    cond = jnp.concatenate([c, c_ctx[None], jnp.zeros((8 - b_ - 1, d_), F32)], axis=0)
    mod = _ada_vectors(cond, ada_w, ada_b, n_ada)
    vec = lambda i, j: mod[i, j, :b_][:, None, :]
    cvec = lambda i, j: mod[i, j, b_][None, None, :]
    rowtab, coltab, lower = _tables(hg_lb_logits, t_)
    row_of = lambda a, i: a[i][None, :]
    router_t = lambda i: jnp.swapaxes(moe_router[i], 0, 1)

    w_in = hg_w_in[0].astype(BF16)
    lb0 = row_of(lower, 0)
    gain0 = row_of(norm_mix, 0)
    qc, vc, lfc_f, lfc_b = _hgrn_in(ctx, None, gain0, cvec(0, 0), cvec(0, 1), lb0, w_in, with_gate=False)
    zeros = jnp.zeros((b_, n_heads, HEAD_DIM, HEAD_DIM), F32)
    _, _, s_f, s_b = _gla_bidir(qc, vc, lfc_f, lfc_b, zeros, zeros, lb0)
    x0, q, v, lf_f, lf_b, g = _hgrn_in(x, (rowtab, coltab), gain0, vec(0, 0), vec(0, 1), lb0, w_in, with_gate=True)
    o_f, o_b, _, _ = _gla_bidir(q, v, lf_f, lf_b, s_f, s_b, lb0)
    x1, h, aff = _mixer_out(_hgrn_out_kernel, "hgrn_out", [o_f, o_b, g], [], x0,
                            [row_of(hg_norm, 0), hg_w_out[0].astype(BF16)],
                            vec(0, 2), row_of(norm_ffn, 0), vec(0, 3), vec(0, 4), router_t(0))
    x2 = _moe(x1, h, aff, vec(0, 5), 0, moe_w_gate, moe_w_up, moe_w_down)

    bg, cu = _conv_in(x2, row_of(norm_mix, 1), vec(1, 0), vec(1, 1), sc_w_in[0].astype(BF16))
    x3, h, aff = _mixer_out(_conv_out_kernel, "conv_out", [cu, bg], [cu, cu], x2,
                            [sc_conv[0], sc_w_out[0].astype(BF16)],
                            vec(1, 2), row_of(norm_ffn, 1), vec(1, 3), vec(1, 4), router_t(1))
    x4 = _moe(x3, h, aff, vec(1, 5), 1, moe_w_gate, moe_w_up, moe_w_down)
    return _final_norm(x4, norm_final[None, :])
```

```python
import functools
import math

import numpy as np
import jax
import jax.numpy as jnp
from jax import lax
from jax.experimental import pallas as pl
from jax.experimental.pallas import tpu as pltpu

F32 = jnp.float32
BF16 = jnp.bfloat16

EPS = 1e-6
POS_TEMP = 10000.0
GRID_W = 64
HEAD_DIM = 128
EC_CAPACITY_FACTOR = 2
GLA_CHUNK = 256
GLA_LEVELS = 8
GLA_HEADS_PER_STEP = 4
GLA_MERGED = 5
GLA_MAX_EXPONENT = 80.0
ROW_TILE = 512
SUB_ROWS = 256
TOKEN_BLOCK = 256
BLOCKS_PER_STEP = 4
COMBINE_BLOCKS = 2
SLOT_WINDOW = 64
DENSE_WINDOW = 128
SLOT_ALIGN = 16
FFN_TILE = 1024
FFN_ROWS = 256
V7X_VMEM_LIMIT = 56 * 1024 * 1024

NT_DIMS = (((1,), (1,)), ((), ()))
TN_DIMS = (((0,), (0,)), ((), ()))


def _params(*sem, vmem=None):
    return pltpu.CompilerParams(dimension_semantics=sem, vmem_limit_bytes=vmem)


def _dot_bf16x3(a, b, dims):
    a0, b0 = a.astype(BF16), b.astype(BF16)
    a1 = (a - a0.astype(F32)).astype(BF16)
    b1 = (b - b0.astype(F32)).astype(BF16)
    d = lambda x, y: lax.dot_general(x, y, dims, preferred_element_type=F32)
    return (d(a0, b1) + d(a1, b0)) + d(a0, b0)


def _sub_tiles(n_rows):
    sub = min(SUB_ROWS, n_rows)
    return [slice(s, s + sub) for s in range(0, n_rows, sub)]


def _sigmoid(x):
    return 1.0 / (1.0 + jnp.exp(-x))


def _silu(x):
    return x * _sigmoid(x)


def _modulate(x, gain, shift, scale):
    y = x * lax.rsqrt(jnp.mean(x * x, axis=-1, keepdims=True) + EPS)
    return (y * gain) * (1.0 + scale) + shift


def _ada_kernel(cond_ref, w_ref, b_ref, out_ref):
    s = _silu(cond_ref[...])
    out_ref[0, 0] = _dot_bf16x3(s, w_ref[0], (((1,), (0,)), ((), ()))) + b_ref[0, 0]


def _ada_vectors(cond, ada_w, ada_b, n_ada):
    depth, d_, _ = ada_w.shape
    return pl.pallas_call(
        _ada_kernel,
        grid=(depth, n_ada),
        in_specs=[pl.BlockSpec((8, d_), lambda i, j: (0, 0)),
                  pl.BlockSpec((1, d_, d_), lambda i, j: (i, 0, j)),
                  pl.BlockSpec((1, 1, 1, d_), lambda i, j: (i, j, 0, 0))],
        out_specs=pl.BlockSpec((1, 1, 8, d_), lambda i, j: (i, j, 0, 0)),
        out_shape=jax.ShapeDtypeStruct((depth, n_ada, 8, d_), F32),
        compiler_params=_params("arbitrary", "arbitrary"),
        name="ada_vectors",
    )(cond, ada_w, ada_b.reshape(depth, n_ada, 1, d_))


def _tables_kernel(lb_logits_ref, rowtab_ref, coltab_ref, lb_ref, *, n_freq):
    def table(n_pos):
        p = lax.broadcasted_iota(jnp.int32, (n_pos, n_freq), 0).astype(F32)
        j = lax.broadcasted_iota(jnp.int32, (n_pos, n_freq), 1).astype(F32)
        omega = jnp.exp(j * (-math.log(POS_TEMP) / n_freq))
        ang = p * omega
        return jnp.concatenate([jnp.sin(ang), jnp.cos(ang)], axis=-1)

    rowtab_ref[...] = table(rowtab_ref.shape[0])
    coltab_ref[...] = table(coltab_ref.shape[0])
    logits = lb_logits_ref[...]
    e = jnp.exp(logits - jnp.max(logits, axis=0, keepdims=True))
    sm = e / jnp.sum(e, axis=0, keepdims=True)
    acc = sm[0:1]
    lb_ref[0:1] = acc
    for i in range(1, lb_ref.shape[0]):
        acc = acc + sm[i:i + 1]
        lb_ref[i:i + 1] = acc


def _tables(lb_logits, n_tokens):
    n_lb, d_ = lb_logits.shape
    n_freq = d_ // 4
    rows = n_tokens // GRID_W
    return pl.pallas_call(
        functools.partial(_tables_kernel, n_freq=n_freq),
        out_shape=[jax.ShapeDtypeStruct((rows, 2 * n_freq), F32),
                   jax.ShapeDtypeStruct((GRID_W, 2 * n_freq), F32),
                   jax.ShapeDtypeStruct((n_lb, d_), F32)],
        name="pos_tables",
    )(lb_logits)


def _hgrn_in_kernel(*refs, with_pos, with_gate):
    it = iter(refs)
    x_ref = next(it)
    if with_pos:
        rowtab_ref, coltab_ref = next(it), next(it)
    gain_ref, shift_ref, scale_ref, lb_ref, w_ref = next(it), next(it), next(it), next(it), next(it)
    if with_pos:
        x0_ref = next(it)
    q_ref, v_ref, lff_ref, lfb_ref = next(it), next(it), next(it), next(it)
    g_ref = next(it) if with_gate else None

    d_ = x_ref.shape[-1]
    lb = lb_ref[...]
    for rows in _sub_tiles(x_ref.shape[1]):
        x = x_ref[0, rows]
        if with_pos:
            grid_rows = range(rows.start // GRID_W, rows.stop // GRID_W)
            pos_row = jnp.concatenate(
                [jnp.broadcast_to(rowtab_ref[0, r:r + 1, :], (GRID_W, rowtab_ref.shape[-1])) for r in grid_rows], axis=0)
            pos_col = jnp.concatenate([coltab_ref[...]] * len(grid_rows), axis=0)
            x = x + jnp.concatenate([pos_row, pos_col], axis=-1)
            x0_ref[0, rows] = x
        h = _modulate(x, gain_ref[...], shift_ref[0], scale_ref[0]).astype(BF16)
        part = lambda p: jnp.dot(h, w_ref[:, p * d_:(p + 1) * d_], preferred_element_type=F32)
        q_ref[0, rows] = (part(0) * HEAD_DIM ** -0.5).astype(BF16)
        v_ref[0, rows] = part(1).astype(BF16)
        lff_ref[0, rows] = jnp.log(lb + (1.0 - lb) * _sigmoid(part(2)))
        lfb_ref[0, rows] = jnp.log(lb + (1.0 - lb) * _sigmoid(part(3)))
        if with_gate:
            g_ref[0, rows] = part(4).astype(BF16)


def _hgrn_in(x, tabs, gain, shift, scale, lb, w_in, *, with_gate):
    b_, t_, d_ = x.shape
    tm = min(ROW_TILE, t_)
    with_pos = tabs is not None
    per_sample = lambda a: pl.BlockSpec((1, 1, d_), (lambda b, i: (b, 0, 0)) if a.shape[0] > 1 else (lambda b, i: (0, 0, 0)))
    row = pl.BlockSpec((1, tm, d_), lambda b, i: (b, i, 0))
    vec = pl.BlockSpec((1, d_), lambda b, i: (0, 0))
    args, in_specs = [x], [row]
    if with_pos:
        rowtab, coltab = tabs
        rows_per_tile = tm // GRID_W
        args += [rowtab.reshape(rowtab.shape[0] // rows_per_tile, rows_per_tile, rowtab.shape[1]), coltab]
        in_specs += [pl.BlockSpec((1, rows_per_tile, rowtab.shape[1]), lambda b, i: (i, 0, 0)),
                     pl.BlockSpec(coltab.shape, lambda b, i: (0, 0))]
    args += [gain, shift, scale, lb, w_in]
    in_specs += [vec, per_sample(shift), per_sample(scale), vec, pl.BlockSpec(w_in.shape, lambda b, i: (0, 0))]
    out_shape, out_specs = [], []
    if with_pos:
        out_shape.append(jax.ShapeDtypeStruct((b_, t_, d_), F32))
        out_specs.append(row)
    out_shape += [jax.ShapeDtypeStruct((b_, t_, d_), BF16)] * 2 + [jax.ShapeDtypeStruct((b_, t_, d_), F32)] * 2
    out_specs += [row] * 4
    if with_gate:
        out_shape.append(jax.ShapeDtypeStruct((b_, t_, d_), BF16))
        out_specs.append(row)
    return pl.pallas_call(
        functools.partial(_hgrn_in_kernel, with_pos=with_pos, with_gate=with_gate),
        grid=(b_, t_ // tm),
        in_specs=in_specs, out_specs=out_specs, out_shape=out_shape,
        compiler_params=_params("arbitrary", "arbitrary", vmem=V7X_VMEM_LIMIT),
        name="hgrn_in_latent" if with_pos else "hgrn_in_context",
    )(*args)


def _gla_consts(reverse):
    c = GLA_CHUNK
    idx = np.arange(c)
    rank = (c - 1 - idx) if reverse else idx
    tri = (rank[None, :] <= rank[:, None]).astype(np.float32)
    hc = c // 2
    hrank = rank[:hc] - rank[:hc].min()
    lvl = np.full((hc, hc), -1, np.int32)
    rt, rs = hrank[:, None], hrank[None, :]
    lvl[rt == rs] = 0
    for level in range(1, GLA_LEVELS):
        blk, half = 1 << level, 1 << (level - 1)
        lvl[(rt // blk == rs // blk) & ((rt % blk) >= half) & ((rs % blk) < half)] = level
    return jnp.asarray(tri, BF16), jnp.asarray(lvl)


def _later_group(level, reverse, group):
    rank = (GLA_CHUNK - 1 - 8 * group) if reverse else 8 * group
    return (rank % (1 << level)) >= (1 << (level - 1))


def _boundary_rows(level, reverse):
    c = GLA_CHUNK
    blk, half = 1 << level, 1 << (level - 1)
    rows = []
    for i in range(c):
        rank = (c - 1 - i) if reverse else i
        brank = (rank // blk) * blk + half - 1
        rows.append((c - 1 - brank) if reverse else brank)
    return rows


def _gla_low_levels(q, kk, lf, cum, lvl, bcast, halves, reverse):
    ng = q.shape[0] // 8
    qb, kb = q.astype(BF16), kk.astype(BF16)
    tiles = [jnp.where(lvl == 0, lax.dot_general(qb[hs], kb[hs], NT_DIMS, preferred_element_type=F32), 0.0)
             for hs in halves]
    row8 = lax.broadcasted_iota(jnp.int32, (8, HEAD_DIM), 0)
    rank8 = (7 - row8) if reverse else row8
    for level in range(1, 4):
        sgn8 = jnp.where(((rank8 >> (level - 1)) & 1) == 1, 1.0, -1.0)
        sgn = jnp.concatenate([sgn8] * ng, axis=0)
        later = sgn > 0.0
        if level == 1:
            g = jnp.where(later, lf, 0.0)
        else:
            brow = _boundary_rows(level, reverse)
            pieces = []
            for grp in range(ng):
                first = bcast(brow[8 * grp])
                pieces.append(jnp.where(row8 < 4, first, bcast(brow[8 * grp + 7])) if level == 2 else first)
            g = (cum - jnp.concatenate(pieces, axis=0)) * sgn
        xe = (jnp.where(later, q, kk) * jnp.exp(g)).astype(BF16)
        for h, hs in enumerate(halves):
            s = lax.dot_general(xe[hs], xe[hs], NT_DIMS, preferred_element_type=F32)
            tiles[h] = jnp.where(lvl == level, s, tiles[h])
    return tiles


def _gla_chunk(q_ref, v_ref, lf_ref, lanes, st_ref, cum_ref, tri_ref, lvl_ref, o_ref, reverse, merged):
    c = GLA_CHUNK
    hc, ng = c // 2, c // 8
    halves = (slice(0, hc), slice(hc, c))
    lf = lf_ref[0, :, lanes]
    q = q_ref[0, :, lanes].astype(F32)
    v = v_ref[0, :, lanes]
    kk = 1.0 - jnp.exp(lf)
    hi = lf.astype(BF16)
    lo = (lf - hi.astype(F32)).astype(BF16)
    two = jnp.dot(tri_ref[...], jnp.concatenate([hi, lo], axis=1), preferred_element_type=F32)
    cum = two[:, HEAD_DIM:] + two[:, :HEAD_DIM]
    cum_ref[...] = cum
    lvl = lvl_ref[...]
    bcast = lambda r: jnp.broadcast_to(cum_ref[r:r + 1, :], (8, HEAD_DIM))
    groups = lambda a: [a[8 * i:8 * i + 8] for i in range(a.shape[0] // 8)]

    if merged:
        blk = 1 << merged
        cache, pieces = {}, []
        for grp in range(ng):
            rank = (c - 1 - 8 * grp) if reverse else 8 * grp
            first = (rank // blk) * blk
            row = (c - 1 - first) if reverse else first
            pieces.append(cache.setdefault(row, bcast(row)))
        d = cum - jnp.concatenate(pieces, axis=0)
        xq = (q * jnp.exp(d)).astype(BF16)
        xk = (kk * jnp.exp(-d)).astype(BF16)
        inside = jnp.logical_and(lvl >= 0, lvl <= merged)
        tiles = [jnp.where(inside, lax.dot_general(xq[hs], xk[hs], NT_DIMS, preferred_element_type=F32), 0.0)
                 for hs in halves]
    else:
        tiles = _gla_low_levels(q, kk, lf, cum, lvl, bcast, halves, reverse)

    tile_rows = [groups(t) for t in tiles]
    lvl_rows = groups(lvl)
    q_rows, k_rows, cum_rows = groups(q), groups(kk), groups(cum)
    for level in range(max(4, merged + 1), GLA_LEVELS):
        brow = _boundary_rows(level, reverse)
        cache = {}
        g_rows, x_rows = [], []
        for grp in range(ng):
            cb = cache.setdefault(brow[8 * grp], bcast(brow[8 * grp]))
            later = _later_group(level, reverse, grp)
            g_rows.append(cum_rows[grp] - cb if later else cb - cum_rows[grp])
            x_rows.append(q_rows[grp] if later else k_rows[grp])
        xe = jnp.concatenate(x_rows, axis=0) * jnp.exp(jnp.concatenate(g_rows, axis=0))
        xe_rows = groups(xe)
        for h, hs in enumerate(halves):
            later_groups = [grp for grp in range(h * ng // 2, (h + 1) * ng // 2) if _later_group(level, reverse, grp)]
            qc = jnp.concatenate([xe_rows[grp] for grp in later_groups], axis=0).astype(BF16)
            s = lax.dot_general(qc, xe[hs].astype(BF16), NT_DIMS, preferred_element_type=F32)
            for i, grp in enumerate(later_groups):
                local = grp - h * ng // 2
                tile_rows[h][local] = jnp.where(lvl_rows[local] == level, s[8 * i:8 * i + 8], tile_rows[h][local])

    early, late = (1, 0) if reverse else (0, 1)
    cb = cum_ref[_boundary_rows(GLA_LEVELS, reverse)[0]:_boundary_rows(GLA_LEVELS, reverse)[0] + 1, :]
    ql = (q[halves[late]] * jnp.exp(cum[halves[late]] - cb)).astype(BF16)
    ke = (kk[halves[early]] * jnp.exp(cb - cum[halves[early]])).astype(BF16)
    cross = lax.dot_general(ql, ke, NT_DIMS, preferred_element_type=F32)
    t_a, t_b = (jnp.concatenate(rows, axis=0) for rows in tile_rows)
    zero = jnp.zeros((hc, hc), F32)
    if reverse:
        scores = jnp.concatenate([jnp.concatenate([t_a, cross], axis=1), jnp.concatenate([zero, t_b], axis=1)], axis=0)
    else:
        scores = jnp.concatenate([jnp.concatenate([t_a, zero], axis=1), jnp.concatenate([cross, t_b], axis=1)], axis=0)

    st = st_ref[...]
    qe = (q * jnp.exp(cum)).astype(BF16)
    o = jnp.dot(scores.astype(BF16), v, preferred_element_type=F32)
    o = o + lax.dot_general(qe, st.astype(BF16), NT_DIMS, preferred_element_type=F32)
    o_ref[0, :, lanes] = o.astype(o_ref.dtype)
    last_row = 0 if reverse else c - 1
    last = cum_ref[last_row:last_row + 1, :]
    ke_all = (kk * jnp.exp(last - cum)).astype(BF16)
    st_ref[...] = st * jnp.exp(last) + lax.dot_general(v, ke_all, TN_DIMS, preferred_element_type=F32)


def _gla_kernel(qf_ref, vf_ref, lff_ref, qb_ref, vb_ref, lfb_ref, s0f_ref, s0b_ref,
                trif_ref, lvlf_ref, trib_ref, lvlb_ref,
                of_ref, ob_ref, sff_ref, sfb_ref, stf_ref, stb_ref, cumf_ref, cumb_ref, *, merged):
    j = pl.program_id(2)

    @pl.when(j == 0)
    def _():
        stf_ref[...] = s0f_ref[0]
        stb_ref[...] = s0b_ref[0]

    for k in range(stf_ref.shape[0]):
        lanes = slice(k * HEAD_DIM, (k + 1) * HEAD_DIM)
        _gla_chunk(qf_ref, vf_ref, lff_ref, lanes, stf_ref.at[k], cumf_ref.at[k], trif_ref, lvlf_ref, of_ref,
                   False, merged)
        _gla_chunk(qb_ref, vb_ref, lfb_ref, lanes, stb_ref.at[k], cumb_ref.at[k], trib_ref, lvlb_ref, ob_ref,
                   True, merged)

    @pl.when(j == pl.num_programs(2) - 1)
    def _():
        sff_ref[0] = stf_ref[...]
        sfb_ref[0] = stb_ref[...]


def _gla_bidir(q, v, lf_f, lf_b, s0f, s0b, lb):
    worst = (2 ** GLA_MERGED - 1) * jnp.max(-jnp.log(lb))
    run = lambda merged: (lambda *a: _gla_call(*a, merged=merged))
    return lax.cond(worst < GLA_MAX_EXPONENT, run(GLA_MERGED), run(0), q, v, lf_f, lf_b, s0f, s0b)


def _gla_call(q, v, lf_f, lf_b, s0f, s0b, *, merged):
    b_, t_, d_ = q.shape
    h_ = d_ // HEAD_DIM
    hp = math.gcd(GLA_HEADS_PER_STEP, h_)
    c = GLA_CHUNK
    n = t_ // c
    fwd = lambda b, h, j: (b, j, h)
    bwd = lambda b, h, j: (b, n - 1 - j, h)
    st = lambda b, h, j: (b, h, 0, 0)
    const = lambda b, h, j: (0, 0)
    blk = lambda im: pl.BlockSpec((1, c, hp * HEAD_DIM), im)
    st_spec = pl.BlockSpec((1, hp, HEAD_DIM, HEAD_DIM), st)
    cspecs = [pl.BlockSpec((c, c), const), pl.BlockSpec((c // 2, c // 2), const)]
    state = pltpu.VMEM((hp, HEAD_DIM, HEAD_DIM), F32)
    cum = pltpu.VMEM((hp, c, HEAD_DIM), F32)
    return pl.pallas_call(
        functools.partial(_gla_kernel, merged=merged),
        grid=(b_, h_ // hp, n),
        in_specs=[blk(fwd), blk(fwd), blk(fwd), blk(bwd), blk(bwd), blk(bwd), st_spec, st_spec] + cspecs + cspecs,
        out_specs=[blk(fwd), blk(bwd), st_spec, st_spec],
        out_shape=[jax.ShapeDtypeStruct((b_, t_, d_), BF16)] * 2
                  + [jax.ShapeDtypeStruct((b_, h_, HEAD_DIM, HEAD_DIM), F32)] * 2,
        scratch_shapes=[state, state, cum, cum],
        compiler_params=_params("arbitrary", "arbitrary", "arbitrary"),
        name="gla_merged" if merged else "gla_split",
    )(q, v, lf_f, q, v, lf_b, s0f, s0b, *_gla_consts(False), *_gla_consts(True))


def _mixer_epilogue(y, rows, x_ref, w_ref, gate_ref, gain_ref, shift_ref, scale_ref, wr_ref, x1_ref, h_ref, aff_ref):
    y = jnp.dot(y.astype(BF16), w_ref[...], preferred_element_type=F32)
    x1 = x_ref[0, rows] + gate_ref[0] * y
    x1_ref[0, rows] = x1
    hf = _modulate(x1, gain_ref[...], shift_ref[0], scale_ref[0])
    h_ref[0, rows] = hf.astype(BF16)
    logits = _dot_bf16x3(wr_ref[...], hf, NT_DIMS)
    e = jnp.exp(logits - jnp.max(logits, axis=0, keepdims=True))
    aff_ref[0, :, rows] = e / jnp.sum(e, axis=0, keepdims=True)


def _hgrn_out_kernel(of_ref, ob_ref, g_ref, x_ref, hnorm_ref, w_ref, *rest):
    hn = hnorm_ref[...]
    for rows in _sub_tiles(x_ref.shape[1]):
        o = of_ref[0, rows].astype(F32) + ob_ref[0, rows].astype(F32)
        heads = []
        for h in range(o.shape[-1] // HEAD_DIM):
            oh = o[:, h * HEAD_DIM:(h + 1) * HEAD_DIM]
            heads.append(oh * lax.rsqrt(jnp.mean(oh * oh, axis=-1, keepdims=True) + EPS) * hn)
        y = jnp.concatenate(heads, axis=-1) * _silu(g_ref[0, rows].astype(F32))
        _mixer_epilogue(y, rows, x_ref, w_ref, *rest)


def _conv_out_kernel(cu_ref, bg_ref, prev_ref, next_ref, x_ref, wc_ref, w_ref, *rest):
    i = pl.program_id(1)
    cu = cu_ref[0].astype(F32)
    tm = cu.shape[0]
    rid = lax.broadcasted_iota(jnp.int32, cu.shape, 0)
    before = jnp.where(i == 0, 0.0, prev_ref[0, 7:8, :].astype(F32))
    after = jnp.where(i == pl.num_programs(1) - 1, 0.0, next_ref[0, 0:1, :].astype(F32))
    left = jnp.where(rid == 0, before, pltpu.roll(cu, 1, 0))
    right = jnp.where(rid == tm - 1, after, pltpu.roll(cu, tm - 1, 0))
    wc = wc_ref[...]
    conv = left * wc[0:1] + cu * wc[1:2] + right * wc[2:3]
    for rows in _sub_tiles(tm):
        _mixer_epilogue(bg_ref[0, rows].astype(F32) * conv[rows], rows, x_ref, w_ref, *rest)


def _mixer_out(kernel, name, row_args, halo_args, x, small_args, gate, gain, shift, scale, w_router_t):
    b_, t_, d_ = x.shape
    e_ = w_router_t.shape[0]
    tm = min(ROW_TILE, t_)
    row = pl.BlockSpec((1, tm, d_), lambda b, i: (b, i, 0))
    per_sample = pl.BlockSpec((1, 1, d_), lambda b, i: (b, 0, 0))
    whole = lambda a: pl.BlockSpec(a.shape, lambda b, i: (0,) * a.ndim)
    n8 = t_ // 8
    halo_specs = [pl.BlockSpec((1, 8, d_), lambda b, i: (b, jnp.maximum(i * (tm // 8) - 1, 0), 0)),
                  pl.BlockSpec((1, 8, d_), lambda b, i: (b, jnp.minimum((i + 1) * (tm // 8), n8 - 1), 0))]
    return pl.pallas_call(
        kernel,
        grid=(b_, t_ // tm),
        in_specs=[row] * len(row_args) + halo_specs[:len(halo_args)] + [row]
                 + [whole(a) for a in small_args] + [per_sample, whole(gain), per_sample, per_sample, whole(w_router_t)],
        out_specs=[row, row, pl.BlockSpec((1, e_, tm), lambda b, i: (b, 0, i))],
        out_shape=[jax.ShapeDtypeStruct((b_, t_, d_), F32), jax.ShapeDtypeStruct((b_, t_, d_), BF16),
                   jax.ShapeDtypeStruct((b_, e_, t_), F32)],
        compiler_params=_params("arbitrary", "arbitrary", vmem=V7X_VMEM_LIMIT),
        name=name,
    )(*row_args, *halo_args, x, *small_args, gate, gain, shift, scale, w_router_t)


def _conv_in_kernel(x_ref, gain_ref, shift_ref, scale_ref, w_ref, bg_ref, cu_ref):
    d_ = x_ref.shape[-1]
    for rows in _sub_tiles(x_ref.shape[1]):
        h = _modulate(x_ref[0, rows], gain_ref[...], shift_ref[0], scale_ref[0]).astype(BF16)
        part = lambda p: jnp.dot(h, w_ref[:, p * d_:(p + 1) * d_], preferred_element_type=F32)
        bg_ref[0, rows] = part(0).astype(BF16)
        cu_ref[0, rows] = (part(1) * part(2)).astype(BF16)


def _conv_in(x, gain, shift, scale, w_in):
    b_, t_, d_ = x.shape
    tm = min(ROW_TILE, t_)
    row = pl.BlockSpec((1, tm, d_), lambda b, i: (b, i, 0))
    per_sample = pl.BlockSpec((1, 1, d_), lambda b, i: (b, 0, 0))
    return pl.pallas_call(
        _conv_in_kernel,
        grid=(b_, t_ // tm),
        in_specs=[row, pl.BlockSpec((1, d_), lambda b, i: (0, 0)), per_sample, per_sample,
                  pl.BlockSpec(w_in.shape, lambda b, i: (0, 0))],
        out_specs=[row, row],
        out_shape=[jax.ShapeDtypeStruct((b_, t_, d_), BF16)] * 2,
        compiler_params=_params("arbitrary", "arbitrary", vmem=V7X_VMEM_LIMIT),
        name="conv_in",
    )(x, gain, shift, scale, w_in)


def _route_kernel(aff_ref, tri_ref, blockind_ref, slot_ref, base_ref, dense_ref, *, cap):
    aff = aff_ref[0]
    e_, t_ = aff.shape

    def as_float(word):
        return pltpu.bitcast(word, F32)

    def count_ge(th):
        return jnp.sum(jnp.where(aff >= th, 1.0, 0.0), axis=1, keepdims=True)

    def search(_, carry):
        lo, hi = carry
        mid = lo + ((hi - lo + 1) >> 1)
        ok = count_ge(as_float(mid)) >= cap
        return jnp.where(ok, mid, lo), jnp.where(ok, hi, mid - 1)

    lo0 = jnp.zeros((e_, 1), jnp.int32)
    hi0 = jnp.full((e_, 1), 0x7F7FFFFF, jnp.int32)
    kth, _ = lax.fori_loop(0, 32, search, (lo0, hi0))
    above = aff >= as_float(kth + 1)
    tied = jnp.logical_and(aff >= as_float(kth), jnp.logical_not(above))
    need = cap - jnp.sum(jnp.where(above, 1.0, 0.0), axis=1, keepdims=True)
    tri = tri_ref[...]
    tb = tri.shape[0]
    carry_t = jnp.zeros((e_, 1), F32)
    carry_s = jnp.zeros((e_, 1), F32)
    sel_blocks = []
    for j in range(t_ // tb):
        cols = slice(j * tb, (j + 1) * tb)
        tied_j = tied[:, cols]
        ct = jnp.dot(jnp.where(tied_j, 1.0, 0.0).astype(BF16), tri, preferred_element_type=F32) + carry_t
        carry_t = ct[:, tb - 1:tb]
        sel_j = jnp.where(above[:, cols], 1.0, jnp.where(tied_j & (ct <= need), 1.0, 0.0))
        cs = jnp.dot(sel_j.astype(BF16), tri, preferred_element_type=F32) + carry_s
        carry_s = cs[:, tb - 1:tb]
        slot_ref[0, :, cols] = jnp.where(sel_j > 0.0, cs - 1.0, -1.0).astype(jnp.int32)
        sel_blocks.append(sel_j.astype(BF16))
    sel = jnp.concatenate(sel_blocks, axis=1)
    counts = jnp.dot(sel, blockind_ref[...], preferred_element_type=F32)
    base, end = counts[:, :128], counts[:, 128:]
    base_ref[0] = base.astype(jnp.int32)
    span = end - jnp.floor(base * (1.0 / SLOT_ALIGN)) * SLOT_ALIGN
    dense = jnp.max(span, axis=0, keepdims=True) <= DENSE_WINDOW
    dense_ref[0] = jnp.where(dense, 1, 0).astype(jnp.int32)


def _route(aff):
    b_, e_, t_ = aff.shape
    cap = EC_CAPACITY_FACTOR * t_ // e_
    tb = TOKEN_BLOCK
    nb = t_ // tb
    tri = jnp.asarray(np.triu(np.ones((tb, tb), np.float32)), BF16)
    tok, col = np.arange(t_)[:, None], np.arange(128)[None, :]
    blockind = np.concatenate([(tok < col * tb) & (col <= nb), (tok < (col + 1) * tb) & (col < nb)], axis=1)
    slot, base, dense = pl.pallas_call(
        functools.partial(_route_kernel, cap=cap),
        grid=(b_,),
        in_specs=[pl.BlockSpec((1, e_, t_), lambda b: (b, 0, 0)),
                  pl.BlockSpec((tb, tb), lambda b: (0, 0)),
                  pl.BlockSpec((t_, 256), lambda b: (0, 0))],
        out_specs=[pl.BlockSpec((1, e_, t_), lambda b: (b, 0, 0)), pl.BlockSpec((1, e_, 128), lambda b: (b, 0, 0)),
                   pl.BlockSpec((1, 1, 128), lambda b: (b, 0, 0))],
        out_shape=[jax.ShapeDtypeStruct((b_, e_, t_), jnp.int32), jax.ShapeDtypeStruct((b_, e_, 128), jnp.int32),
                   jax.ShapeDtypeStruct((b_, 1, 128), jnp.int32)],
        compiler_params=_params("arbitrary"),
        name="route",
    )(aff, tri, jnp.asarray(blockind.astype(np.float32), BF16))
    return slot, base[:, :, :nb + 1].reshape(-1), dense[:, 0, :nb].reshape(-1)


def _window_plan(tbl_ref, b, e, tb, ne, nb):
    idx = (b * ne + e) * (nb + 1) + tb
    base, end = tbl_ref[idx], tbl_ref[idx + 1]
    start = (base >> 4) << 4
    n_win = jnp.where(end > base, (end - start + SLOT_WINDOW - 1) >> 6, 0)
    return start, n_win


def _dense_start(tbl_ref, b, e, tb, ne, nb, cap):
    base = tbl_ref[(b * ne + e) * (nb + 1) + tb]
    return pl.multiple_of(jnp.minimum((base >> 4) << 4, cap - DENSE_WINDOW), SLOT_ALIGN)


def _gather_kernel(tbl_ref, dense_ref, h_ref, slot_ref, xg_ref, *, ne, nb, cap):
    b, step = pl.program_id(0), pl.program_id(2)
    n_tok = TOKEN_BLOCK

    @pl.when(step == 0)
    def _():
        xg_ref[...] = jnp.zeros(xg_ref.shape, xg_ref.dtype)

    def dense_block(sub):
        tb = step * BLOCKS_PER_STEP + sub
        toks = slice(sub * n_tok, (sub + 1) * n_tok)
        starts = [_dense_start(tbl_ref, b, e, tb, ne, nb, cap) for e in range(ne)]
        ids = lax.broadcasted_iota(jnp.int32, (DENSE_WINDOW, n_tok), 0)
        onehot = jnp.concatenate(
            [jnp.where(ids == slot_ref[0, e:e + 1, toks] - starts[e], 1.0, 0.0).astype(BF16) for e in range(ne)],
            axis=0)
        rows = jnp.dot(onehot, h_ref[0, toks, :], preferred_element_type=F32).astype(BF16)
        for e in range(ne):
            win = xg_ref.at[0, e, pl.ds(starts[e], DENSE_WINDOW), :]
            win[...] = win[...] + rows[e * DENSE_WINDOW:(e + 1) * DENSE_WINDOW]

    def windowed_block(sub):
        tb = step * BLOCKS_PER_STEP + sub
        toks = slice(sub * n_tok, (sub + 1) * n_tok)
        h = h_ref[0, toks, :]
        ids0 = lax.broadcasted_iota(jnp.int32, (SLOT_WINDOW, n_tok), 0)
        for e in range(ne):
            srow = slot_ref[0, e:e + 1, toks]
            start, n_win = _window_plan(tbl_ref, b, e, tb, ne, nb)

            def body(k, carry, e=e, srow=srow, start=start):
                lo = start + k * SLOT_WINDOW
                w0 = pl.multiple_of(jnp.minimum(lo, cap - SLOT_WINDOW), SLOT_ALIGN)
                ids = w0 + ids0
                onehot = jnp.where(ids >= lo, jnp.where(ids == srow, 1.0, 0.0), 0.0).astype(BF16)
                rows = jnp.dot(onehot, h, preferred_element_type=F32)
                win = xg_ref.at[0, e, pl.ds(w0, SLOT_WINDOW), :]
                win[...] = win[...] + rows.astype(BF16)
                return carry

            lax.fori_loop(0, n_win, body, 0)

    _per_block_dispatch(dense_ref, b * nb + step * BLOCKS_PER_STEP, BLOCKS_PER_STEP, dense_block, windowed_block)


def _per_block_dispatch(dense_ref, first, n_blocks, dense_block, windowed_block):
    flags = [dense_ref[first + sub] > 0 for sub in range(n_blocks)]
    all_dense = functools.reduce(jnp.logical_and, flags)

    @pl.when(all_dense)
    def _():
        for sub in range(n_blocks):
            dense_block(sub)

    @pl.when(jnp.logical_not(all_dense))
    def _():
        for sub in range(n_blocks):
            pl.when(flags[sub])(functools.partial(dense_block, sub))
            pl.when(jnp.logical_not(flags[sub]))(functools.partial(windowed_block, sub))


def _gather(h, slot, tbl, dense):
    b_, t_, d_ = h.shape
    e_ = slot.shape[1]
    cap = EC_CAPACITY_FACTOR * t_ // e_
    tb = TOKEN_BLOCK * BLOCKS_PER_STEP
    nb = t_ // TOKEN_BLOCK
    dh = d_ // 2
    return pl.pallas_call(
        functools.partial(_gather_kernel, ne=e_, nb=nb, cap=cap),
        grid_spec=pltpu.PrefetchScalarGridSpec(
            num_scalar_prefetch=2,
            grid=(b_, 2, t_ // tb),
            in_specs=[pl.BlockSpec((1, tb, dh), lambda b, c, i, *_: (b, i, c)),
                      pl.BlockSpec((1, e_, tb), lambda b, c, i, *_: (b, 0, i))],
            out_specs=pl.BlockSpec((1, e_, cap, dh), lambda b, c, i, *_: (b, 0, 0, c)),
        ),
        out_shape=jax.ShapeDtypeStruct((b_, e_, cap, d_), BF16),
        compiler_params=_params("arbitrary", "arbitrary", "arbitrary", vmem=V7X_VMEM_LIMIT),
        name="moe_gather",
    )(tbl, dense, h, slot)


def _expert_kernel(xg_ref, wg_ref, wu_ref, wd_ref, y_ref, acc_ref):
    f = pl.program_id(1)
    n_b, _, cap, _ = xg_ref.shape
    wg = wg_ref[0, 0].astype(BF16)
    wu = wu_ref[0, 0].astype(BF16)
    wd = wd_ref[0, 0].astype(BF16)

    @pl.when(jnp.logical_and(pl.program_id(0) == 0, f == 0))
    def _():
        acc_ref[...] = jnp.zeros(acc_ref.shape, F32)

    for b in range(n_b):
        for r in range(cap // FFN_ROWS):
            rows = pl.ds(r * FFN_ROWS, FFN_ROWS)
            acc_rows = pl.ds((b * cap) + r * FFN_ROWS, FFN_ROWS)
            xr = xg_ref[b, 0, rows, :]
            a = jnp.dot(xr, wg, preferred_element_type=F32)
            u = jnp.dot(xr, wu, preferred_element_type=F32)
            part = jnp.dot((_silu(a) * u).astype(BF16), wd, preferred_element_type=F32)
            total = jnp.where(f == 0, 0.0, acc_ref[acc_rows, :]) + part
            acc_ref[acc_rows, :] = total
            y_ref[b, 0, rows, :] = total.astype(BF16)


def _experts(xg, layer, w_gate, w_up, w_down):
    b_, e_, cap, d_ = xg.shape
    f_ = w_gate.shape[-1]
    ft = min(FFN_TILE, f_)
    return pl.pallas_call(
        _expert_kernel,
        grid=(e_, f_ // ft),
        in_specs=[pl.BlockSpec((b_, 1, cap, d_), lambda e, f: (0, e, 0, 0)),
                  pl.BlockSpec((1, 1, d_, ft), lambda e, f: (layer, e, 0, f)),
                  pl.BlockSpec((1, 1, d_, ft), lambda e, f: (layer, e, 0, f)),
                  pl.BlockSpec((1, 1, ft, d_), lambda e, f: (layer, e, f, 0))],
        out_specs=pl.BlockSpec((b_, 1, cap, d_), lambda e, f: (0, e, 0, 0)),
        out_shape=jax.ShapeDtypeStruct((b_, e_, cap, d_), BF16),
        scratch_shapes=[pltpu.VMEM((b_ * cap, d_), F32)],
        compiler_params=_params("arbitrary", "arbitrary", vmem=V7X_VMEM_LIMIT),
        name="moe_experts",
    )(xg, w_gate, w_up, w_down)


def _combine_kernel(tbl_ref, dense_ref, y_ref, slot_ref, aff_ref, x_ref, gate_ref, *rest, ne, nb, cap, final):
    if final:
        gain_ref, out_ref, acc_ref = rest
    else:
        out_ref, acc_ref = rest
    b, step = pl.program_id(0), pl.program_id(1)
    n_tok = TOKEN_BLOCK

    def finish(toks, moe):
        out = x_ref[0, toks, :] + gate_ref[0] * moe
        if final:
            out = out * lax.rsqrt(jnp.mean(out * out, axis=-1, keepdims=True) + EPS) * gain_ref[...]
        out_ref[0, toks, :] = out

    def dense_block(sub):
        tb = step * COMBINE_BLOCKS + sub
        toks = slice(sub * n_tok, (sub + 1) * n_tok)
        starts = [_dense_start(tbl_ref, b, e, tb, ne, nb, cap) for e in range(ne)]
        ids = lax.broadcasted_iota(jnp.int32, (DENSE_WINDOW, n_tok), 0)
        weights = jnp.concatenate(
            [jnp.where(ids == slot_ref[0, e:e + 1, toks] - starts[e], aff_ref[0, e:e + 1, toks], 0.0).astype(BF16)
             for e in range(ne)], axis=0)
        yw = jnp.concatenate([y_ref[0, e, pl.ds(starts[e], DENSE_WINDOW), :] for e in range(ne)], axis=0)
        finish(toks, lax.dot_general(weights, yw, TN_DIMS, preferred_element_type=F32))

    def windowed_block(sub):
        tb = step * COMBINE_BLOCKS + sub
        toks = slice(sub * n_tok, (sub + 1) * n_tok)
        ids0 = lax.broadcasted_iota(jnp.int32, (SLOT_WINDOW, n_tok), 0)
        acc_ref[...] = jnp.zeros(acc_ref.shape, F32)
        for e in range(ne):
            srow = slot_ref[0, e:e + 1, toks]
            grow = aff_ref[0, e:e + 1, toks]
            start, n_win = _window_plan(tbl_ref, b, e, tb, ne, nb)

            def body(k, carry, e=e, srow=srow, grow=grow, start=start):
                lo = start + k * SLOT_WINDOW
                w0 = pl.multiple_of(jnp.minimum(lo, cap - SLOT_WINDOW), SLOT_ALIGN)
                ids = w0 + ids0
                weights = jnp.where(ids >= lo, jnp.where(ids == srow, grow, 0.0), 0.0).astype(BF16)
                yw = y_ref[0, e, pl.ds(w0, SLOT_WINDOW), :]
                acc_ref[...] = acc_ref[...] + lax.dot_general(weights, yw, TN_DIMS, preferred_element_type=F32)
                return carry

            lax.fori_loop(0, n_win, body, 0)
        finish(toks, acc_ref[...])

    _per_block_dispatch(dense_ref, b * nb + step * COMBINE_BLOCKS, COMBINE_BLOCKS, dense_block, windowed_block)


def _combine(y, slot, aff, tbl, dense, x, gate, final_gain):
    b_, t_, d_ = x.shape
    e_, cap = y.shape[1], y.shape[2]
    tb = TOKEN_BLOCK * COMBINE_BLOCKS
    nb = t_ // TOKEN_BLOCK
    final = final_gain is not None
    route_spec = pl.BlockSpec((1, e_, tb), lambda b, i, *_: (b, 0, i))
    row = pl.BlockSpec((1, tb, d_), lambda b, i, *_: (b, i, 0))
    in_specs = [pl.BlockSpec((1, e_, cap, d_), lambda b, i, *_: (b, 0, 0, 0), pipeline_mode=pl.Buffered(1)),
                route_spec, route_spec, row, pl.BlockSpec((1, 1, d_), lambda b, i, *_: (b, 0, 0))]
    args = [tbl, dense, y, slot, aff, x, gate]
    if final:
        in_specs.append(pl.BlockSpec((1, d_), lambda b, i, *_: (0, 0)))
        args.append(final_gain)
    return pl.pallas_call(
        functools.partial(_combine_kernel, ne=e_, nb=nb, cap=cap, final=final),
        grid_spec=pltpu.PrefetchScalarGridSpec(
            num_scalar_prefetch=2,
            grid=(b_, t_ // tb),
            in_specs=in_specs,
            out_specs=row,
            scratch_shapes=[pltpu.VMEM((TOKEN_BLOCK, d_), F32)],
        ),
        out_shape=jax.ShapeDtypeStruct((b_, t_, d_), F32),
        compiler_params=_params("arbitrary", "arbitrary", vmem=V7X_VMEM_LIMIT),
        name="moe_combine_final" if final else "moe_combine",
    )(*args)


def _moe(x, h, aff, gate, layer, w_gate, w_up, w_down, final_gain=None):
    slot, tbl, dense = _route(aff)
    xg = _gather(h, slot, tbl, dense)
    y = _experts(xg, layer, w_gate, w_up, w_down)
    return _combine(y, slot, aff, tbl, dense, x, gate, final_gain)


def kernel(x, c, ctx, c_ctx, ada_w, ada_b, norm_mix, norm_ffn, norm_final, hg_w_in, hg_lb_logits, hg_norm, hg_w_out, sc_w_in, sc_conv, sc_w_out, moe_router, moe_w_gate, moe_w_up, moe_w_down):
    b_, t_, d_ = x.shape
    depth = ada_w.shape[0]
    n_ada = ada_w.shape[-1] // d_
    n_heads = d_ // HEAD_DIM
    assert depth == 2 and n_ada == 6 and b_ + 1 <= 8
    assert t_ % GLA_CHUNK == 0 and ctx.shape[1] % GLA_CHUNK == 0 and ROW_TILE % GRID_W == 0

    cond = jnp.concatenate([c, c_ctx[None], jnp.zeros((8 - b_ - 1, d_), F32)], axis=0)
    mod = _ada_vectors(cond, ada_w, ada_b, n_ada)
    vec = lambda i, j: mod[i, j, :b_][:, None, :]
    cvec = lambda i, j: mod[i, j, b_][None, None, :]
    rowtab, coltab, lower = _tables(hg_lb_logits, t_)
    row_of = lambda a, i: a[i][None, :]
    router_t = lambda i: jnp.swapaxes(moe_router[i], 0, 1)

    w_in = hg_w_in[0].astype(BF16)
    lb0 = row_of(lower, 0)
    gain0 = row_of(norm_mix, 0)
    qc, vc, lfc_f, lfc_b = _hgrn_in(ctx, None, gain0, cvec(0, 0), cvec(0, 1), lb0, w_in, with_gate=False)
    zeros = jnp.zeros((b_, n_heads, HEAD_DIM, HEAD_DIM), F32)
    _, _, s_f, s_b = _gla_bidir(qc, vc, lfc_f, lfc_b, zeros, zeros, lb0)
    x0, q, v, lf_f, lf_b, g = _hgrn_in(x, (rowtab, coltab), gain0, vec(0, 0), vec(0, 1), lb0, w_in, with_gate=True)
    o_f, o_b, _, _ = _gla_bidir(q, v, lf_f, lf_b, s_f, s_b, lb0)
    x1, h, aff = _mixer_out(_hgrn_out_kernel, "hgrn_out", [o_f, o_b, g], [], x0,
                            [row_of(hg_norm, 0), hg_w_out[0].astype(BF16)],
                            vec(0, 2), row_of(norm_ffn, 0), vec(0, 3), vec(0, 4), router_t(0))
    x2 = _moe(x1, h, aff, vec(0, 5), 0, moe_w_gate, moe_w_up, moe_w_down)

    bg, cu = _conv_in(x2, row_of(norm_mix, 1), vec(1, 0), vec(1, 1), sc_w_in[0].astype(BF16))
    x3, h, aff = _mixer_out(_conv_out_kernel, "conv_out", [cu, bg], [cu, cu], x2,
                            [sc_conv[0], sc_w_out[0].astype(BF16)],
                            vec(1, 2), row_of(norm_ffn, 1), vec(1, 3), vec(1, 4), router_t(1))
    return _moe(x3, h, aff, vec(1, 5), 1, moe_w_gate, moe_w_up, moe_w_down, final_gain=norm_final[None, :])
```

```python
import functools
import math

import numpy as np
import jax
import jax.numpy as jnp
from jax import lax
from jax.experimental import pallas as pl
from jax.experimental.pallas import tpu as pltpu

F32 = jnp.float32
BF16 = jnp.bfloat16

EPS = 1e-6
POS_TEMP = 10000.0
GRID_W = 64
HEAD_DIM = 128
EC_CAPACITY_FACTOR = 2
GLA_CHUNK = 256
GLA_LEVELS = 8
GLA_HEADS_PER_STEP = 4
GLA_MERGED = 5
GLA_MAX_EXPONENT = 80.0
ROW_TILE = 512
SUB_ROWS = 256
TOKEN_BLOCK = 256
BLOCKS_PER_STEP = 4
COMBINE_BLOCKS = 2
SLOT_WINDOW = 64
DENSE_WINDOW = 128
SLOT_ALIGN = 16
FFN_TILE = 1024
FFN_ROWS = 256
V7X_VMEM_LIMIT = 56 * 1024 * 1024

NT_DIMS = (((1,), (1,)), ((), ()))
TN_DIMS = (((0,), (0,)), ((), ()))


def _params(*sem, vmem=None):
    return pltpu.CompilerParams(dimension_semantics=sem, vmem_limit_bytes=vmem)


def _dot_bf16x3(a, b, dims):
    a0, b0 = a.astype(BF16), b.astype(BF16)
    a1 = (a - a0.astype(F32)).astype(BF16)
    b1 = (b - b0.astype(F32)).astype(BF16)
    d = lambda x, y: lax.dot_general(x, y, dims, preferred_element_type=F32)
    return (d(a0, b1) + d(a1, b0)) + d(a0, b0)


def _sub_tiles(n_rows):
    sub = min(SUB_ROWS, n_rows)
    return [slice(s, s + sub) for s in range(0, n_rows, sub)]


def _sigmoid(x):
    return 1.0 / (1.0 + jnp.exp(-x))


def _silu(x):
    return x * _sigmoid(x)


def _modulate(x, gain, shift, scale):
    y = x * lax.rsqrt(jnp.mean(x * x, axis=-1, keepdims=True) + EPS)
    return y * (gain * (1.0 + scale)) + shift


def _ada_kernel(cond_ref, w_ref, b_ref, out_ref):
    s = _silu(cond_ref[...])
    out_ref[0, 0] = _dot_bf16x3(s, w_ref[0], (((1,), (0,)), ((), ()))) + b_ref[0, 0]


def _ada_vectors(cond, ada_w, ada_b, n_ada):
    depth, d_, _ = ada_w.shape
    return pl.pallas_call(
        _ada_kernel,
        grid=(depth, n_ada),
        in_specs=[pl.BlockSpec((8, d_), lambda i, j: (0, 0)),
                  pl.BlockSpec((1, d_, d_), lambda i, j: (i, 0, j)),
                  pl.BlockSpec((1, 1, 1, d_), lambda i, j: (i, j, 0, 0))],
        out_specs=pl.BlockSpec((1, 1, 8, d_), lambda i, j: (i, j, 0, 0)),
        out_shape=jax.ShapeDtypeStruct((depth, n_ada, 8, d_), F32),
        compiler_params=_params("arbitrary", "arbitrary"),
        name="ada_vectors",
    )(cond, ada_w, ada_b.reshape(depth, n_ada, 1, d_))


def _tables_kernel(lb_logits_ref, rowtab_ref, coltab_ref, lb_ref, *, n_freq):
    def table(n_pos):
        p = lax.broadcasted_iota(jnp.int32, (n_pos, n_freq), 0).astype(F32)
        j = lax.broadcasted_iota(jnp.int32, (n_pos, n_freq), 1).astype(F32)
        omega = jnp.exp(j * (-math.log(POS_TEMP) / n_freq))
        ang = p * omega
        return jnp.concatenate([jnp.sin(ang), jnp.cos(ang)], axis=-1)

    rowtab_ref[...] = table(rowtab_ref.shape[0])
    coltab_ref[...] = table(coltab_ref.shape[0])
    logits = lb_logits_ref[...]
    e = jnp.exp(logits - jnp.max(logits, axis=0, keepdims=True))
    sm = e / jnp.sum(e, axis=0, keepdims=True)
    acc = sm[0:1]
    lb_ref[0:1] = acc
    for i in range(1, lb_ref.shape[0]):
        acc = acc + sm[i:i + 1]
        lb_ref[i:i + 1] = acc


def _tables(lb_logits, n_tokens):
    n_lb, d_ = lb_logits.shape
    n_freq = d_ // 4
    rows = n_tokens // GRID_W
    return pl.pallas_call(
        functools.partial(_tables_kernel, n_freq=n_freq),
        out_shape=[jax.ShapeDtypeStruct((rows, 2 * n_freq), F32),
                   jax.ShapeDtypeStruct((GRID_W, 2 * n_freq), F32),
                   jax.ShapeDtypeStruct((n_lb, d_), F32)],
        name="pos_tables",
    )(lb_logits)


def _hgrn_in_kernel(*refs, with_pos, with_gate):
    it = iter(refs)
    x_ref = next(it)
    if with_pos:
        rowtab_ref, coltab_ref = next(it), next(it)
    gain_ref, shift_ref, scale_ref, lb_ref, w_ref = next(it), next(it), next(it), next(it), next(it)
    if with_pos:
        x0_ref = next(it)
    q_ref, v_ref, lff_ref, lfb_ref = next(it), next(it), next(it), next(it)
    g_ref = next(it) if with_gate else None

    d_ = x_ref.shape[-1]
    lb = lb_ref[...]
    for rows in _sub_tiles(x_ref.shape[1]):
        x = x_ref[0, rows]
        if with_pos:
            grid_rows = range(rows.start // GRID_W, rows.stop // GRID_W)
            pos_row = jnp.concatenate(
                [jnp.broadcast_to(rowtab_ref[0, r:r + 1, :], (GRID_W, rowtab_ref.shape[-1])) for r in grid_rows], axis=0)
            pos_col = jnp.concatenate([coltab_ref[...]] * len(grid_rows), axis=0)
            x = x + jnp.concatenate([pos_row, pos_col], axis=-1)
            x0_ref[0, rows] = x
        h = _modulate(x, gain_ref[...], shift_ref[0], scale_ref[0]).astype(BF16)
        part = lambda p: jnp.dot(h, w_ref[:, p * d_:(p + 1) * d_], preferred_element_type=F32)
        q_ref[0, rows] = (part(0) * HEAD_DIM ** -0.5).astype(BF16)
        v_ref[0, rows] = part(1).astype(BF16)
        lff_ref[0, rows] = jnp.log(lb + (1.0 - lb) * _sigmoid(part(2)))
        lfb_ref[0, rows] = jnp.log(lb + (1.0 - lb) * _sigmoid(part(3)))
        if with_gate:
            g_ref[0, rows] = part(4).astype(BF16)


def _hgrn_in(x, tabs, gain, shift, scale, lb, w_in, *, with_gate):
    b_, t_, d_ = x.shape
    tm = min(ROW_TILE, t_)
    with_pos = tabs is not None
    per_sample = lambda a: pl.BlockSpec((1, 1, d_), (lambda b, i: (b, 0, 0)) if a.shape[0] > 1 else (lambda b, i: (0, 0, 0)))
    row = pl.BlockSpec((1, tm, d_), lambda b, i: (b, i, 0))
    vec = pl.BlockSpec((1, d_), lambda b, i: (0, 0))
    args, in_specs = [x], [row]
    if with_pos:
        rowtab, coltab = tabs
        rows_per_tile = tm // GRID_W
        args += [rowtab.reshape(rowtab.shape[0] // rows_per_tile, rows_per_tile, rowtab.shape[1]), coltab]
        in_specs += [pl.BlockSpec((1, rows_per_tile, rowtab.shape[1]), lambda b, i: (i, 0, 0)),
                     pl.BlockSpec(coltab.shape, lambda b, i: (0, 0))]
    args += [gain, shift, scale, lb, w_in]
    in_specs += [vec, per_sample(shift), per_sample(scale), vec, pl.BlockSpec(w_in.shape, lambda b, i: (0, 0))]
    out_shape, out_specs = [], []
    if with_pos:
        out_shape.append(jax.ShapeDtypeStruct((b_, t_, d_), F32))
        out_specs.append(row)
    out_shape += [jax.ShapeDtypeStruct((b_, t_, d_), BF16)] * 2 + [jax.ShapeDtypeStruct((b_, t_, d_), F32)] * 2
    out_specs += [row] * 4
    if with_gate:
        out_shape.append(jax.ShapeDtypeStruct((b_, t_, d_), BF16))
        out_specs.append(row)
    return pl.pallas_call(
        functools.partial(_hgrn_in_kernel, with_pos=with_pos, with_gate=with_gate),
        grid=(b_, t_ // tm),
        in_specs=in_specs, out_specs=out_specs, out_shape=out_shape,
        compiler_params=_params("arbitrary", "arbitrary", vmem=V7X_VMEM_LIMIT),
        name="hgrn_in_latent" if with_pos else "hgrn_in_context",
    )(*args)


def _gla_consts(reverse):
    c = GLA_CHUNK
    idx = np.arange(c)
    rank = (c - 1 - idx) if reverse else idx
    tri = (rank[None, :] <= rank[:, None]).astype(np.float32)
    hc = c // 2
    hrank = rank[:hc] - rank[:hc].min()
    lvl = np.full((hc, hc), -1, np.int32)
    rt, rs = hrank[:, None], hrank[None, :]
    lvl[rt == rs] = 0
    for level in range(1, GLA_LEVELS):
        blk, half = 1 << level, 1 << (level - 1)
        lvl[(rt // blk == rs // blk) & ((rt % blk) >= half) & ((rs % blk) < half)] = level
    return jnp.asarray(tri, BF16), jnp.asarray(lvl)


def _later_group(level, reverse, group):
    rank = (GLA_CHUNK - 1 - 8 * group) if reverse else 8 * group
    return (rank % (1 << level)) >= (1 << (level - 1))


def _boundary_rows(level, reverse):
    c = GLA_CHUNK
    blk, half = 1 << level, 1 << (level - 1)
    rows = []
    for i in range(c):
        rank = (c - 1 - i) if reverse else i
        brank = (rank // blk) * blk + half - 1
        rows.append((c - 1 - brank) if reverse else brank)
    return rows


def _gla_low_levels(q, kk, lf, cum, lvl, bcast, halves, reverse):
    ng = q.shape[0] // 8
    qb, kb = q.astype(BF16), kk.astype(BF16)
    tiles = [jnp.where(lvl == 0, lax.dot_general(qb[hs], kb[hs], NT_DIMS, preferred_element_type=F32), 0.0)
             for hs in halves]
    row8 = lax.broadcasted_iota(jnp.int32, (8, HEAD_DIM), 0)
    rank8 = (7 - row8) if reverse else row8
    for level in range(1, 4):
        sgn8 = jnp.where(((rank8 >> (level - 1)) & 1) == 1, 1.0, -1.0)
        sgn = jnp.concatenate([sgn8] * ng, axis=0)
        later = sgn > 0.0
        if level == 1:
            g = jnp.where(later, lf, 0.0)
        else:
            brow = _boundary_rows(level, reverse)
            pieces = []
            for grp in range(ng):
                first = bcast(brow[8 * grp])
                pieces.append(jnp.where(row8 < 4, first, bcast(brow[8 * grp + 7])) if level == 2 else first)
            g = (cum - jnp.concatenate(pieces, axis=0)) * sgn
        xe = (jnp.where(later, q, kk) * jnp.exp(g)).astype(BF16)
        for h, hs in enumerate(halves):
            s = lax.dot_general(xe[hs], xe[hs], NT_DIMS, preferred_element_type=F32)
            tiles[h] = jnp.where(lvl == level, s, tiles[h])
    return tiles


def _gla_chunk(q_ref, v_ref, lf_ref, lanes, st_ref, cum_ref, tri_ref, lvl_ref, o_ref, reverse, merged):
    c = GLA_CHUNK
    hc, ng = c // 2, c // 8
    halves = (slice(0, hc), slice(hc, c))
    lf = lf_ref[0, :, lanes]
    q = q_ref[0, :, lanes].astype(F32)
    v = v_ref[0, :, lanes]
    kk = 1.0 - jnp.exp(lf)
    hi = lf.astype(BF16)
    lo = (lf - hi.astype(F32)).astype(BF16)
    two = jnp.dot(tri_ref[...], jnp.concatenate([hi, lo], axis=1), preferred_element_type=F32)
    cum = two[:, HEAD_DIM:] + two[:, :HEAD_DIM]
    cum_ref[...] = cum
    lvl = lvl_ref[...]
    bcast = lambda r: jnp.broadcast_to(cum_ref[r:r + 1, :], (8, HEAD_DIM))
    groups = lambda a: [a[8 * i:8 * i + 8] for i in range(a.shape[0] // 8)]

    if merged:
        blk = 1 << merged
        cache, pieces = {}, []
        for grp in range(ng):
            rank = (c - 1 - 8 * grp) if reverse else 8 * grp
            first = (rank // blk) * blk
            row = (c - 1 - first) if reverse else first
            pieces.append(cache.setdefault(row, bcast(row)))
        inside = jnp.logical_and(lvl >= 0, lvl <= merged)
        tiles = []
        for h, hs in enumerate(halves):
            d = cum[hs] - jnp.concatenate(pieces[h * ng // 2:(h + 1) * ng // 2], axis=0)
            xq = (q[hs] * jnp.exp(d)).astype(BF16)
            xk = (kk[hs] * jnp.exp(-d)).astype(BF16)
            tiles.append(jnp.where(inside, lax.dot_general(xq, xk, NT_DIMS, preferred_element_type=F32), 0.0))
    else:
        tiles = _gla_low_levels(q, kk, lf, cum, lvl, bcast, halves, reverse)

    tile_rows = [groups(t) for t in tiles]
    lvl_rows = groups(lvl)
    q_rows, k_rows, cum_rows = groups(q), groups(kk), groups(cum)
    for h in range(2):
        half_groups = range(h * ng // 2, (h + 1) * ng // 2)
        for level in range(max(4, merged + 1), GLA_LEVELS):
            brow = _boundary_rows(level, reverse)
            cache = {}
            g_rows, x_rows = [], []
            for grp in half_groups:
                cb = cache.setdefault(brow[8 * grp], bcast(brow[8 * grp]))
                later = _later_group(level, reverse, grp)
                g_rows.append(cum_rows[grp] - cb if later else cb - cum_rows[grp])
                x_rows.append(q_rows[grp] if later else k_rows[grp])
            xe = jnp.concatenate(x_rows, axis=0) * jnp.exp(jnp.concatenate(g_rows, axis=0))
            xe_rows = groups(xe)
            later_local = [grp - half_groups[0] for grp in half_groups if _later_group(level, reverse, grp)]
            qc = jnp.concatenate([xe_rows[i] for i in later_local], axis=0).astype(BF16)
            s = lax.dot_general(qc, xe.astype(BF16), NT_DIMS, preferred_element_type=F32)
            for i, local in enumerate(later_local):
                tile_rows[h][local] = jnp.where(lvl_rows[local] == level, s[8 * i:8 * i + 8], tile_rows[h][local])

    early, late = (1, 0) if reverse else (0, 1)
    cb = cum_ref[_boundary_rows(GLA_LEVELS, reverse)[0]:_boundary_rows(GLA_LEVELS, reverse)[0] + 1, :]
    ql = (q[halves[late]] * jnp.exp(cum[halves[late]] - cb)).astype(BF16)
    ke = (kk[halves[early]] * jnp.exp(cb - cum[halves[early]])).astype(BF16)
    cross = lax.dot_general(ql, ke, NT_DIMS, preferred_element_type=F32)
    t_a, t_b = (jnp.concatenate(rows, axis=0) for rows in tile_rows)
    zero = jnp.zeros((hc, hc), F32)
    if reverse:
        scores = jnp.concatenate([jnp.concatenate([t_a, cross], axis=1), jnp.concatenate([zero, t_b], axis=1)], axis=0)
    else:
        scores = jnp.concatenate([jnp.concatenate([t_a, zero], axis=1), jnp.concatenate([cross, t_b], axis=1)], axis=0)

    st = st_ref[...]
    qe = (q * jnp.exp(cum)).astype(BF16)
    o = jnp.dot(scores.astype(BF16), v, preferred_element_type=F32)
    o = o + lax.dot_general(qe, st.astype(BF16), NT_DIMS, preferred_element_type=F32)
    o_ref[0, :, lanes] = o.astype(o_ref.dtype)
    last_row = 0 if reverse else c - 1
    last = cum_ref[last_row:last_row + 1, :]
    ke_all = (kk * jnp.exp(last - cum)).astype(BF16)
    st_ref[...] = st * jnp.exp(last) + lax.dot_general(v, ke_all, TN_DIMS, preferred_element_type=F32)


def _gla_kernel(qf_ref, vf_ref, lff_ref, qb_ref, vb_ref, lfb_ref, s0f_ref, s0b_ref,
                trif_ref, lvlf_ref, trib_ref, lvlb_ref,
                of_ref, ob_ref, sff_ref, sfb_ref, stf_ref, stb_ref, cumf_ref, cumb_ref, *, merged):
    j = pl.program_id(2)

    @pl.when(j == 0)
    def _():
        stf_ref[...] = s0f_ref[0]
        stb_ref[...] = s0b_ref[0]

    for k in range(stf_ref.shape[0]):
        lanes = slice(k * HEAD_DIM, (k + 1) * HEAD_DIM)
        _gla_chunk(qf_ref, vf_ref, lff_ref, lanes, stf_ref.at[k], cumf_ref.at[k], trif_ref, lvlf_ref, of_ref,
                   False, merged)
        _gla_chunk(qb_ref, vb_ref, lfb_ref, lanes, stb_ref.at[k], cumb_ref.at[k], trib_ref, lvlb_ref, ob_ref,
                   True, merged)

    @pl.when(j == pl.num_programs(2) - 1)
    def _():
        sff_ref[0] = stf_ref[...]
        sfb_ref[0] = stb_ref[...]


def _gla_bidir(q, v, lf_f, lf_b, s0f, s0b, lb):
    worst = (2 ** GLA_MERGED - 1) * jnp.max(-jnp.log(lb))
    run = lambda merged: (lambda *a: _gla_call(*a, merged=merged))
    return lax.cond(worst < GLA_MAX_EXPONENT, run(GLA_MERGED), run(0), q, v, lf_f, lf_b, s0f, s0b)


def _gla_call(q, v, lf_f, lf_b, s0f, s0b, *, merged):
    b_, t_, d_ = q.shape
    h_ = d_ // HEAD_DIM
    hp = math.gcd(GLA_HEADS_PER_STEP, h_)
    c = GLA_CHUNK
    n = t_ // c
    fwd = lambda b, h, j: (b, j, h)
    bwd = lambda b, h, j: (b, n - 1 - j, h)
    st = lambda b, h, j: (b, h, 0, 0)
    const = lambda b, h, j: (0, 0)
    blk = lambda im: pl.BlockSpec((1, c, hp * HEAD_DIM), im)
    st_spec = pl.BlockSpec((1, hp, HEAD_DIM, HEAD_DIM), st)
    cspecs = [pl.BlockSpec((c, c), const), pl.BlockSpec((c // 2, c // 2), const)]
    state = pltpu.VMEM((hp, HEAD_DIM, HEAD_DIM), F32)
    cum = pltpu.VMEM((hp, c, HEAD_DIM), F32)
    return pl.pallas_call(
        functools.partial(_gla_kernel, merged=merged),
        grid=(b_, h_ // hp, n),
        in_specs=[blk(fwd), blk(fwd), blk(fwd), blk(bwd), blk(bwd), blk(bwd), st_spec, st_spec] + cspecs + cspecs,
        out_specs=[blk(fwd), blk(bwd), st_spec, st_spec],
        out_shape=[jax.ShapeDtypeStruct((b_, t_, d_), BF16)] * 2
                  + [jax.ShapeDtypeStruct((b_, h_, HEAD_DIM, HEAD_DIM), F32)] * 2,
        scratch_shapes=[state, state, cum, cum],
        compiler_params=_params("arbitrary", "arbitrary", "arbitrary"),
        name="gla_merged" if merged else "gla_split",
    )(q, v, lf_f, q, v, lf_b, s0f, s0b, *_gla_consts(False), *_gla_consts(True))


def _mixer_epilogue(y, rows, x_ref, w_ref, gate_ref, gain_ref, shift_ref, scale_ref, wr_ref, x1_ref, h_ref, aff_ref):
    y = jnp.dot(y.astype(BF16), w_ref[...], preferred_element_type=F32)
    x1 = x_ref[0, rows] + gate_ref[0] * y
    x1_ref[0, rows] = x1
    hf = _modulate(x1, gain_ref[...], shift_ref[0], scale_ref[0])
    h_ref[0, rows] = hf.astype(BF16)
    logits = _dot_bf16x3(wr_ref[...], hf, NT_DIMS)
    e = jnp.exp(logits - jnp.max(logits, axis=0, keepdims=True))
    aff_ref[0, :, rows] = e / jnp.sum(e, axis=0, keepdims=True)


def _hgrn_out_kernel(of_ref, ob_ref, g_ref, x_ref, hnorm_ref, w_ref, *rest):
    hn = hnorm_ref[...]
    for rows in _sub_tiles(x_ref.shape[1]):
        o = of_ref[0, rows].astype(F32) + ob_ref[0, rows].astype(F32)
        heads = []
        for h in range(o.shape[-1] // HEAD_DIM):
            oh = o[:, h * HEAD_DIM:(h + 1) * HEAD_DIM]
            heads.append(oh * lax.rsqrt(jnp.mean(oh * oh, axis=-1, keepdims=True) + EPS) * hn)
        y = jnp.concatenate(heads, axis=-1) * _silu(g_ref[0, rows].astype(F32))
        _mixer_epilogue(y, rows, x_ref, w_ref, *rest)


def _conv_mixer_kernel(prev_ref, next_ref, x_ref, shift1_ref, scale1_ref, gain1_ref, win_ref, wc_ref, w_ref, *rest):
    i = pl.program_id(1)
    tm, d_ = x_ref.shape[1], x_ref.shape[2]
    subs = _sub_tiles(tm)
    halo = jnp.concatenate([prev_ref[0], next_ref[0]], axis=0)
    b_gate, cu = [], []
    for k, rows in enumerate(subs):
        xs = x_ref[0, rows]
        if k == 0:
            xs = jnp.concatenate([xs, halo], axis=0)
        h = _modulate(xs, gain1_ref[...], shift1_ref[0], scale1_ref[0]).astype(BF16)
        part = lambda p, hh: jnp.dot(hh, win_ref[:, p * d_:(p + 1) * d_], preferred_element_type=F32)
        n = rows.stop - rows.start
        b_gate.append(part(0, h[:n]))
        cu.append(part(1, h) * part(2, h))
    cu_halo = cu[0][subs[0].stop - subs[0].start:]
    cu = jnp.concatenate([cu[0][:subs[0].stop - subs[0].start]] + cu[1:], axis=0)
    rid = lax.broadcasted_iota(jnp.int32, cu.shape, 0)
    before = jnp.where(i == 0, 0.0, cu_halo[7:8])
    after = jnp.where(i == pl.num_programs(1) - 1, 0.0, cu_halo[8:9])
    left = jnp.where(rid == 0, before, pltpu.roll(cu, 1, 0))
    right = jnp.where(rid == tm - 1, after, pltpu.roll(cu, tm - 1, 0))
    wc = wc_ref[...]
    conv = left * wc[0:1] + cu * wc[1:2] + right * wc[2:3]
    for k, rows in enumerate(subs):
        _mixer_epilogue(b_gate[k] * conv[rows], rows, x_ref, w_ref, *rest)


def _mixer_out(kernel, name, row_args, halo_args, x, sample_args, small_args, gate, gain, shift, scale, w_router_t):
    b_, t_, d_ = x.shape
    e_ = w_router_t.shape[0]
    tm = min(ROW_TILE, t_)
    row = pl.BlockSpec((1, tm, d_), lambda b, i: (b, i, 0))
    per_sample = pl.BlockSpec((1, 1, d_), lambda b, i: (b, 0, 0))
    whole = lambda a: pl.BlockSpec(a.shape, lambda b, i: (0,) * a.ndim)
    n8 = t_ // 8
    halo_specs = [pl.BlockSpec((1, 8, d_), lambda b, i: (b, jnp.maximum(i * (tm // 8) - 1, 0), 0)),
                  pl.BlockSpec((1, 8, d_), lambda b, i: (b, jnp.minimum((i + 1) * (tm // 8), n8 - 1), 0))]
    return pl.pallas_call(
        kernel,
        grid=(b_, t_ // tm),
        in_specs=[row] * len(row_args) + halo_specs[:len(halo_args)] + [row] + [per_sample] * len(sample_args)
                 + [whole(a) for a in small_args] + [per_sample, whole(gain), per_sample, per_sample, whole(w_router_t)],
        out_specs=[row, row, pl.BlockSpec((1, e_, tm), lambda b, i: (b, 0, i))],
        out_shape=[jax.ShapeDtypeStruct((b_, t_, d_), F32), jax.ShapeDtypeStruct((b_, t_, d_), BF16),
                   jax.ShapeDtypeStruct((b_, e_, t_), F32)],
        compiler_params=_params("arbitrary", "arbitrary", vmem=V7X_VMEM_LIMIT),
        name=name,
    )(*row_args, *halo_args, x, *sample_args, *small_args, gate, gain, shift, scale, w_router_t)


def _route_kernel(aff_ref, tri_ref, blockind_ref, slot_ref, base_ref, dense_ref, *, cap):
    aff = aff_ref[0]
    e_, t_ = aff.shape

    def as_float(word):
        return pltpu.bitcast(word, F32)

    def count_ge(th):
        return jnp.sum(jnp.where(aff >= th, 1.0, 0.0), axis=1, keepdims=True)

    def search(_, carry):
        lo, hi = carry
        mid = lo + ((hi - lo + 1) >> 1)
        ok = count_ge(as_float(mid)) >= cap
        return jnp.where(ok, mid, lo), jnp.where(ok, hi, mid - 1)

    lo0 = jnp.zeros((e_, 1), jnp.int32)
    hi0 = jnp.full((e_, 1), 0x7F7FFFFF, jnp.int32)
    kth, _ = lax.fori_loop(0, 32, search, (lo0, hi0))
    above = aff >= as_float(kth + 1)
    tied = jnp.logical_and(aff >= as_float(kth), jnp.logical_not(above))
    need = cap - jnp.sum(jnp.where(above, 1.0, 0.0), axis=1, keepdims=True)
    tri = tri_ref[...]
    tb = tri.shape[0]
    carry_t = jnp.zeros((e_, 1), F32)
    carry_s = jnp.zeros((e_, 1), F32)
    sel_blocks = []
    for j in range(t_ // tb):
        cols = slice(j * tb, (j + 1) * tb)
        tied_j = tied[:, cols]
        ct = jnp.dot(jnp.where(tied_j, 1.0, 0.0).astype(BF16), tri, preferred_element_type=F32) + carry_t
        carry_t = ct[:, tb - 1:tb]
        sel_j = jnp.where(above[:, cols], 1.0, jnp.where(tied_j & (ct <= need), 1.0, 0.0))
        cs = jnp.dot(sel_j.astype(BF16), tri, preferred_element_type=F32) + carry_s
        carry_s = cs[:, tb - 1:tb]
        slot_ref[0, :, cols] = jnp.where(sel_j > 0.0, cs - 1.0, -1.0).astype(jnp.int32)
        sel_blocks.append(sel_j.astype(BF16))
    sel = jnp.concatenate(sel_blocks, axis=1)
    counts = jnp.dot(sel, blockind_ref[...], preferred_element_type=F32)
    base, end = counts[:, :128], counts[:, 128:]
    base_ref[0] = base.astype(jnp.int32)
    span = end - jnp.floor(base * (1.0 / SLOT_ALIGN)) * SLOT_ALIGN
    dense = jnp.max(span, axis=0, keepdims=True) <= DENSE_WINDOW
    dense_ref[0] = jnp.where(dense, 1, 0).astype(jnp.int32)


def _route(aff):
    b_, e_, t_ = aff.shape
    cap = EC_CAPACITY_FACTOR * t_ // e_
    tb = TOKEN_BLOCK
    nb = t_ // tb
    tri = jnp.asarray(np.triu(np.ones((tb, tb), np.float32)), BF16)
    tok, col = np.arange(t_)[:, None], np.arange(128)[None, :]
    blockind = np.concatenate([(tok < col * tb) & (col <= nb), (tok < (col + 1) * tb) & (col < nb)], axis=1)
    slot, base, dense = pl.pallas_call(
        functools.partial(_route_kernel, cap=cap),
        grid=(b_,),
        in_specs=[pl.BlockSpec((1, e_, t_), lambda b: (b, 0, 0)),
                  pl.BlockSpec((tb, tb), lambda b: (0, 0)),
                  pl.BlockSpec((t_, 256), lambda b: (0, 0))],
        out_specs=[pl.BlockSpec((1, e_, t_), lambda b: (b, 0, 0)), pl.BlockSpec((1, e_, 128), lambda b: (b, 0, 0)),
                   pl.BlockSpec((1, 1, 128), lambda b: (b, 0, 0))],
        out_shape=[jax.ShapeDtypeStruct((b_, e_, t_), jnp.int32), jax.ShapeDtypeStruct((b_, e_, 128), jnp.int32),
                   jax.ShapeDtypeStruct((b_, 1, 128), jnp.int32)],
        compiler_params=_params("arbitrary"),
        name="route",
    )(aff, tri, jnp.asarray(blockind.astype(np.float32), BF16))
    return slot, base[:, :, :nb + 1].reshape(-1), dense[:, 0, :nb].reshape(-1)


def _window_plan(tbl_ref, b, e, tb, ne, nb):
    idx = (b * ne + e) * (nb + 1) + tb
    base, end = tbl_ref[idx], tbl_ref[idx + 1]
    start = (base >> 4) << 4
    n_win = jnp.where(end > base, (end - start + SLOT_WINDOW - 1) >> 6, 0)
    return start, n_win


def _dense_start(tbl_ref, b, e, tb, ne, nb, cap):
    base = tbl_ref[(b * ne + e) * (nb + 1) + tb]
    return pl.multiple_of(jnp.minimum((base >> 4) << 4, cap - DENSE_WINDOW), SLOT_ALIGN)


def _gather_kernel(tbl_ref, dense_ref, h_ref, slot_ref, xg_ref, *, ne, nb, cap):
    b, step = pl.program_id(0), pl.program_id(2)
    n_tok = TOKEN_BLOCK

    @pl.when(step == 0)
    def _():
        xg_ref[...] = jnp.zeros(xg_ref.shape, xg_ref.dtype)

    def dense_block(sub):
        tb = step * BLOCKS_PER_STEP + sub
        toks = slice(sub * n_tok, (sub + 1) * n_tok)
        starts = [_dense_start(tbl_ref, b, e, tb, ne, nb, cap) for e in range(ne)]
        ids = lax.broadcasted_iota(jnp.int32, (DENSE_WINDOW, n_tok), 0)
        onehot = jnp.concatenate(
            [jnp.where(ids == slot_ref[0, e:e + 1, toks] - starts[e], 1.0, 0.0).astype(BF16) for e in range(ne)],
            axis=0)
        rows = jnp.dot(onehot, h_ref[0, toks, :], preferred_element_type=F32).astype(BF16)
        for e in range(ne):
            win = xg_ref.at[0, e, pl.ds(starts[e], DENSE_WINDOW), :]
            win[...] = win[...] + rows[e * DENSE_WINDOW:(e + 1) * DENSE_WINDOW]

    def windowed_block(sub):
        tb = step * BLOCKS_PER_STEP + sub
        toks = slice(sub * n_tok, (sub + 1) * n_tok)
        h = h_ref[0, toks, :]
        ids0 = lax.broadcasted_iota(jnp.int32, (SLOT_WINDOW, n_tok), 0)
        for e in range(ne):
            srow = slot_ref[0, e:e + 1, toks]
            start, n_win = _window_plan(tbl_ref, b, e, tb, ne, nb)

            def body(k, carry, e=e, srow=srow, start=start):
                lo = start + k * SLOT_WINDOW
                w0 = pl.multiple_of(jnp.minimum(lo, cap - SLOT_WINDOW), SLOT_ALIGN)
                ids = w0 + ids0
                onehot = jnp.where(ids >= lo, jnp.where(ids == srow, 1.0, 0.0), 0.0).astype(BF16)
                rows = jnp.dot(onehot, h, preferred_element_type=F32)
                win = xg_ref.at[0, e, pl.ds(w0, SLOT_WINDOW), :]
                win[...] = win[...] + rows.astype(BF16)
                return carry

            lax.fori_loop(0, n_win, body, 0)

    _per_block_dispatch(dense_ref, b * nb + step * BLOCKS_PER_STEP, BLOCKS_PER_STEP, dense_block, windowed_block)


def _per_block_dispatch(dense_ref, first, n_blocks, dense_block, windowed_block):
    flags = [dense_ref[first + sub] > 0 for sub in range(n_blocks)]
    all_dense = functools.reduce(jnp.logical_and, flags)

    @pl.when(all_dense)
    def _():
        for sub in range(n_blocks):
            dense_block(sub)

    @pl.when(jnp.logical_not(all_dense))
    def _():
        for sub in range(n_blocks):
            pl.when(flags[sub])(functools.partial(dense_block, sub))
            pl.when(jnp.logical_not(flags[sub]))(functools.partial(windowed_block, sub))


def _gather(h, slot, tbl, dense):
    b_, t_, d_ = h.shape
    e_ = slot.shape[1]
    cap = EC_CAPACITY_FACTOR * t_ // e_
    tb = TOKEN_BLOCK * BLOCKS_PER_STEP
    nb = t_ // TOKEN_BLOCK
    dh = d_ // 2
    return pl.pallas_call(
        functools.partial(_gather_kernel, ne=e_, nb=nb, cap=cap),
        grid_spec=pltpu.PrefetchScalarGridSpec(
            num_scalar_prefetch=2,
            grid=(b_, 2, t_ // tb),
            in_specs=[pl.BlockSpec((1, tb, dh), lambda b, c, i, *_: (b, i, c)),
                      pl.BlockSpec((1, e_, tb), lambda b, c, i, *_: (b, 0, i))],
            out_specs=pl.BlockSpec((1, e_, cap, dh), lambda b, c, i, *_: (b, 0, 0, c)),
        ),
        out_shape=jax.ShapeDtypeStruct((b_, e_, cap, d_), BF16),
        compiler_params=_params("arbitrary", "arbitrary", "arbitrary", vmem=V7X_VMEM_LIMIT),
        name="moe_gather",
    )(tbl, dense, h, slot)


def _expert_kernel(xg_ref, wg_ref, wu_ref, wd_ref, y_ref, acc_ref):
    f = pl.program_id(1)
    n_b, _, cap, _ = xg_ref.shape
    wg = wg_ref[0, 0].astype(BF16)
    wu = wu_ref[0, 0].astype(BF16)
    wd = wd_ref[0, 0].astype(BF16)

    @pl.when(jnp.logical_and(pl.program_id(0) == 0, f == 0))
    def _():
        acc_ref[...] = jnp.zeros(acc_ref.shape, F32)

    for b in range(n_b):
        for r in range(cap // FFN_ROWS):
            rows = pl.ds(r * FFN_ROWS, FFN_ROWS)
            acc_rows = pl.ds((b * cap) + r * FFN_ROWS, FFN_ROWS)
            xr = xg_ref[b, 0, rows, :]
            a = jnp.dot(xr, wg, preferred_element_type=F32)
            u = jnp.dot(xr, wu, preferred_element_type=F32)
            part = jnp.dot((_silu(a) * u).astype(BF16), wd, preferred_element_type=F32)
            total = jnp.where(f == 0, 0.0, acc_ref[acc_rows, :]) + part
            acc_ref[acc_rows, :] = total
            y_ref[b, 0, rows, :] = total.astype(BF16)


def _experts(xg, layer, w_gate, w_up, w_down):
    b_, e_, cap, d_ = xg.shape
    f_ = w_gate.shape[-1]
    ft = min(FFN_TILE, f_)
    return pl.pallas_call(
        _expert_kernel,
        grid=(e_, f_ // ft),
        in_specs=[pl.BlockSpec((b_, 1, cap, d_), lambda e, f: (0, e, 0, 0)),
                  pl.BlockSpec((1, 1, d_, ft), lambda e, f: (layer, e, 0, f)),
                  pl.BlockSpec((1, 1, d_, ft), lambda e, f: (layer, e, 0, f)),
                  pl.BlockSpec((1, 1, ft, d_), lambda e, f: (layer, e, f, 0))],
        out_specs=pl.BlockSpec((b_, 1, cap, d_), lambda e, f: (0, e, 0, 0)),
        out_shape=jax.ShapeDtypeStruct((b_, e_, cap, d_), BF16),
        scratch_shapes=[pltpu.VMEM((b_ * cap, d_), F32)],
        compiler_params=_params("arbitrary", "arbitrary", vmem=V7X_VMEM_LIMIT),
        name="moe_experts",
    )(xg, w_gate, w_up, w_down)


def _combine_kernel(tbl_ref, dense_ref, y_ref, slot_ref, aff_ref, x_ref, gate_ref, *rest, ne, nb, cap, final):
    if final:
        gain_ref, out_ref, acc_ref = rest
    else:
        out_ref, acc_ref = rest
    b, step = pl.program_id(0), pl.program_id(1)
    n_tok = TOKEN_BLOCK

    def finish(toks, moe):
        out = x_ref[0, toks, :] + gate_ref[0] * moe
        if final:
            out = out * lax.rsqrt(jnp.mean(out * out, axis=-1, keepdims=True) + EPS) * gain_ref[...]
        out_ref[0, toks, :] = out

    def dense_block(sub):
        tb = step * COMBINE_BLOCKS + sub
        toks = slice(sub * n_tok, (sub + 1) * n_tok)
        starts = [_dense_start(tbl_ref, b, e, tb, ne, nb, cap) for e in range(ne)]
        ids = lax.broadcasted_iota(jnp.int32, (DENSE_WINDOW, n_tok), 0)
        weights = jnp.concatenate(
            [jnp.where(ids == slot_ref[0, e:e + 1, toks] - starts[e], aff_ref[0, e:e + 1, toks], 0.0).astype(BF16)
             for e in range(ne)], axis=0)
        yw = jnp.concatenate([y_ref[0, e, pl.ds(starts[e], DENSE_WINDOW), :] for e in range(ne)], axis=0)
        finish(toks, lax.dot_general(weights, yw, TN_DIMS, preferred_element_type=F32))

    def windowed_block(sub):
        tb = step * COMBINE_BLOCKS + sub
        toks = slice(sub * n_tok, (sub + 1) * n_tok)
        ids0 = lax.broadcasted_iota(jnp.int32, (SLOT_WINDOW, n_tok), 0)
        acc_ref[...] = jnp.zeros(acc_ref.shape, F32)
        for e in range(ne):
            srow = slot_ref[0, e:e + 1, toks]
            grow = aff_ref[0, e:e + 1, toks]
            start, n_win = _window_plan(tbl_ref, b, e, tb, ne, nb)

            def body(k, carry, e=e, srow=srow, grow=grow, start=start):
                lo = start + k * SLOT_WINDOW
                w0 = pl.multiple_of(jnp.minimum(lo, cap - SLOT_WINDOW), SLOT_ALIGN)
                ids = w0 + ids0
                weights = jnp.where(ids >= lo, jnp.where(ids == srow, grow, 0.0), 0.0).astype(BF16)
                yw = y_ref[0, e, pl.ds(w0, SLOT_WINDOW), :]
                acc_ref[...] = acc_ref[...] + lax.dot_general(weights, yw, TN_DIMS, preferred_element_type=F32)
                return carry

            lax.fori_loop(0, n_win, body, 0)
        finish(toks, acc_ref[...])

    _per_block_dispatch(dense_ref, b * nb + step * COMBINE_BLOCKS, COMBINE_BLOCKS, dense_block, windowed_block)


def _combine(y, slot, aff, tbl, dense, x, gate, final_gain):
    b_, t_, d_ = x.shape
    e_, cap = y.shape[1], y.shape[2]
    tb = TOKEN_BLOCK * COMBINE_BLOCKS
    nb = t_ // TOKEN_BLOCK
    final = final_gain is not None
    route_spec = pl.BlockSpec((1, e_, tb), lambda b, i, *_: (b, 0, i))
    row = pl.BlockSpec((1, tb, d_), lambda b, i, *_: (b, i, 0))
    in_specs = [pl.BlockSpec((1, e_, cap, d_), lambda b, i, *_: (b, 0, 0, 0), pipeline_mode=pl.Buffered(1)),
                route_spec, route_spec, row, pl.BlockSpec((1, 1, d_), lambda b, i, *_: (b, 0, 0))]
    args = [tbl, dense, y, slot, aff, x, gate]
    if final:
        in_specs.append(pl.BlockSpec((1, d_), lambda b, i, *_: (0, 0)))
        args.append(final_gain)
    return pl.pallas_call(
        functools.partial(_combine_kernel, ne=e_, nb=nb, cap=cap, final=final),
        grid_spec=pltpu.PrefetchScalarGridSpec(
            num_scalar_prefetch=2,
            grid=(b_, t_ // tb),
            in_specs=in_specs,
            out_specs=row,
            scratch_shapes=[pltpu.VMEM((TOKEN_BLOCK, d_), F32)],
        ),
        out_shape=jax.ShapeDtypeStruct((b_, t_, d_), F32),
        compiler_params=_params("arbitrary", "arbitrary", vmem=V7X_VMEM_LIMIT),
        name="moe_combine_final" if final else "moe_combine",
    )(*args)


def _moe(x, h, aff, gate, layer, w_gate, w_up, w_down, final_gain=None):
    slot, tbl, dense = _route(aff)
    xg = _gather(h, slot, tbl, dense)
    y = _experts(xg, layer, w_gate, w_up, w_down)
    return _combine(y, slot, aff, tbl, dense, x, gate, final_gain)


def kernel(x, c, ctx, c_ctx, ada_w, ada_b, norm_mix, norm_ffn, norm_final, hg_w_in, hg_lb_logits, hg_norm, hg_w_out, sc_w_in, sc_conv, sc_w_out, moe_router, moe_w_gate, moe_w_up, moe_w_down):
    b_, t_, d_ = x.shape
    depth = ada_w.shape[0]
    n_ada = ada_w.shape[-1] // d_
    n_heads = d_ // HEAD_DIM
    assert depth == 2 and n_ada == 6 and b_ + 1 <= 8
    assert t_ % GLA_CHUNK == 0 and ctx.shape[1] % GLA_CHUNK == 0 and ROW_TILE % GRID_W == 0

    cond = jnp.concatenate([c, c_ctx[None], jnp.zeros((8 - b_ - 1, d_), F32)], axis=0)
    mod = _ada_vectors(cond, ada_w, ada_b, n_ada)
    vec = lambda i, j: mod[i, j, :b_][:, None, :]
    cvec = lambda i, j: mod[i, j, b_][None, None, :]
    rowtab, coltab, lower = _tables(hg_lb_logits, t_)
    row_of = lambda a, i: a[i][None, :]
    router_t = lambda i: jnp.swapaxes(moe_router[i], 0, 1)

    w_in = hg_w_in[0].astype(BF16)
    lb0 = row_of(lower, 0)
    gain0 = row_of(norm_mix, 0)
    qc, vc, lfc_f, lfc_b = _hgrn_in(ctx, None, gain0, cvec(0, 0), cvec(0, 1), lb0, w_in, with_gate=False)
    zeros = jnp.zeros((b_, n_heads, HEAD_DIM, HEAD_DIM), F32)
    _, _, s_f, s_b = _gla_bidir(qc, vc, lfc_f, lfc_b, zeros, zeros, lb0)
    x0, q, v, lf_f, lf_b, g = _hgrn_in(x, (rowtab, coltab), gain0, vec(0, 0), vec(0, 1), lb0, w_in, with_gate=True)
    o_f, o_b, _, _ = _gla_bidir(q, v, lf_f, lf_b, s_f, s_b, lb0)
    x1, h, aff = _mixer_out(_hgrn_out_kernel, "hgrn_out", [o_f, o_b, g], [], x0, [],
                            [row_of(hg_norm, 0), hg_w_out[0].astype(BF16)],
                            vec(0, 2), row_of(norm_ffn, 0), vec(0, 3), vec(0, 4), router_t(0))
    x2 = _moe(x1, h, aff, vec(0, 5), 0, moe_w_gate, moe_w_up, moe_w_down)

    x3, h, aff = _mixer_out(_conv_mixer_kernel, "conv_mixer", [], [x2, x2], x2, [vec(1, 0), vec(1, 1)],
                            [row_of(norm_mix, 1), sc_w_in[0].astype(BF16), sc_conv[0], sc_w_out[0].astype(BF16)],
                            vec(1, 2), row_of(norm_ffn, 1), vec(1, 3), vec(1, 4), router_t(1))
    return _moe(x3, h, aff, vec(1, 5), 1, moe_w_gate, moe_w_up, moe_w_down, final_gain=norm_final[None, :])
```

```python
import functools
import math

import numpy as np
import jax
import jax.numpy as jnp
from jax import lax
from jax.experimental import pallas as pl
from jax.experimental.pallas import tpu as pltpu

F32 = jnp.float32
BF16 = jnp.bfloat16

EPS = 1e-6
POS_TEMP = 10000.0
GRID_W = 64
HEAD_DIM = 128
EC_CAPACITY_FACTOR = 2
GLA_CHUNK = 256
GLA_LEVELS = 8
GLA_HEADS_PER_STEP = 8
GLA_MERGED = 5
GLA_MAX_EXPONENT = 80.0
ROW_TILE = 512
SUB_ROWS = 256
TOKEN_BLOCK = 256
BLOCKS_PER_STEP = 4
COMBINE_BLOCKS = 2
SLOT_WINDOW = 64
DENSE_WINDOWS = (96, 128)
SLOT_ALIGN = 16
FFN_TILE = 1024
FFN_ROWS = 256
V7X_VMEM_LIMIT = 56 * 1024 * 1024

NT_DIMS = (((1,), (1,)), ((), ()))
TN_DIMS = (((0,), (0,)), ((), ()))


def _params(*sem, vmem=None):
    return pltpu.CompilerParams(dimension_semantics=sem, vmem_limit_bytes=vmem)


def _dot_bf16x3(a, b, dims):
    a0, b0 = a.astype(BF16), b.astype(BF16)
    a1 = (a - a0.astype(F32)).astype(BF16)
    b1 = (b - b0.astype(F32)).astype(BF16)
    d = lambda x, y: lax.dot_general(x, y, dims, preferred_element_type=F32)
    return (d(a0, b1) + d(a1, b0)) + d(a0, b0)


def _sub_tiles(n_rows):
    sub = min(SUB_ROWS, n_rows)
    return [slice(s, s + sub) for s in range(0, n_rows, sub)]


def _sigmoid(x):
    return 1.0 / (1.0 + jnp.exp(-x))


def _silu(x):
    return x * _sigmoid(x)


def _modulate(x, gain, shift, scale):
    y = x * lax.rsqrt(jnp.mean(x * x, axis=-1, keepdims=True) + EPS)
    return y * (gain * (1.0 + scale)) + shift


def _ada_kernel(cond_ref, w_ref, b_ref, out_ref):
    s = _silu(cond_ref[...])
    out_ref[0, 0] = _dot_bf16x3(s, w_ref[0], (((1,), (0,)), ((), ()))) + b_ref[0, 0]


def _ada_vectors(cond, ada_w, ada_b, n_ada):
    depth, d_, _ = ada_w.shape
    return pl.pallas_call(
        _ada_kernel,
        grid=(depth, n_ada),
        in_specs=[pl.BlockSpec((8, d_), lambda i, j: (0, 0)),
                  pl.BlockSpec((1, d_, d_), lambda i, j: (i, 0, j)),
                  pl.BlockSpec((1, 1, 1, d_), lambda i, j: (i, j, 0, 0))],
        out_specs=pl.BlockSpec((1, 1, 8, d_), lambda i, j: (i, j, 0, 0)),
        out_shape=jax.ShapeDtypeStruct((depth, n_ada, 8, d_), F32),
        compiler_params=_params("arbitrary", "arbitrary"),
        name="ada_vectors",
    )(cond, ada_w, ada_b.reshape(depth, n_ada, 1, d_))


def _tables_kernel(lb_logits_ref, rowtab_ref, coltab_ref, lb_ref, *, n_freq):
    def table(n_pos):
        p = lax.broadcasted_iota(jnp.int32, (n_pos, n_freq), 0).astype(F32)
        j = lax.broadcasted_iota(jnp.int32, (n_pos, n_freq), 1).astype(F32)
        omega = jnp.exp(j * (-math.log(POS_TEMP) / n_freq))
        ang = p * omega
        return jnp.concatenate([jnp.sin(ang), jnp.cos(ang)], axis=-1)

    rowtab_ref[...] = table(rowtab_ref.shape[0])
    coltab_ref[...] = table(coltab_ref.shape[0])
    logits = lb_logits_ref[...]
    e = jnp.exp(logits - jnp.max(logits, axis=0, keepdims=True))
    sm = e / jnp.sum(e, axis=0, keepdims=True)
    acc = sm[0:1]
    lb_ref[0:1] = acc
    for i in range(1, lb_ref.shape[0]):
        acc = acc + sm[i:i + 1]
        lb_ref[i:i + 1] = acc


def _tables(lb_logits, n_tokens):
    n_lb, d_ = lb_logits.shape
    n_freq = d_ // 4
    rows = n_tokens // GRID_W
    return pl.pallas_call(
        functools.partial(_tables_kernel, n_freq=n_freq),
        out_shape=[jax.ShapeDtypeStruct((rows, 2 * n_freq), F32),
                   jax.ShapeDtypeStruct((GRID_W, 2 * n_freq), F32),
                   jax.ShapeDtypeStruct((n_lb, d_), F32)],
        name="pos_tables",
    )(lb_logits)


def _hgrn_in_kernel(*refs, with_pos, with_gate):
    it = iter(refs)
    x_ref = next(it)
    if with_pos:
        rowtab_ref, coltab_ref = next(it), next(it)
    gain_ref, shift_ref, scale_ref, lb_ref, w_ref = next(it), next(it), next(it), next(it), next(it)
    if with_pos:
        x0_ref = next(it)
    q_ref, v_ref, lff_ref, lfb_ref = next(it), next(it), next(it), next(it)
    g_ref = next(it) if with_gate else None

    d_ = x_ref.shape[-1]
    lb = lb_ref[...]
    for rows in _sub_tiles(x_ref.shape[1]):
        x = x_ref[0, rows]
        if with_pos:
            grid_rows = range(rows.start // GRID_W, rows.stop // GRID_W)
            pos_row = jnp.concatenate(
                [jnp.broadcast_to(rowtab_ref[0, r:r + 1, :], (GRID_W, rowtab_ref.shape[-1])) for r in grid_rows], axis=0)
            pos_col = jnp.concatenate([coltab_ref[...]] * len(grid_rows), axis=0)
            x = x + jnp.concatenate([pos_row, pos_col], axis=-1)
            x0_ref[0, rows] = x
        h = _modulate(x, gain_ref[...], shift_ref[0], scale_ref[0]).astype(BF16)
        part = lambda p: jnp.dot(h, w_ref[:, p * d_:(p + 1) * d_], preferred_element_type=F32)
        q_ref[0, rows] = (part(0) * HEAD_DIM ** -0.5).astype(BF16)
        v_ref[0, rows] = part(1).astype(BF16)
        lff_ref[0, rows] = jnp.log(lb + (1.0 - lb) * _sigmoid(part(2)))
        lfb_ref[0, rows] = jnp.log(lb + (1.0 - lb) * _sigmoid(part(3)))
        if with_gate:
            g_ref[0, rows] = part(4).astype(BF16)


def _hgrn_in(x, tabs, gain, shift, scale, lb, w_in, *, with_gate):
    b_, t_, d_ = x.shape
    tm = min(ROW_TILE, t_)
    with_pos = tabs is not None
    per_sample = lambda a: pl.BlockSpec((1, 1, d_), (lambda b, i: (b, 0, 0)) if a.shape[0] > 1 else (lambda b, i: (0, 0, 0)))
    row = pl.BlockSpec((1, tm, d_), lambda b, i: (b, i, 0))
    vec = pl.BlockSpec((1, d_), lambda b, i: (0, 0))
    args, in_specs = [x], [row]
    if with_pos:
        rowtab, coltab = tabs
        rows_per_tile = tm // GRID_W
        args += [rowtab.reshape(rowtab.shape[0] // rows_per_tile, rows_per_tile, rowtab.shape[1]), coltab]
        in_specs += [pl.BlockSpec((1, rows_per_tile, rowtab.shape[1]), lambda b, i: (i, 0, 0)),
                     pl.BlockSpec(coltab.shape, lambda b, i: (0, 0))]
    args += [gain, shift, scale, lb, w_in]
    in_specs += [vec, per_sample(shift), per_sample(scale), vec, pl.BlockSpec(w_in.shape, lambda b, i: (0, 0))]
    out_shape, out_specs = [], []
    if with_pos:
        out_shape.append(jax.ShapeDtypeStruct((b_, t_, d_), F32))
        out_specs.append(row)
    out_shape += [jax.ShapeDtypeStruct((b_, t_, d_), BF16)] * 2 + [jax.ShapeDtypeStruct((b_, t_, d_), F32)] * 2
    out_specs += [row] * 4
    if with_gate:
        out_shape.append(jax.ShapeDtypeStruct((b_, t_, d_), BF16))
        out_specs.append(row)
    return pl.pallas_call(
        functools.partial(_hgrn_in_kernel, with_pos=with_pos, with_gate=with_gate),
        grid=(b_, t_ // tm),
        in_specs=in_specs, out_specs=out_specs, out_shape=out_shape,
        compiler_params=_params("arbitrary", "arbitrary", vmem=V7X_VMEM_LIMIT),
        name="hgrn_in_latent" if with_pos else "hgrn_in_context",
    )(*args)


def _gla_consts(reverse):
    c = GLA_CHUNK
    idx = np.arange(c)
    rank = (c - 1 - idx) if reverse else idx
    tri = (rank[None, :] <= rank[:, None]).astype(np.float32)
    hc = c // 2
    hrank = rank[:hc] - rank[:hc].min()
    lvl = np.full((hc, hc), -1, np.int32)
    rt, rs = hrank[:, None], hrank[None, :]
    lvl[rt == rs] = 0
    for level in range(1, GLA_LEVELS):
        blk, half = 1 << level, 1 << (level - 1)
        lvl[(rt // blk == rs // blk) & ((rt % blk) >= half) & ((rs % blk) < half)] = level
    return jnp.asarray(tri, BF16), jnp.asarray(lvl)


def _later_group(level, reverse, group):
    rank = (GLA_CHUNK - 1 - 8 * group) if reverse else 8 * group
    return (rank % (1 << level)) >= (1 << (level - 1))


def _boundary_rows(level, reverse):
    c = GLA_CHUNK
    blk, half = 1 << level, 1 << (level - 1)
    rows = []
    for i in range(c):
        rank = (c - 1 - i) if reverse else i
        brank = (rank // blk) * blk + half - 1
        rows.append((c - 1 - brank) if reverse else brank)
    return rows


def _gla_low_levels(q, kk, lf, cum, lvl, bcast, halves, reverse):
    ng = q.shape[0] // 8
    qb, kb = q.astype(BF16), kk.astype(BF16)
    tiles = [jnp.where(lvl == 0, lax.dot_general(qb[hs], kb[hs], NT_DIMS, preferred_element_type=F32), 0.0)
             for hs in halves]
    row8 = lax.broadcasted_iota(jnp.int32, (8, HEAD_DIM), 0)
    rank8 = (7 - row8) if reverse else row8
    for level in range(1, 4):
        sgn8 = jnp.where(((rank8 >> (level - 1)) & 1) == 1, 1.0, -1.0)
        sgn = jnp.concatenate([sgn8] * ng, axis=0)
        later = sgn > 0.0
        if level == 1:
            g = jnp.where(later, lf, 0.0)
        else:
            brow = _boundary_rows(level, reverse)
            pieces = []
            for grp in range(ng):
                first = bcast(brow[8 * grp])
                pieces.append(jnp.where(row8 < 4, first, bcast(brow[8 * grp + 7])) if level == 2 else first)
            g = (cum - jnp.concatenate(pieces, axis=0)) * sgn
        xe = (jnp.where(later, q, kk) * jnp.exp(g)).astype(BF16)
        for h, hs in enumerate(halves):
            s = lax.dot_general(xe[hs], xe[hs], NT_DIMS, preferred_element_type=F32)
            tiles[h] = jnp.where(lvl == level, s, tiles[h])
    return tiles


def _gla_chunk(q_ref, v_ref, lf_ref, lanes, st_ref, cum_ref, tri_ref, lvl_ref, o_ref, reverse, merged):
    c = GLA_CHUNK
    hc, ng = c // 2, c // 8
    halves = (slice(0, hc), slice(hc, c))
    lf = lf_ref[0, :, lanes]
    q = q_ref[0, :, lanes].astype(F32)
    v = v_ref[0, :, lanes]
    kk = 1.0 - jnp.exp(lf)
    hi = lf.astype(BF16)
    lo = (lf - hi.astype(F32)).astype(BF16)
    two = jnp.dot(tri_ref[...], jnp.concatenate([hi, lo], axis=1), preferred_element_type=F32)
    cum = two[:, HEAD_DIM:] + two[:, :HEAD_DIM]
    cum_ref[...] = cum
    lvl = lvl_ref[...]
    bcast = lambda r: jnp.broadcast_to(cum_ref[r:r + 1, :], (8, HEAD_DIM))
    groups = lambda a: [a[8 * i:8 * i + 8] for i in range(a.shape[0] // 8)]

    if merged:
        blk = 1 << merged
        cache, pieces = {}, []
        for grp in range(ng):
            rank = (c - 1 - 8 * grp) if reverse else 8 * grp
            first = (rank // blk) * blk
            row = (c - 1 - first) if reverse else first
            pieces.append(cache.setdefault(row, bcast(row)))
        inside = jnp.logical_and(lvl >= 0, lvl <= merged)
        tiles = []
        for h, hs in enumerate(halves):
            d = cum[hs] - jnp.concatenate(pieces[h * ng // 2:(h + 1) * ng // 2], axis=0)
            xq = (q[hs] * jnp.exp(d)).astype(BF16)
            xk = (kk[hs] * jnp.exp(-d)).astype(BF16)
            tiles.append(jnp.where(inside, lax.dot_general(xq, xk, NT_DIMS, preferred_element_type=F32), 0.0))
    else:
        tiles = _gla_low_levels(q, kk, lf, cum, lvl, bcast, halves, reverse)

    tile_rows = [groups(t) for t in tiles]
    lvl_rows = groups(lvl)
    q_rows, k_rows, cum_rows = groups(q), groups(kk), groups(cum)
    for h in range(2):
        half_groups = range(h * ng // 2, (h + 1) * ng // 2)
        for level in range(max(4, merged + 1), GLA_LEVELS):
            brow = _boundary_rows(level, reverse)
            cache = {}
            g_rows, x_rows = [], []
            for grp in half_groups:
                cb = cache.setdefault(brow[8 * grp], bcast(brow[8 * grp]))
                later = _later_group(level, reverse, grp)
                g_rows.append(cum_rows[grp] - cb if later else cb - cum_rows[grp])
                x_rows.append(q_rows[grp] if later else k_rows[grp])
            xe = jnp.concatenate(x_rows, axis=0) * jnp.exp(jnp.concatenate(g_rows, axis=0))
            xe_rows = groups(xe)
            later_local = [grp - half_groups[0] for grp in half_groups if _later_group(level, reverse, grp)]
            qc = jnp.concatenate([xe_rows[i] for i in later_local], axis=0).astype(BF16)
            s = lax.dot_general(qc, xe.astype(BF16), NT_DIMS, preferred_element_type=F32)
            for i, local in enumerate(later_local):
                tile_rows[h][local] = jnp.where(lvl_rows[local] == level, s[8 * i:8 * i + 8], tile_rows[h][local])

    early, late = (1, 0) if reverse else (0, 1)
    cb = cum_ref[_boundary_rows(GLA_LEVELS, reverse)[0]:_boundary_rows(GLA_LEVELS, reverse)[0] + 1, :]
    ql = (q[halves[late]] * jnp.exp(cum[halves[late]] - cb)).astype(BF16)
    ke = (kk[halves[early]] * jnp.exp(cb - cum[halves[early]])).astype(BF16)
    cross = lax.dot_general(ql, ke, NT_DIMS, preferred_element_type=F32)
    t_a, t_b = (jnp.concatenate(rows, axis=0) for rows in tile_rows)
    zero = jnp.zeros((hc, hc), F32)
    if reverse:
        scores = jnp.concatenate([jnp.concatenate([t_a, cross], axis=1), jnp.concatenate([zero, t_b], axis=1)], axis=0)
    else:
        scores = jnp.concatenate([jnp.concatenate([t_a, zero], axis=1), jnp.concatenate([cross, t_b], axis=1)], axis=0)

    st = st_ref[...]
    qe = (q * jnp.exp(cum)).astype(BF16)
    o = jnp.dot(scores.astype(BF16), v, preferred_element_type=F32)
    o = o + lax.dot_general(qe, st.astype(BF16), NT_DIMS, preferred_element_type=F32)
    o_ref[0, :, lanes] = o.astype(o_ref.dtype)
    last_row = 0 if reverse else c - 1
    last = cum_ref[last_row:last_row + 1, :]
    ke_all = (kk * jnp.exp(last - cum)).astype(BF16)
    st_ref[...] = st * jnp.exp(last) + lax.dot_general(v, ke_all, TN_DIMS, preferred_element_type=F32)


def _gla_kernel(qf_ref, vf_ref, lff_ref, qb_ref, vb_ref, lfb_ref, s0f_ref, s0b_ref,
                trif_ref, lvlf_ref, trib_ref, lvlb_ref,
                of_ref, ob_ref, sff_ref, sfb_ref, stf_ref, stb_ref, cumf_ref, cumb_ref, *, merged):
    j = pl.program_id(2)

    @pl.when(j == 0)
    def _():
        stf_ref[...] = s0f_ref[0]
        stb_ref[...] = s0b_ref[0]

    for k in range(stf_ref.shape[0]):
        lanes = slice(k * HEAD_DIM, (k + 1) * HEAD_DIM)
        _gla_chunk(qf_ref, vf_ref, lff_ref, lanes, stf_ref.at[k], cumf_ref.at[k], trif_ref, lvlf_ref, of_ref,
                   False, merged)
        _gla_chunk(qb_ref, vb_ref, lfb_ref, lanes, stb_ref.at[k], cumb_ref.at[k], trib_ref, lvlb_ref, ob_ref,
                   True, merged)

    @pl.when(j == pl.num_programs(2) - 1)
    def _():
        sff_ref[0] = stf_ref[...]
        sfb_ref[0] = stb_ref[...]


def _gla_bidir(q, v, lf_f, lf_b, s0f, s0b, lb):
    worst = (2 ** GLA_MERGED - 1) * jnp.max(-jnp.log(lb))
    run = lambda merged: (lambda *a: _gla_call(*a, merged=merged))
    return lax.cond(worst < GLA_MAX_EXPONENT, run(GLA_MERGED), run(0), q, v, lf_f, lf_b, s0f, s0b)


def _gla_call(q, v, lf_f, lf_b, s0f, s0b, *, merged):
    b_, t_, d_ = q.shape
    h_ = d_ // HEAD_DIM
    hp = math.gcd(GLA_HEADS_PER_STEP, h_)
    c = GLA_CHUNK
    n = t_ // c
    fwd = lambda b, h, j: (b, j, h)
    bwd = lambda b, h, j: (b, n - 1 - j, h)
    st = lambda b, h, j: (b, h, 0, 0)
    const = lambda b, h, j: (0, 0)
    blk = lambda im: pl.BlockSpec((1, c, hp * HEAD_DIM), im)
    st_spec = pl.BlockSpec((1, hp, HEAD_DIM, HEAD_DIM), st)
    cspecs = [pl.BlockSpec((c, c), const), pl.BlockSpec((c // 2, c // 2), const)]
    state = pltpu.VMEM((hp, HEAD_DIM, HEAD_DIM), F32)
    cum = pltpu.VMEM((hp, c, HEAD_DIM), F32)
    return pl.pallas_call(
        functools.partial(_gla_kernel, merged=merged),
        grid=(b_, h_ // hp, n),
        in_specs=[blk(fwd), blk(fwd), blk(fwd), blk(bwd), blk(bwd), blk(bwd), st_spec, st_spec] + cspecs + cspecs,
        out_specs=[blk(fwd), blk(bwd), st_spec, st_spec],
        out_shape=[jax.ShapeDtypeStruct((b_, t_, d_), BF16)] * 2
                  + [jax.ShapeDtypeStruct((b_, h_, HEAD_DIM, HEAD_DIM), F32)] * 2,
        scratch_shapes=[state, state, cum, cum],
        compiler_params=_params("arbitrary", "arbitrary", "arbitrary"),
        name="gla_merged" if merged else "gla_split",
    )(q, v, lf_f, q, v, lf_b, s0f, s0b, *_gla_consts(False), *_gla_consts(True))


def _mixer_epilogue(y, rows, x_ref, w_ref, gate_ref, gain_ref, shift_ref, scale_ref, wr_ref, x1_ref, h_ref, aff_ref):
    y = jnp.dot(y.astype(BF16), w_ref[...], preferred_element_type=F32)
    x1 = x_ref[0, rows] + gate_ref[0] * y
    x1_ref[0, rows] = x1
    hf = _modulate(x1, gain_ref[...], shift_ref[0], scale_ref[0])
    h_ref[0, rows] = hf.astype(BF16)
    logits = _dot_bf16x3(wr_ref[...], hf, NT_DIMS)
    e = jnp.exp(logits - jnp.max(logits, axis=0, keepdims=True))
    aff_ref[0, :, rows] = e / jnp.sum(e, axis=0, keepdims=True)


def _hgrn_out_kernel(of_ref, ob_ref, g_ref, x_ref, hnorm_ref, w_ref, *rest):
    hn = hnorm_ref[...]
    for rows in _sub_tiles(x_ref.shape[1]):
        o = of_ref[0, rows].astype(F32) + ob_ref[0, rows].astype(F32)
        heads = []
        for h in range(o.shape[-1] // HEAD_DIM):
            oh = o[:, h * HEAD_DIM:(h + 1) * HEAD_DIM]
            heads.append(oh * lax.rsqrt(jnp.mean(oh * oh, axis=-1, keepdims=True) + EPS) * hn)
        y = jnp.concatenate(heads, axis=-1) * _silu(g_ref[0, rows].astype(F32))
        _mixer_epilogue(y, rows, x_ref, w_ref, *rest)


def _conv_mixer_kernel(prev_ref, next_ref, x_ref, shift1_ref, scale1_ref, gain1_ref, win_ref, wc_ref, w_ref, *rest):
    i = pl.program_id(1)
    tm, d_ = x_ref.shape[1], x_ref.shape[2]
    subs = _sub_tiles(tm)
    halo = jnp.concatenate([prev_ref[0], next_ref[0]], axis=0)
    b_gate, cu = [], []
    for k, rows in enumerate(subs):
        xs = x_ref[0, rows]
        if k == 0:
            xs = jnp.concatenate([xs, halo], axis=0)
        h = _modulate(xs, gain1_ref[...], shift1_ref[0], scale1_ref[0]).astype(BF16)
        part = lambda p, hh: jnp.dot(hh, win_ref[:, p * d_:(p + 1) * d_], preferred_element_type=F32)
        n = rows.stop - rows.start
        b_gate.append(part(0, h[:n]))
        cu.append(part(1, h) * part(2, h))
    cu_halo = cu[0][subs[0].stop - subs[0].start:]
    cu = jnp.concatenate([cu[0][:subs[0].stop - subs[0].start]] + cu[1:], axis=0)
    rid = lax.broadcasted_iota(jnp.int32, cu.shape, 0)
    before = jnp.where(i == 0, 0.0, cu_halo[7:8])
    after = jnp.where(i == pl.num_programs(1) - 1, 0.0, cu_halo[8:9])
    left = jnp.where(rid == 0, before, pltpu.roll(cu, 1, 0))
    right = jnp.where(rid == tm - 1, after, pltpu.roll(cu, tm - 1, 0))
    wc = wc_ref[...]
    conv = left * wc[0:1] + cu * wc[1:2] + right * wc[2:3]
    for k, rows in enumerate(subs):
        _mixer_epilogue(b_gate[k] * conv[rows], rows, x_ref, w_ref, *rest)


def _mixer_out(kernel, name, row_args, halo_args, x, sample_args, small_args, gate, gain, shift, scale, w_router_t):
    b_, t_, d_ = x.shape
    e_ = w_router_t.shape[0]
    tm = min(ROW_TILE, t_)
    row = pl.BlockSpec((1, tm, d_), lambda b, i: (b, i, 0))
    per_sample = pl.BlockSpec((1, 1, d_), lambda b, i: (b, 0, 0))
    whole = lambda a: pl.BlockSpec(a.shape, lambda b, i: (0,) * a.ndim)
    n8 = t_ // 8
    halo_specs = [pl.BlockSpec((1, 8, d_), lambda b, i: (b, jnp.maximum(i * (tm // 8) - 1, 0), 0)),
                  pl.BlockSpec((1, 8, d_), lambda b, i: (b, jnp.minimum((i + 1) * (tm // 8), n8 - 1), 0))]
    return pl.pallas_call(
        kernel,
        grid=(b_, t_ // tm),
        in_specs=[row] * len(row_args) + halo_specs[:len(halo_args)] + [row] + [per_sample] * len(sample_args)
                 + [whole(a) for a in small_args] + [per_sample, whole(gain), per_sample, per_sample, whole(w_router_t)],
        out_specs=[row, row, pl.BlockSpec((1, e_, tm), lambda b, i: (b, 0, i))],
        out_shape=[jax.ShapeDtypeStruct((b_, t_, d_), F32), jax.ShapeDtypeStruct((b_, t_, d_), BF16),
                   jax.ShapeDtypeStruct((b_, e_, t_), F32)],
        compiler_params=_params("arbitrary", "arbitrary", vmem=V7X_VMEM_LIMIT),
        name=name,
    )(*row_args, *halo_args, x, *sample_args, *small_args, gate, gain, shift, scale, w_router_t)


def _route_kernel(aff_ref, tri_ref, blockind_ref, slot_ref, base_ref, dense_ref, *, cap):
    aff = aff_ref[0]
    e_, t_ = aff.shape

    def as_float(word):
        return pltpu.bitcast(word, F32)

    def count_ge(th):
        return jnp.sum(jnp.where(aff >= th, 1.0, 0.0), axis=1, keepdims=True)

    def search(_, carry):
        lo, hi = carry
        mid = lo + ((hi - lo + 1) >> 1)
        ok = count_ge(as_float(mid)) >= cap
        return jnp.where(ok, mid, lo), jnp.where(ok, hi, mid - 1)

    lo0 = jnp.zeros((e_, 1), jnp.int32)
    hi0 = jnp.full((e_, 1), 0x7F7FFFFF, jnp.int32)
    kth, _ = lax.fori_loop(0, 32, search, (lo0, hi0))
    above = aff >= as_float(kth + 1)
    tied = jnp.logical_and(aff >= as_float(kth), jnp.logical_not(above))
    need = cap - jnp.sum(jnp.where(above, 1.0, 0.0), axis=1, keepdims=True)
    tri = tri_ref[...]
    tb = tri.shape[0]
    carry_t = jnp.zeros((e_, 1), F32)
    carry_s = jnp.zeros((e_, 1), F32)
    sel_blocks = []
    for j in range(t_ // tb):
        cols = slice(j * tb, (j + 1) * tb)
        tied_j = tied[:, cols]
        ct = jnp.dot(jnp.where(tied_j, 1.0, 0.0).astype(BF16), tri, preferred_element_type=F32) + carry_t
        carry_t = ct[:, tb - 1:tb]
        sel_j = jnp.where(above[:, cols], 1.0, jnp.where(tied_j & (ct <= need), 1.0, 0.0))
        cs = jnp.dot(sel_j.astype(BF16), tri, preferred_element_type=F32) + carry_s
        carry_s = cs[:, tb - 1:tb]
        slot_ref[0, :, cols] = jnp.where(sel_j > 0.0, cs - 1.0, -1.0).astype(jnp.int32)
        sel_blocks.append(sel_j.astype(BF16))
    sel = jnp.concatenate(sel_blocks, axis=1)
    counts = jnp.dot(sel, blockind_ref[...], preferred_element_type=F32)
    base, end = counts[:, :128], counts[:, 128:]
    base_ref[0] = base.astype(jnp.int32)
    span = end - jnp.floor(base * (1.0 / SLOT_ALIGN)) * SLOT_ALIGN
    widest = jnp.max(span, axis=0, keepdims=True)
    dense_ref[0] = sum(jnp.where(widest <= w, 1, 0) for w in DENSE_WINDOWS).astype(jnp.int32)


def _route(aff):
    b_, e_, t_ = aff.shape
    cap = EC_CAPACITY_FACTOR * t_ // e_
    tb = TOKEN_BLOCK
    nb = t_ // tb
    tri = jnp.asarray(np.triu(np.ones((tb, tb), np.float32)), BF16)
    tok, col = np.arange(t_)[:, None], np.arange(128)[None, :]
    blockind = np.concatenate([(tok < col * tb) & (col <= nb), (tok < (col + 1) * tb) & (col < nb)], axis=1)
    slot, base, dense = pl.pallas_call(
        functools.partial(_route_kernel, cap=cap),
        grid=(b_,),
        in_specs=[pl.BlockSpec((1, e_, t_), lambda b: (b, 0, 0)),
                  pl.BlockSpec((tb, tb), lambda b: (0, 0)),
                  pl.BlockSpec((t_, 256), lambda b: (0, 0))],
        out_specs=[pl.BlockSpec((1, e_, t_), lambda b: (b, 0, 0)), pl.BlockSpec((1, e_, 128), lambda b: (b, 0, 0)),
                   pl.BlockSpec((1, 1, 128), lambda b: (b, 0, 0))],
        out_shape=[jax.ShapeDtypeStruct((b_, e_, t_), jnp.int32), jax.ShapeDtypeStruct((b_, e_, 128), jnp.int32),
                   jax.ShapeDtypeStruct((b_, 1, 128), jnp.int32)],
        compiler_params=_params("arbitrary"),
        name="route",
    )(aff, tri, jnp.asarray(blockind.astype(np.float32), BF16))
    return slot, base[:, :, :nb + 1].reshape(-1), dense[:, 0, :nb].reshape(-1)


def _window_plan(tbl_ref, b, e, tb, ne, nb):
    idx = (b * ne + e) * (nb + 1) + tb
    base, end = tbl_ref[idx], tbl_ref[idx + 1]
    start = (base >> 4) << 4
    n_win = jnp.where(end > base, (end - start + SLOT_WINDOW - 1) >> 6, 0)
    return start, n_win


def _dense_start(tbl_ref, b, e, tb, ne, nb, cap, window):
    base = tbl_ref[(b * ne + e) * (nb + 1) + tb]
    return pl.multiple_of(jnp.minimum((base >> 4) << 4, cap - window), SLOT_ALIGN)


def _gather_kernel(tbl_ref, dense_ref, h_ref, slot_ref, xg_ref, *, ne, nb, cap):
    b, step = pl.program_id(0), pl.program_id(2)
    n_tok = TOKEN_BLOCK

    @pl.when(step == 0)
    def _():
        xg_ref[...] = jnp.zeros(xg_ref.shape, xg_ref.dtype)

    def dense_block(sub, window):
        tb = step * BLOCKS_PER_STEP + sub
        toks = slice(sub * n_tok, (sub + 1) * n_tok)
        starts = [_dense_start(tbl_ref, b, e, tb, ne, nb, cap, window) for e in range(ne)]
        ids = lax.broadcasted_iota(jnp.int32, (window, n_tok), 0)
        onehot = jnp.concatenate(
            [jnp.where(ids == slot_ref[0, e:e + 1, toks] - starts[e], 1.0, 0.0).astype(BF16) for e in range(ne)],
            axis=0)
        rows = jnp.dot(onehot, h_ref[0, toks, :], preferred_element_type=F32).astype(BF16)
        for e in range(ne):
            win = xg_ref.at[0, e, pl.ds(starts[e], window), :]
            win[...] = win[...] + rows[e * window:(e + 1) * window]

    def windowed_block(sub):
        tb = step * BLOCKS_PER_STEP + sub
        toks = slice(sub * n_tok, (sub + 1) * n_tok)
        h = h_ref[0, toks, :]
        ids0 = lax.broadcasted_iota(jnp.int32, (SLOT_WINDOW, n_tok), 0)
        for e in range(ne):
            srow = slot_ref[0, e:e + 1, toks]
            start, n_win = _window_plan(tbl_ref, b, e, tb, ne, nb)

            def body(k, carry, e=e, srow=srow, start=start):
                lo = start + k * SLOT_WINDOW
                w0 = pl.multiple_of(jnp.minimum(lo, cap - SLOT_WINDOW), SLOT_ALIGN)
                ids = w0 + ids0
                onehot = jnp.where(ids >= lo, jnp.where(ids == srow, 1.0, 0.0), 0.0).astype(BF16)
                rows = jnp.dot(onehot, h, preferred_element_type=F32)
                win = xg_ref.at[0, e, pl.ds(w0, SLOT_WINDOW), :]
                win[...] = win[...] + rows.astype(BF16)
                return carry

            lax.fori_loop(0, n_win, body, 0)

    _per_block_dispatch(dense_ref, b * nb + step * BLOCKS_PER_STEP, BLOCKS_PER_STEP, dense_block, windowed_block)


def _per_block_dispatch(dense_ref, first, n_blocks, dense_block, windowed_block):
    fits = [dense_ref[first + sub] for sub in range(n_blocks)]
    common = functools.reduce(jnp.minimum, fits)
    n_tiers = len(DENSE_WINDOWS)
    for k, window in enumerate(DENSE_WINDOWS):
        chosen = (common >= n_tiers) if k == 0 else (common == n_tiers - k)

        @pl.when(chosen)
        def _(window=window):
            for sub in range(n_blocks):
                dense_block(sub, window)

    @pl.when(common == 0)
    def _():
        for sub in range(n_blocks):
            pl.when(fits[sub] > 0)(functools.partial(dense_block, sub, DENSE_WINDOWS[-1]))
            pl.when(fits[sub] == 0)(functools.partial(windowed_block, sub))


def _gather(h, slot, tbl, dense):
    b_, t_, d_ = h.shape
    e_ = slot.shape[1]
    cap = EC_CAPACITY_FACTOR * t_ // e_
    tb = TOKEN_BLOCK * BLOCKS_PER_STEP
    nb = t_ // TOKEN_BLOCK
    dh = d_ // 2
    return pl.pallas_call(
        functools.partial(_gather_kernel, ne=e_, nb=nb, cap=cap),
        grid_spec=pltpu.PrefetchScalarGridSpec(
            num_scalar_prefetch=2,
            grid=(b_, 2, t_ // tb),
            in_specs=[pl.BlockSpec((1, tb, dh), lambda b, c, i, *_: (b, i, c)),
                      pl.BlockSpec((1, e_, tb), lambda b, c, i, *_: (b, 0, i))],
            out_specs=pl.BlockSpec((1, e_, cap, dh), lambda b, c, i, *_: (b, 0, 0, c)),
        ),
        out_shape=jax.ShapeDtypeStruct((b_, e_, cap, d_), BF16),
        compiler_params=_params("arbitrary", "arbitrary", "arbitrary", vmem=V7X_VMEM_LIMIT),
        name="moe_gather",
    )(tbl, dense, h, slot)


def _expert_kernel(xg_ref, wg_ref, wu_ref, wd_ref, y_ref, acc_ref):
    f = pl.program_id(1)
    n_b, _, cap, _ = xg_ref.shape
    wg = wg_ref[0, 0].astype(BF16)
    wu = wu_ref[0, 0].astype(BF16)
    wd = wd_ref[0, 0].astype(BF16)

    @pl.when(jnp.logical_and(pl.program_id(0) == 0, f == 0))
    def _():
        acc_ref[...] = jnp.zeros(acc_ref.shape, F32)

    for b in range(n_b):
        for r in range(cap // FFN_ROWS):
            rows = pl.ds(r * FFN_ROWS, FFN_ROWS)
            acc_rows = pl.ds((b * cap) + r * FFN_ROWS, FFN_ROWS)
            xr = xg_ref[b, 0, rows, :]
            a = jnp.dot(xr, wg, preferred_element_type=F32)
            u = jnp.dot(xr, wu, preferred_element_type=F32)
            part = jnp.dot((_silu(a) * u).astype(BF16), wd, preferred_element_type=F32)
            total = jnp.where(f == 0, 0.0, acc_ref[acc_rows, :]) + part
            acc_ref[acc_rows, :] = total
            y_ref[b, 0, rows, :] = total.astype(BF16)


def _experts(xg, layer, w_gate, w_up, w_down):
    b_, e_, cap, d_ = xg.shape
    f_ = w_gate.shape[-1]
    ft = min(FFN_TILE, f_)
    return pl.pallas_call(
        _expert_kernel,
        grid=(e_, f_ // ft),
        in_specs=[pl.BlockSpec((b_, 1, cap, d_), lambda e, f: (0, e, 0, 0)),
                  pl.BlockSpec((1, 1, d_, ft), lambda e, f: (layer, e, 0, f)),
                  pl.BlockSpec((1, 1, d_, ft), lambda e, f: (layer, e, 0, f)),
                  pl.BlockSpec((1, 1, ft, d_), lambda e, f: (layer, e, f, 0))],
        out_specs=pl.BlockSpec((b_, 1, cap, d_), lambda e, f: (0, e, 0, 0)),
        out_shape=jax.ShapeDtypeStruct((b_, e_, cap, d_), BF16),
        scratch_shapes=[pltpu.VMEM((b_ * cap, d_), F32)],
        compiler_params=_params("arbitrary", "arbitrary", vmem=V7X_VMEM_LIMIT),
        name="moe_experts",
    )(xg, w_gate, w_up, w_down)


def _combine_kernel(tbl_ref, dense_ref, y_ref, slot_ref, aff_ref, x_ref, gate_ref, *rest, ne, nb, cap, final):
    if final:
        gain_ref, out_ref, acc_ref = rest
    else:
        out_ref, acc_ref = rest
    b, step = pl.program_id(0), pl.program_id(1)
    n_tok = TOKEN_BLOCK

    def finish(toks, moe):
        out = x_ref[0, toks, :] + gate_ref[0] * moe
        if final:
            out = out * lax.rsqrt(jnp.mean(out * out, axis=-1, keepdims=True) + EPS) * gain_ref[...]
        out_ref[0, toks, :] = out

    def dense_block(sub, window):
        tb = step * COMBINE_BLOCKS + sub
        toks = slice(sub * n_tok, (sub + 1) * n_tok)
        starts = [_dense_start(tbl_ref, b, e, tb, ne, nb, cap, window) for e in range(ne)]
        ids = lax.broadcasted_iota(jnp.int32, (window, n_tok), 0)
        weights = jnp.concatenate(
            [jnp.where(ids == slot_ref[0, e:e + 1, toks] - starts[e], aff_ref[0, e:e + 1, toks], 0.0).astype(BF16)
             for e in range(ne)], axis=0)
        yw = jnp.concatenate([y_ref[0, e, pl.ds(starts[e], window), :] for e in range(ne)], axis=0)
        finish(toks, lax.dot_general(weights, yw, TN_DIMS, preferred_element_type=F32))

    def windowed_block(sub):
        tb = step * COMBINE_BLOCKS + sub
        toks = slice(sub * n_tok, (sub + 1) * n_tok)
        ids0 = lax.broadcasted_iota(jnp.int32, (SLOT_WINDOW, n_tok), 0)
        acc_ref[...] = jnp.zeros(acc_ref.shape, F32)
        for e in range(ne):
            srow = slot_ref[0, e:e + 1, toks]
            grow = aff_ref[0, e:e + 1, toks]
            start, n_win = _window_plan(tbl_ref, b, e, tb, ne, nb)

            def body(k, carry, e=e, srow=srow, grow=grow, start=start):
                lo = start + k * SLOT_WINDOW
                w0 = pl.multiple_of(jnp.minimum(lo, cap - SLOT_WINDOW), SLOT_ALIGN)
                ids = w0 + ids0
                weights = jnp.where(ids >= lo, jnp.where(ids == srow, grow, 0.0), 0.0).astype(BF16)
                yw = y_ref[0, e, pl.ds(w0, SLOT_WINDOW), :]
                acc_ref[...] = acc_ref[...] + lax.dot_general(weights, yw, TN_DIMS, preferred_element_type=F32)
                return carry

            lax.fori_loop(0, n_win, body, 0)
        finish(toks, acc_ref[...])

    _per_block_dispatch(dense_ref, b * nb + step * COMBINE_BLOCKS, COMBINE_BLOCKS, dense_block, windowed_block)


def _combine(y, slot, aff, tbl, dense, x, gate, final_gain):
    b_, t_, d_ = x.shape
    e_, cap = y.shape[1], y.shape[2]
    tb = TOKEN_BLOCK * COMBINE_BLOCKS
    nb = t_ // TOKEN_BLOCK
    final = final_gain is not None
    route_spec = pl.BlockSpec((1, e_, tb), lambda b, i, *_: (b, 0, i))
    row = pl.BlockSpec((1, tb, d_), lambda b, i, *_: (b, i, 0))
    in_specs = [pl.BlockSpec((1, e_, cap, d_), lambda b, i, *_: (b, 0, 0, 0), pipeline_mode=pl.Buffered(1)),
                route_spec, route_spec, row, pl.BlockSpec((1, 1, d_), lambda b, i, *_: (b, 0, 0))]
    args = [tbl, dense, y, slot, aff, x, gate]
    if final:
        in_specs.append(pl.BlockSpec((1, d_), lambda b, i, *_: (0, 0)))
        args.append(final_gain)
    return pl.pallas_call(
        functools.partial(_combine_kernel, ne=e_, nb=nb, cap=cap, final=final),
        grid_spec=pltpu.PrefetchScalarGridSpec(
            num_scalar_prefetch=2,
            grid=(b_, t_ // tb),
            in_specs=in_specs,
            out_specs=row,
            scratch_shapes=[pltpu.VMEM((TOKEN_BLOCK, d_), F32)],
        ),
        out_shape=jax.ShapeDtypeStruct((b_, t_, d_), F32),
        compiler_params=_params("arbitrary", "arbitrary", vmem=V7X_VMEM_LIMIT),
        name="moe_combine_final" if final else "moe_combine",
    )(*args)


def _moe(x, h, aff, gate, layer, w_gate, w_up, w_down, final_gain=None):
    slot, tbl, dense = _route(aff)
    xg = _gather(h, slot, tbl, dense)
    y = _experts(xg, layer, w_gate, w_up, w_down)
    return _combine(y, slot, aff, tbl, dense, x, gate, final_gain)


def kernel(x, c, ctx, c_ctx, ada_w, ada_b, norm_mix, norm_ffn, norm_final, hg_w_in, hg_lb_logits, hg_norm, hg_w_out, sc_w_in, sc_conv, sc_w_out, moe_router, moe_w_gate, moe_w_up, moe_w_down):
    b_, t_, d_ = x.shape
    depth = ada_w.shape[0]
    n_ada = ada_w.shape[-1] // d_
    n_heads = d_ // HEAD_DIM
    assert depth == 2 and n_ada == 6 and b_ + 1 <= 8
    assert t_ % GLA_CHUNK == 0 and ctx.shape[1] % GLA_CHUNK == 0 and ROW_TILE % GRID_W == 0

    cond = jnp.concatenate([c, c_ctx[None], jnp.zeros((8 - b_ - 1, d_), F32)], axis=0)
    mod = _ada_vectors(cond, ada_w, ada_b, n_ada)
    vec = lambda i, j: mod[i, j, :b_][:, None, :]
    cvec = lambda i, j: mod[i, j, b_][None, None, :]
    rowtab, coltab, lower = _tables(hg_lb_logits, t_)
    row_of = lambda a, i: a[i][None, :]
    router_t = lambda i: jnp.swapaxes(moe_router[i], 0, 1)

    w_in = hg_w_in[0].astype(BF16)
    lb0 = row_of(lower, 0)
    gain0 = row_of(norm_mix, 0)
    qc, vc, lfc_f, lfc_b = _hgrn_in(ctx, None, gain0, cvec(0, 0), cvec(0, 1), lb0, w_in, with_gate=False)
    zeros = jnp.zeros((b_, n_heads, HEAD_DIM, HEAD_DIM), F32)
    _, _, s_f, s_b = _gla_bidir(qc, vc, lfc_f, lfc_b, zeros, zeros, lb0)
    x0, q, v, lf_f, lf_b, g = _hgrn_in(x, (rowtab, coltab), gain0, vec(0, 0), vec(0, 1), lb0, w_in, with_gate=True)
    o_f, o_b, _, _ = _gla_bidir(q, v, lf_f, lf_b, s_f, s_b, lb0)
    x1, h, aff = _mixer_out(_hgrn_out_kernel, "hgrn_out", [o_f, o_b, g], [], x0, [],
                            [row_of(hg_norm, 0), hg_w_out[0].astype(BF16)],
                            vec(0, 2), row_of(norm_ffn, 0), vec(0, 3), vec(0, 4), router_t(0))
    x2 = _moe(x1, h, aff, vec(0, 5), 0, moe_w_gate, moe_w_up, moe_w_down)

    x3, h, aff = _mixer_out(_conv_mixer_kernel, "conv_mixer", [], [x2, x2], x2, [vec(1, 0), vec(1, 1)],
                            [row_of(norm_mix, 1), sc_w_in[0].astype(BF16), sc_conv[0], sc_w_out[0].astype(BF16)],
                            vec(1, 2), row_of(norm_ffn, 1), vec(1, 3), vec(1, 4), router_t(1))
    return _moe(x3, h, aff, vec(1, 5), 1, moe_w_gate, moe_w_up, moe_w_down, final_gain=norm_final[None, :])
```

```python
import functools
import math

import numpy as np
import jax
import jax.numpy as jnp
from jax import lax
from jax.experimental import pallas as pl
from jax.experimental.pallas import tpu as pltpu

F32 = jnp.float32
BF16 = jnp.bfloat16

EPS = 1e-6
POS_TEMP = 10000.0
GRID_W = 64
HEAD_DIM = 128
EC_CAPACITY_FACTOR = 2
GLA_CHUNK = 256
GLA_LEVELS = 8
GLA_HEADS_PER_STEP = 8
GLA_MERGED = 5
GLA_MAX_EXPONENT = 80.0
ROW_TILE = 512
SUB_ROWS = 256
TOKEN_BLOCK = 256
BLOCKS_PER_STEP = 4
COMBINE_BLOCKS = 2
SLOT_WINDOW = 64
DENSE_WINDOWS = (96, 128)
SLOT_ALIGN = 16
FFN_TILE = 1024
FFN_ROWS = 256
V7X_VMEM_LIMIT = 56 * 1024 * 1024

NT_DIMS = (((1,), (1,)), ((), ()))
TN_DIMS = (((0,), (0,)), ((), ()))


def _params(*sem, vmem=None):
    return pltpu.CompilerParams(dimension_semantics=sem, vmem_limit_bytes=vmem)


def _dot_bf16x3(a, b, dims):
    a0, b0 = a.astype(BF16), b.astype(BF16)
    a1 = (a - a0.astype(F32)).astype(BF16)
    b1 = (b - b0.astype(F32)).astype(BF16)
    d = lambda x, y: lax.dot_general(x, y, dims, preferred_element_type=F32)
    return (d(a0, b1) + d(a1, b0)) + d(a0, b0)


def _sub_tiles(n_rows):
    sub = min(SUB_ROWS, n_rows)
    return [slice(s, s + sub) for s in range(0, n_rows, sub)]


def _sigmoid(x):
    return 1.0 / (1.0 + jnp.exp(-x))


def _silu(x):
    return x * _sigmoid(x)


def _modulate(x, gain, shift, scale):
    y = x * lax.rsqrt(jnp.mean(x * x, axis=-1, keepdims=True) + EPS)
    return y * (gain * (1.0 + scale)) + shift


def _ada_kernel(cond_ref, w_ref, b_ref, out_ref):
    s = _silu(cond_ref[...])
    out_ref[0, 0] = _dot_bf16x3(s, w_ref[0], (((1,), (0,)), ((), ()))) + b_ref[0, 0]


def _ada_vectors(cond, ada_w, ada_b, n_ada):
    depth, d_, _ = ada_w.shape
    return pl.pallas_call(
        _ada_kernel,
        grid=(depth, n_ada),
        in_specs=[pl.BlockSpec((8, d_), lambda i, j: (0, 0)),
                  pl.BlockSpec((1, d_, d_), lambda i, j: (i, 0, j)),
                  pl.BlockSpec((1, 1, 1, d_), lambda i, j: (i, j, 0, 0))],
        out_specs=pl.BlockSpec((1, 1, 8, d_), lambda i, j: (i, j, 0, 0)),
        out_shape=jax.ShapeDtypeStruct((depth, n_ada, 8, d_), F32),
        compiler_params=_params("arbitrary", "arbitrary"),
        name="ada_vectors",
    )(cond, ada_w, ada_b.reshape(depth, n_ada, 1, d_))


def _tables_kernel(lb_logits_ref, rowtab_ref, coltab_ref, lb_ref, *, n_freq):
    def table(n_pos):
        p = lax.broadcasted_iota(jnp.int32, (n_pos, n_freq), 0).astype(F32)
        j = lax.broadcasted_iota(jnp.int32, (n_pos, n_freq), 1).astype(F32)
        omega = jnp.exp(j * (-math.log(POS_TEMP) / n_freq))
        ang = p * omega
        return jnp.concatenate([jnp.sin(ang), jnp.cos(ang)], axis=-1)

    rowtab_ref[...] = table(rowtab_ref.shape[0])
    coltab_ref[...] = table(coltab_ref.shape[0])
    logits = lb_logits_ref[...]
    e = jnp.exp(logits - jnp.max(logits, axis=0, keepdims=True))
    sm = e / jnp.sum(e, axis=0, keepdims=True)
    acc = sm[0:1]
    lb_ref[0:1] = acc
    for i in range(1, lb_ref.shape[0]):
        acc = acc + sm[i:i + 1]
        lb_ref[i:i + 1] = acc


def _tables(lb_logits, n_tokens):
    n_lb, d_ = lb_logits.shape
    n_freq = d_ // 4
    rows = n_tokens // GRID_W
    return pl.pallas_call(
        functools.partial(_tables_kernel, n_freq=n_freq),
        out_shape=[jax.ShapeDtypeStruct((rows, 2 * n_freq), F32),
                   jax.ShapeDtypeStruct((GRID_W, 2 * n_freq), F32),
                   jax.ShapeDtypeStruct((n_lb, d_), F32)],
        name="pos_tables",
    )(lb_logits)


def _hgrn_in_kernel(*refs, with_pos, with_gate):
    it = iter(refs)
    x_ref = next(it)
    if with_pos:
        rowtab_ref, coltab_ref = next(it), next(it)
    gain_ref, shift_ref, scale_ref, lb_ref, w_ref = next(it), next(it), next(it), next(it), next(it)
    if with_pos:
        x0_ref = next(it)
    q_ref, v_ref, lff_ref, lfb_ref = next(it), next(it), next(it), next(it)
    g_ref = next(it) if with_gate else None

    d_ = x_ref.shape[-1]
    lb = lb_ref[...]
    for rows in _sub_tiles(x_ref.shape[1]):
        x = x_ref[0, rows]
        if with_pos:
            grid_rows = range(rows.start // GRID_W, rows.stop // GRID_W)
            pos_row = jnp.concatenate(
                [jnp.broadcast_to(rowtab_ref[0, r:r + 1, :], (GRID_W, rowtab_ref.shape[-1])) for r in grid_rows], axis=0)
            pos_col = jnp.concatenate([coltab_ref[...]] * len(grid_rows), axis=0)
            x = x + jnp.concatenate([pos_row, pos_col], axis=-1)
            x0_ref[0, rows] = x
        h = _modulate(x, gain_ref[...], shift_ref[0], scale_ref[0]).astype(BF16)
        part = lambda p: jnp.dot(h, w_ref[:, p * d_:(p + 1) * d_], preferred_element_type=F32)
        lff_ref[0, rows] = jnp.log(lb + (1.0 - lb) * _sigmoid(part(2)))
        lfb_ref[0, rows] = jnp.log(lb + (1.0 - lb) * _sigmoid(part(3)))
        q_ref[0, rows] = (part(0) * HEAD_DIM ** -0.5).astype(BF16)
        v_ref[0, rows] = part(1).astype(BF16)
        if with_gate:
            g_ref[0, rows] = part(4).astype(BF16)


def _hgrn_in(x, tabs, gain, shift, scale, lb, w_in, *, with_gate):
    b_, t_, d_ = x.shape
    tm = min(ROW_TILE, t_)
    with_pos = tabs is not None
    per_sample = lambda a: pl.BlockSpec((1, 1, d_), (lambda b, i: (b, 0, 0)) if a.shape[0] > 1 else (lambda b, i: (0, 0, 0)))
    row = pl.BlockSpec((1, tm, d_), lambda b, i: (b, i, 0))
    vec = pl.BlockSpec((1, d_), lambda b, i: (0, 0))
    args, in_specs = [x], [row]
    if with_pos:
        rowtab, coltab = tabs
        rows_per_tile = tm // GRID_W
        args += [rowtab.reshape(rowtab.shape[0] // rows_per_tile, rows_per_tile, rowtab.shape[1]), coltab]
        in_specs += [pl.BlockSpec((1, rows_per_tile, rowtab.shape[1]), lambda b, i: (i, 0, 0)),
                     pl.BlockSpec(coltab.shape, lambda b, i: (0, 0))]
    args += [gain, shift, scale, lb, w_in]
    in_specs += [vec, per_sample(shift), per_sample(scale), vec, pl.BlockSpec(w_in.shape, lambda b, i: (0, 0))]
    out_shape, out_specs = [], []
    if with_pos:
        out_shape.append(jax.ShapeDtypeStruct((b_, t_, d_), F32))
        out_specs.append(row)
    out_shape += [jax.ShapeDtypeStruct((b_, t_, d_), BF16)] * 2 + [jax.ShapeDtypeStruct((b_, t_, d_), F32)] * 2
    out_specs += [row] * 4
    if with_gate:
        out_shape.append(jax.ShapeDtypeStruct((b_, t_, d_), BF16))
        out_specs.append(row)
    return pl.pallas_call(
        functools.partial(_hgrn_in_kernel, with_pos=with_pos, with_gate=with_gate),
        grid=(b_, t_ // tm),
        in_specs=in_specs, out_specs=out_specs, out_shape=out_shape,
        compiler_params=_params("arbitrary", "arbitrary", vmem=V7X_VMEM_LIMIT),
        name="hgrn_in_latent" if with_pos else "hgrn_in_context",
    )(*args)


def _gla_consts(reverse):
    c = GLA_CHUNK
    idx = np.arange(c)
    rank = (c - 1 - idx) if reverse else idx
    tri = (rank[None, :] <= rank[:, None]).astype(np.float32)
    hc = c // 2
    hrank = rank[:hc] - rank[:hc].min()
    lvl = np.full((hc, hc), -1, np.int32)
    rt, rs = hrank[:, None], hrank[None, :]
    lvl[rt == rs] = 0
    for level in range(1, GLA_LEVELS):
        blk, half = 1 << level, 1 << (level - 1)
        lvl[(rt // blk == rs // blk) & ((rt % blk) >= half) & ((rs % blk) < half)] = level
    return jnp.asarray(tri, BF16), jnp.asarray(lvl)


def _later_group(level, reverse, group):
    rank = (GLA_CHUNK - 1 - 8 * group) if reverse else 8 * group
    return (rank % (1 << level)) >= (1 << (level - 1))


def _boundary_rows(level, reverse):
    c = GLA_CHUNK
    blk, half = 1 << level, 1 << (level - 1)
    rows = []
    for i in range(c):
        rank = (c - 1 - i) if reverse else i
        brank = (rank // blk) * blk + half - 1
        rows.append((c - 1 - brank) if reverse else brank)
    return rows


def _gla_low_levels(q, kk, lf, cum, lvl, bcast, halves, reverse):
    ng = q.shape[0] // 8
    qb, kb = q.astype(BF16), kk.astype(BF16)
    tiles = [jnp.where(lvl == 0, lax.dot_general(qb[hs], kb[hs], NT_DIMS, preferred_element_type=F32), 0.0)
             for hs in halves]
    row8 = lax.broadcasted_iota(jnp.int32, (8, HEAD_DIM), 0)
    rank8 = (7 - row8) if reverse else row8
    for level in range(1, 4):
        sgn8 = jnp.where(((rank8 >> (level - 1)) & 1) == 1, 1.0, -1.0)
        sgn = jnp.concatenate([sgn8] * ng, axis=0)
        later = sgn > 0.0
        if level == 1:
            g = jnp.where(later, lf, 0.0)
        else:
            brow = _boundary_rows(level, reverse)
            pieces = []
            for grp in range(ng):
                first = bcast(brow[8 * grp])
                pieces.append(jnp.where(row8 < 4, first, bcast(brow[8 * grp + 7])) if level == 2 else first)
            g = (cum - jnp.concatenate(pieces, axis=0)) * sgn
        xe = (jnp.where(later, q, kk) * jnp.exp(g)).astype(BF16)
        for h, hs in enumerate(halves):
            s = lax.dot_general(xe[hs], xe[hs], NT_DIMS, preferred_element_type=F32)
            tiles[h] = jnp.where(lvl == level, s, tiles[h])
    return tiles


def _gla_chunk(q_ref, v_ref, lf_ref, lanes, st_ref, cum_ref, tri_ref, lvl_ref, o_ref, reverse, merged):
    c = GLA_CHUNK
    hc, ng = c // 2, c // 8
    halves = (slice(0, hc), slice(hc, c))
    lf = lf_ref[0, :, lanes]
    q = q_ref[0, :, lanes].astype(F32)
    v = v_ref[0, :, lanes]
    kk = 1.0 - jnp.exp(lf)
    hi = lf.astype(BF16)
    lo = (lf - hi.astype(F32)).astype(BF16)
    two = jnp.dot(tri_ref[...], jnp.concatenate([hi, lo], axis=1), preferred_element_type=F32)
    cum = two[:, HEAD_DIM:] + two[:, :HEAD_DIM]
    cum_ref[...] = cum
    lvl = lvl_ref[...]
    bcast = lambda r: jnp.broadcast_to(cum_ref[r:r + 1, :], (8, HEAD_DIM))
    groups = lambda a: [a[8 * i:8 * i + 8] for i in range(a.shape[0] // 8)]

    if merged:
        blk = 1 << merged
        cache, pieces = {}, []
        for grp in range(ng):
            rank = (c - 1 - 8 * grp) if reverse else 8 * grp
            first = (rank // blk) * blk
            row = (c - 1 - first) if reverse else first
            pieces.append(cache.setdefault(row, bcast(row)))
        inside = jnp.logical_and(lvl >= 0, lvl <= merged)
        tiles = []
        for h, hs in enumerate(halves):
            d = cum[hs] - jnp.concatenate(pieces[h * ng // 2:(h + 1) * ng // 2], axis=0)
            xq = (q[hs] * jnp.exp(d)).astype(BF16)
            xk = (kk[hs] * jnp.exp(-d)).astype(BF16)
            tiles.append(jnp.where(inside, lax.dot_general(xq, xk, NT_DIMS, preferred_element_type=F32), 0.0))
    else:
        tiles = _gla_low_levels(q, kk, lf, cum, lvl, bcast, halves, reverse)

    tile_rows = [groups(t) for t in tiles]
    lvl_rows = groups(lvl)
    q_rows, k_rows, cum_rows = groups(q), groups(kk), groups(cum)
    for h in range(2):
        half_groups = range(h * ng // 2, (h + 1) * ng // 2)
        for level in range(max(4, merged + 1), GLA_LEVELS):
            brow = _boundary_rows(level, reverse)
            cache = {}
            g_rows, x_rows = [], []
            for grp in half_groups:
                cb = cache.setdefault(brow[8 * grp], bcast(brow[8 * grp]))
                later = _later_group(level, reverse, grp)
                g_rows.append(cum_rows[grp] - cb if later else cb - cum_rows[grp])
                x_rows.append(q_rows[grp] if later else k_rows[grp])
            xe = jnp.concatenate(x_rows, axis=0) * jnp.exp(jnp.concatenate(g_rows, axis=0))
            xe_rows = groups(xe)
            later_local = [grp - half_groups[0] for grp in half_groups if _later_group(level, reverse, grp)]
            qc = jnp.concatenate([xe_rows[i] for i in later_local], axis=0).astype(BF16)
            s = lax.dot_general(qc, xe.astype(BF16), NT_DIMS, preferred_element_type=F32)
            for i, local in enumerate(later_local):
                tile_rows[h][local] = jnp.where(lvl_rows[local] == level, s[8 * i:8 * i + 8], tile_rows[h][local])

    early, late = (1, 0) if reverse else (0, 1)
    cb = cum_ref[_boundary_rows(GLA_LEVELS, reverse)[0]:_boundary_rows(GLA_LEVELS, reverse)[0] + 1, :]
    ql = (q[halves[late]] * jnp.exp(cum[halves[late]] - cb)).astype(BF16)
    ke = (kk[halves[early]] * jnp.exp(cb - cum[halves[early]])).astype(BF16)
    cross = lax.dot_general(ql, ke, NT_DIMS, preferred_element_type=F32)
    t_a, t_b = (jnp.concatenate(rows, axis=0) for rows in tile_rows)
    zero = jnp.zeros((hc, hc), F32)
    if reverse:
        scores = jnp.concatenate([jnp.concatenate([t_a, cross], axis=1), jnp.concatenate([zero, t_b], axis=1)], axis=0)
    else:
        scores = jnp.concatenate([jnp.concatenate([t_a, zero], axis=1), jnp.concatenate([cross, t_b], axis=1)], axis=0)

    st = st_ref[...]
    qe = (q * jnp.exp(cum)).astype(BF16)
    o = jnp.dot(scores.astype(BF16), v, preferred_element_type=F32)
    o = o + lax.dot_general(qe, st.astype(BF16), NT_DIMS, preferred_element_type=F32)
    o_ref[0, :, lanes] = o.astype(o_ref.dtype)
    last_row = 0 if reverse else c - 1
    last = cum_ref[last_row:last_row + 1, :]
    ke_all = (kk * jnp.exp(last - cum)).astype(BF16)
    st_ref[...] = st * jnp.exp(last) + lax.dot_general(v, ke_all, TN_DIMS, preferred_element_type=F32)


def _gla_kernel(qf_ref, vf_ref, lff_ref, qb_ref, vb_ref, lfb_ref, s0f_ref, s0b_ref,
                trif_ref, lvlf_ref, trib_ref, lvlb_ref,
                of_ref, ob_ref, sff_ref, sfb_ref, stf_ref, stb_ref, cumf_ref, cumb_ref, *, merged):
    j = pl.program_id(2)

    @pl.when(j == 0)
    def _():
        stf_ref[...] = s0f_ref[0]
        stb_ref[...] = s0b_ref[0]

    for k in range(stf_ref.shape[0]):
        lanes = slice(k * HEAD_DIM, (k + 1) * HEAD_DIM)
        _gla_chunk(qf_ref, vf_ref, lff_ref, lanes, stf_ref.at[k], cumf_ref.at[k], trif_ref, lvlf_ref, of_ref,
                   False, merged)
        _gla_chunk(qb_ref, vb_ref, lfb_ref, lanes, stb_ref.at[k], cumb_ref.at[k], trib_ref, lvlb_ref, ob_ref,
                   True, merged)

    @pl.when(j == pl.num_programs(2) - 1)
    def _():
        sff_ref[0] = stf_ref[...]
        sfb_ref[0] = stb_ref[...]


def _gla_bidir(q, v, lf_f, lf_b, s0f, s0b, lb):
    worst = (2 ** GLA_MERGED - 1) * jnp.max(-jnp.log(lb))
    run = lambda merged: (lambda *a: _gla_call(*a, merged=merged))
    return lax.cond(worst < GLA_MAX_EXPONENT, run(GLA_MERGED), run(0), q, v, lf_f, lf_b, s0f, s0b)


def _gla_call(q, v, lf_f, lf_b, s0f, s0b, *, merged):
    b_, t_, d_ = q.shape
    h_ = d_ // HEAD_DIM
    hp = math.gcd(GLA_HEADS_PER_STEP, h_)
    c = GLA_CHUNK
    n = t_ // c
    fwd = lambda b, h, j: (b, j, h)
    bwd = lambda b, h, j: (b, n - 1 - j, h)
    st = lambda b, h, j: (b, h, 0, 0)
    const = lambda b, h, j: (0, 0)
    blk = lambda im: pl.BlockSpec((1, c, hp * HEAD_DIM), im)
    st_spec = pl.BlockSpec((1, hp, HEAD_DIM, HEAD_DIM), st)
    cspecs = [pl.BlockSpec((c, c), const), pl.BlockSpec((c // 2, c // 2), const)]
    state = pltpu.VMEM((hp, HEAD_DIM, HEAD_DIM), F32)
    cum = pltpu.VMEM((hp, c, HEAD_DIM), F32)
    return pl.pallas_call(
        functools.partial(_gla_kernel, merged=merged),
        grid=(b_, h_ // hp, n),
        in_specs=[blk(fwd), blk(fwd), blk(fwd), blk(bwd), blk(bwd), blk(bwd), st_spec, st_spec] + cspecs + cspecs,
        out_specs=[blk(fwd), blk(bwd), st_spec, st_spec],
        out_shape=[jax.ShapeDtypeStruct((b_, t_, d_), BF16)] * 2
                  + [jax.ShapeDtypeStruct((b_, h_, HEAD_DIM, HEAD_DIM), F32)] * 2,
        scratch_shapes=[state, state, cum, cum],
        compiler_params=_params("arbitrary", "arbitrary", "arbitrary"),
        name="gla_merged" if merged else "gla_split",
    )(q, v, lf_f, q, v, lf_b, s0f, s0b, *_gla_consts(False), *_gla_consts(True))


def _mixer_epilogue(y, rows, x_ref, w_ref, gate_ref, gain_ref, shift_ref, scale_ref, wr_ref, x1_ref, h_ref, aff_ref):
    y = jnp.dot(y.astype(BF16), w_ref[...], preferred_element_type=F32)
    x1 = x_ref[0, rows] + gate_ref[0] * y
    x1_ref[0, rows] = x1
    hf = _modulate(x1, gain_ref[...], shift_ref[0], scale_ref[0])
    h_ref[0, rows] = hf.astype(BF16)
    logits = _dot_bf16x3(wr_ref[...], hf, NT_DIMS)
    e = jnp.exp(logits - jnp.max(logits, axis=0, keepdims=True))
    aff_ref[0, :, rows] = e / jnp.sum(e, axis=0, keepdims=True)


def _hgrn_out_kernel(of_ref, ob_ref, g_ref, x_ref, hnorm_ref, w_ref, *rest):
    hn = hnorm_ref[...]
    for rows in _sub_tiles(x_ref.shape[1]):
        o = of_ref[0, rows].astype(F32) + ob_ref[0, rows].astype(F32)
        heads = []
        for h in range(o.shape[-1] // HEAD_DIM):
            oh = o[:, h * HEAD_DIM:(h + 1) * HEAD_DIM]
            heads.append(oh * lax.rsqrt(jnp.mean(oh * oh, axis=-1, keepdims=True) + EPS) * hn)
        y = jnp.concatenate(heads, axis=-1) * _silu(g_ref[0, rows].astype(F32))
        _mixer_epilogue(y, rows, x_ref, w_ref, *rest)


def _conv_mixer_kernel(prev_ref, next_ref, x_ref, shift1_ref, scale1_ref, gain1_ref, win_ref, wc_ref, w_ref, *rest):
    i = pl.program_id(1)
    tm, d_ = x_ref.shape[1], x_ref.shape[2]
    subs = _sub_tiles(tm)
    halo = jnp.concatenate([prev_ref[0], next_ref[0]], axis=0)
    b_gate, cu = [], []
    for k, rows in enumerate(subs):
        xs = x_ref[0, rows]
        if k == 0:
            xs = jnp.concatenate([xs, halo], axis=0)
        h = _modulate(xs, gain1_ref[...], shift1_ref[0], scale1_ref[0]).astype(BF16)
        part = lambda p, hh: jnp.dot(hh, win_ref[:, p * d_:(p + 1) * d_], preferred_element_type=F32)
        n = rows.stop - rows.start
        b_gate.append(part(0, h[:n]))
        cu.append(part(1, h) * part(2, h))
    cu_halo = cu[0][subs[0].stop - subs[0].start:]
    cu = jnp.concatenate([cu[0][:subs[0].stop - subs[0].start]] + cu[1:], axis=0)
    rid = lax.broadcasted_iota(jnp.int32, cu.shape, 0)
    before = jnp.where(i == 0, 0.0, cu_halo[7:8])
    after = jnp.where(i == pl.num_programs(1) - 1, 0.0, cu_halo[8:9])
    left = jnp.where(rid == 0, before, pltpu.roll(cu, 1, 0))
    right = jnp.where(rid == tm - 1, after, pltpu.roll(cu, tm - 1, 0))
    wc = wc_ref[...]
    conv = left * wc[0:1] + cu * wc[1:2] + right * wc[2:3]
    for k, rows in enumerate(subs):
        _mixer_epilogue(b_gate[k] * conv[rows], rows, x_ref, w_ref, *rest)


def _mixer_out(kernel, name, row_args, halo_args, x, sample_args, small_args, gate, gain, shift, scale, w_router_t):
    b_, t_, d_ = x.shape
    e_ = w_router_t.shape[0]
    tm = min(ROW_TILE, t_)
    row = pl.BlockSpec((1, tm, d_), lambda b, i: (b, i, 0))
    per_sample = pl.BlockSpec((1, 1, d_), lambda b, i: (b, 0, 0))
    whole = lambda a: pl.BlockSpec(a.shape, lambda b, i: (0,) * a.ndim)
    n8 = t_ // 8
    halo_specs = [pl.BlockSpec((1, 8, d_), lambda b, i: (b, jnp.maximum(i * (tm // 8) - 1, 0), 0)),
                  pl.BlockSpec((1, 8, d_), lambda b, i: (b, jnp.minimum((i + 1) * (tm // 8), n8 - 1), 0))]
    return pl.pallas_call(
        kernel,
        grid=(b_, t_ // tm),
        in_specs=[row] * len(row_args) + halo_specs[:len(halo_args)] + [row] + [per_sample] * len(sample_args)
                 + [whole(a) for a in small_args] + [per_sample, whole(gain), per_sample, per_sample, whole(w_router_t)],
        out_specs=[row, row, pl.BlockSpec((1, e_, tm), lambda b, i: (b, 0, i))],
        out_shape=[jax.ShapeDtypeStruct((b_, t_, d_), F32), jax.ShapeDtypeStruct((b_, t_, d_), BF16),
                   jax.ShapeDtypeStruct((b_, e_, t_), F32)],
        compiler_params=_params("arbitrary", "arbitrary", vmem=V7X_VMEM_LIMIT),
        name=name,
    )(*row_args, *halo_args, x, *sample_args, *small_args, gate, gain, shift, scale, w_router_t)


def _route_kernel(aff_ref, tri_ref, blockind_ref, slot_ref, base_ref, dense_ref, *, cap):
    aff = aff_ref[0]
    e_, t_ = aff.shape

    def as_float(word):
        return pltpu.bitcast(word, F32)

    def count_ge(th):
        return jnp.sum(jnp.where(aff >= th, 1.0, 0.0), axis=1, keepdims=True)

    def search(_, carry):
        lo, hi = carry
        mid = lo + ((hi - lo + 1) >> 1)
        ok = count_ge(as_float(mid)) >= cap
        return jnp.where(ok, mid, lo), jnp.where(ok, hi, mid - 1)

    lo0 = jnp.zeros((e_, 1), jnp.int32)
    hi0 = jnp.full((e_, 1), 0x7F7FFFFF, jnp.int32)
    kth, _ = lax.fori_loop(0, 32, search, (lo0, hi0))
    above = aff >= as_float(kth + 1)
    tied = jnp.logical_and(aff >= as_float(kth), jnp.logical_not(above))
    need = cap - jnp.sum(jnp.where(above, 1.0, 0.0), axis=1, keepdims=True)
    tri = tri_ref[...]
    tb = tri.shape[0]
    carry_t = jnp.zeros((e_, 1), F32)
    carry_s = jnp.zeros((e_, 1), F32)
    sel_blocks = []
    for j in range(t_ // tb):
        cols = slice(j * tb, (j + 1) * tb)
        tied_j = tied[:, cols]
        ct = jnp.dot(jnp.where(tied_j, 1.0, 0.0).astype(BF16), tri, preferred_element_type=F32) + carry_t
        carry_t = ct[:, tb - 1:tb]
        sel_j = jnp.where(above[:, cols], 1.0, jnp.where(tied_j & (ct <= need), 1.0, 0.0))
        cs = jnp.dot(sel_j.astype(BF16), tri, preferred_element_type=F32) + carry_s
        carry_s = cs[:, tb - 1:tb]
        slot_ref[0, :, cols] = jnp.where(sel_j > 0.0, cs - 1.0, -1.0).astype(jnp.int32)
        sel_blocks.append(sel_j.astype(BF16))
    sel = jnp.concatenate(sel_blocks, axis=1)
    counts = jnp.dot(sel, blockind_ref[...], preferred_element_type=F32)
    base, end = counts[:, :128], counts[:, 128:]
    base_ref[0] = base.astype(jnp.int32)
    span = end - jnp.floor(base * (1.0 / SLOT_ALIGN)) * SLOT_ALIGN
    widest = jnp.max(span, axis=0, keepdims=True)
    dense_ref[0] = sum(jnp.where(widest <= w, 1, 0) for w in DENSE_WINDOWS).astype(jnp.int32)


def _route(aff):
    b_, e_, t_ = aff.shape
    cap = EC_CAPACITY_FACTOR * t_ // e_
    tb = TOKEN_BLOCK
    nb = t_ // tb
    tri = jnp.asarray(np.triu(np.ones((tb, tb), np.float32)), BF16)
    tok, col = np.arange(t_)[:, None], np.arange(128)[None, :]
    blockind = np.concatenate([(tok < col * tb) & (col <= nb), (tok < (col + 1) * tb) & (col < nb)], axis=1)
    slot, base, dense = pl.pallas_call(
        functools.partial(_route_kernel, cap=cap),
        grid=(b_,),
        in_specs=[pl.BlockSpec((1, e_, t_), lambda b: (b, 0, 0)),
                  pl.BlockSpec((tb, tb), lambda b: (0, 0)),
                  pl.BlockSpec((t_, 256), lambda b: (0, 0))],
        out_specs=[pl.BlockSpec((1, e_, t_), lambda b: (b, 0, 0)), pl.BlockSpec((1, e_, 128), lambda b: (b, 0, 0)),
                   pl.BlockSpec((1, 1, 128), lambda b: (b, 0, 0))],
        out_shape=[jax.ShapeDtypeStruct((b_, e_, t_), jnp.int32), jax.ShapeDtypeStruct((b_, e_, 128), jnp.int32),
                   jax.ShapeDtypeStruct((b_, 1, 128), jnp.int32)],
        compiler_params=_params("arbitrary"),
        name="route",
    )(aff, tri, jnp.asarray(blockind.astype(np.float32), BF16))
    return slot, base[:, :, :nb + 1].reshape(-1), dense[:, 0, :nb].reshape(-1)


def _window_plan(tbl_ref, b, e, tb, ne, nb):
    idx = (b * ne + e) * (nb + 1) + tb
    base, end = tbl_ref[idx], tbl_ref[idx + 1]
    start = _align_down(base)
    n_win = jnp.where(end > base, (end - start + SLOT_WINDOW - 1) >> (SLOT_WINDOW.bit_length() - 1), 0)
    return start, n_win


def _align_down(slot):
    shift = SLOT_ALIGN.bit_length() - 1
    return (slot >> shift) << shift


def _dense_start(tbl_ref, b, e, tb, ne, nb, cap, window):
    base = tbl_ref[(b * ne + e) * (nb + 1) + tb]
    return pl.multiple_of(jnp.minimum(_align_down(base), cap - window), SLOT_ALIGN)


def _gather_kernel(tbl_ref, dense_ref, h_ref, slot_ref, xg_ref, *, ne, nb, cap):
    b, step = pl.program_id(0), pl.program_id(2)
    n_tok = TOKEN_BLOCK

    @pl.when(step == 0)
    def _():
        xg_ref[...] = jnp.zeros(xg_ref.shape, xg_ref.dtype)

    def dense_block(sub, window):
        tb = step * BLOCKS_PER_STEP + sub
        toks = slice(sub * n_tok, (sub + 1) * n_tok)
        starts = [_dense_start(tbl_ref, b, e, tb, ne, nb, cap, window) for e in range(ne)]
        ids = lax.broadcasted_iota(jnp.int32, (window, n_tok), 0)
        onehot = jnp.concatenate(
            [jnp.where(ids == slot_ref[0, e:e + 1, toks] - starts[e], 1.0, 0.0).astype(BF16) for e in range(ne)],
            axis=0)
        rows = jnp.dot(onehot, h_ref[0, toks, :], preferred_element_type=F32).astype(BF16)
        for e in range(ne):
            win = xg_ref.at[0, e, pl.ds(starts[e], window), :]
            win[...] = win[...] + rows[e * window:(e + 1) * window]

    def windowed_block(sub):
        tb = step * BLOCKS_PER_STEP + sub
        toks = slice(sub * n_tok, (sub + 1) * n_tok)
        h = h_ref[0, toks, :]
        ids0 = lax.broadcasted_iota(jnp.int32, (SLOT_WINDOW, n_tok), 0)
        for e in range(ne):
            srow = slot_ref[0, e:e + 1, toks]
            start, n_win = _window_plan(tbl_ref, b, e, tb, ne, nb)

            def body(k, carry, e=e, srow=srow, start=start):
                lo = start + k * SLOT_WINDOW
                w0 = pl.multiple_of(jnp.minimum(lo, cap - SLOT_WINDOW), SLOT_ALIGN)
                ids = w0 + ids0
                onehot = jnp.where(ids >= lo, jnp.where(ids == srow, 1.0, 0.0), 0.0).astype(BF16)
                rows = jnp.dot(onehot, h, preferred_element_type=F32)
                win = xg_ref.at[0, e, pl.ds(w0, SLOT_WINDOW), :]
                win[...] = win[...] + rows.astype(BF16)
                return carry

            lax.fori_loop(0, n_win, body, 0)

    _per_block_dispatch(dense_ref, b * nb + step * BLOCKS_PER_STEP, BLOCKS_PER_STEP, dense_block, windowed_block)


def _per_block_dispatch(dense_ref, first, n_blocks, dense_block, windowed_block):
    fits = [dense_ref[first + sub] for sub in range(n_blocks)]
    common = functools.reduce(jnp.minimum, fits)
    n_tiers = len(DENSE_WINDOWS)
    for k, window in enumerate(DENSE_WINDOWS):
        chosen = (common >= n_tiers) if k == 0 else (common == n_tiers - k)

        @pl.when(chosen)
        def _(window=window):
            for sub in range(n_blocks):
                dense_block(sub, window)

    @pl.when(common == 0)
    def _():
        for sub in range(n_blocks):
            pl.when(fits[sub] > 0)(functools.partial(dense_block, sub, DENSE_WINDOWS[-1]))
            pl.when(fits[sub] == 0)(functools.partial(windowed_block, sub))


def _gather(h, slot, tbl, dense):
    b_, t_, d_ = h.shape
    e_ = slot.shape[1]
    cap = EC_CAPACITY_FACTOR * t_ // e_
    tb = TOKEN_BLOCK * BLOCKS_PER_STEP
    nb = t_ // TOKEN_BLOCK
    dh = d_ // 2
    return pl.pallas_call(
        functools.partial(_gather_kernel, ne=e_, nb=nb, cap=cap),
        grid_spec=pltpu.PrefetchScalarGridSpec(
            num_scalar_prefetch=2,
            grid=(b_, 2, t_ // tb),
            in_specs=[pl.BlockSpec((1, tb, dh), lambda b, c, i, *_: (b, i, c)),
                      pl.BlockSpec((1, e_, tb), lambda b, c, i, *_: (b, 0, i))],
            out_specs=pl.BlockSpec((1, e_, cap, dh), lambda b, c, i, *_: (b, 0, 0, c)),
        ),
        out_shape=jax.ShapeDtypeStruct((b_, e_, cap, d_), BF16),
        compiler_params=_params("arbitrary", "arbitrary", "arbitrary", vmem=V7X_VMEM_LIMIT),
        name="moe_gather",
    )(tbl, dense, h, slot)


def _expert_kernel(xg_ref, wg_ref, wu_ref, wd_ref, y_ref, acc_ref):
    f = pl.program_id(1)
    n_b, _, cap, _ = xg_ref.shape
    wg = wg_ref[0, 0].astype(BF16)
    wu = wu_ref[0, 0].astype(BF16)
    wd = wd_ref[0, 0].astype(BF16)

    @pl.when(jnp.logical_and(pl.program_id(0) == 0, f == 0))
    def _():
        acc_ref[...] = jnp.zeros(acc_ref.shape, F32)

    for b in range(n_b):
        for r in range(cap // FFN_ROWS):
            rows = pl.ds(r * FFN_ROWS, FFN_ROWS)
            acc_rows = pl.ds((b * cap) + r * FFN_ROWS, FFN_ROWS)
            xr = xg_ref[b, 0, rows, :]
            a = jnp.dot(xr, wg, preferred_element_type=F32)
            u = jnp.dot(xr, wu, preferred_element_type=F32)
            part = jnp.dot((_silu(a) * u).astype(BF16), wd, preferred_element_type=F32)
            total = jnp.where(f == 0, 0.0, acc_ref[acc_rows, :]) + part
            acc_ref[acc_rows, :] = total
            y_ref[b, 0, rows, :] = total.astype(BF16)


def _experts(xg, layer, w_gate, w_up, w_down):
    b_, e_, cap, d_ = xg.shape
    f_ = w_gate.shape[-1]
    ft = min(FFN_TILE, f_)
    return pl.pallas_call(
        _expert_kernel,
        grid=(e_, f_ // ft),
        in_specs=[pl.BlockSpec((b_, 1, cap, d_), lambda e, f: (0, e, 0, 0)),
                  pl.BlockSpec((1, 1, d_, ft), lambda e, f: (layer, e, 0, f)),
                  pl.BlockSpec((1, 1, d_, ft), lambda e, f: (layer, e, 0, f)),
                  pl.BlockSpec((1, 1, ft, d_), lambda e, f: (layer, e, f, 0))],
        out_specs=pl.BlockSpec((b_, 1, cap, d_), lambda e, f: (0, e, 0, 0)),
        out_shape=jax.ShapeDtypeStruct((b_, e_, cap, d_), BF16),
        scratch_shapes=[pltpu.VMEM((b_ * cap, d_), F32)],
        compiler_params=_params("arbitrary", "arbitrary", vmem=V7X_VMEM_LIMIT),
        name="moe_experts",
    )(xg, w_gate, w_up, w_down)


def _combine_kernel(tbl_ref, dense_ref, y_ref, slot_ref, aff_ref, x_ref, gate_ref, *rest, ne, nb, cap, final):
    if final:
        gain_ref, out_ref, acc_ref = rest
    else:
        out_ref, acc_ref = rest
    b, step = pl.program_id(0), pl.program_id(1)
    n_tok = TOKEN_BLOCK

    def finish(toks, moe):
        out = x_ref[0, toks, :] + gate_ref[0] * moe
        if final:
            out = out * lax.rsqrt(jnp.mean(out * out, axis=-1, keepdims=True) + EPS) * gain_ref[...]
        out_ref[0, toks, :] = out

    def dense_block(sub, window):
        tb = step * COMBINE_BLOCKS + sub
        toks = slice(sub * n_tok, (sub + 1) * n_tok)
        starts = [_dense_start(tbl_ref, b, e, tb, ne, nb, cap, window) for e in range(ne)]
        ids = lax.broadcasted_iota(jnp.int32, (window, n_tok), 0)
        weights = jnp.concatenate(
            [jnp.where(ids == slot_ref[0, e:e + 1, toks] - starts[e], aff_ref[0, e:e + 1, toks], 0.0).astype(BF16)
             for e in range(ne)], axis=0)
        yw = jnp.concatenate([y_ref[0, e, pl.ds(starts[e], window), :] for e in range(ne)], axis=0)
        finish(toks, lax.dot_general(weights, yw, TN_DIMS, preferred_element_type=F32))

    def windowed_block(sub):
        tb = step * COMBINE_BLOCKS + sub
        toks = slice(sub * n_tok, (sub + 1) * n_tok)
        ids0 = lax.broadcasted_iota(jnp.int32, (SLOT_WINDOW, n_tok), 0)
        acc_ref[...] = jnp.zeros(acc_ref.shape, F32)
        for e in range(ne):
            srow = slot_ref[0, e:e + 1, toks]
            grow = aff_ref[0, e:e + 1, toks]
            start, n_win = _window_plan(tbl_ref, b, e, tb, ne, nb)

            def body(k, carry, e=e, srow=srow, grow=grow, start=start):
                lo = start + k * SLOT_WINDOW
                w0 = pl.multiple_of(jnp.minimum(lo, cap - SLOT_WINDOW), SLOT_ALIGN)
                ids = w0 + ids0
                weights = jnp.where(ids >= lo, jnp.where(ids == srow, grow, 0.0), 0.0).astype(BF16)
                yw = y_ref[0, e, pl.ds(w0, SLOT_WINDOW), :]
                acc_ref[...] = acc_ref[...] + lax.dot_general(weights, yw, TN_DIMS, preferred_element_type=F32)
                return carry

            lax.fori_loop(0, n_win, body, 0)
        finish(toks, acc_ref[...])

    _per_block_dispatch(dense_ref, b * nb + step * COMBINE_BLOCKS, COMBINE_BLOCKS, dense_block, windowed_block)


def _combine(y, slot, aff, tbl, dense, x, gate, final_gain):
    b_, t_, d_ = x.shape
    e_, cap = y.shape[1], y.shape[2]
    tb = TOKEN_BLOCK * COMBINE_BLOCKS
    nb = t_ // TOKEN_BLOCK
    final = final_gain is not None
    route_spec = pl.BlockSpec((1, e_, tb), lambda b, i, *_: (b, 0, i))
    row = pl.BlockSpec((1, tb, d_), lambda b, i, *_: (b, i, 0))
    in_specs = [pl.BlockSpec((1, e_, cap, d_), lambda b, i, *_: (b, 0, 0, 0), pipeline_mode=pl.Buffered(1)),
                route_spec, route_spec, row, pl.BlockSpec((1, 1, d_), lambda b, i, *_: (b, 0, 0))]
    args = [tbl, dense, y, slot, aff, x, gate]
    if final:
        in_specs.append(pl.BlockSpec((1, d_), lambda b, i, *_: (0, 0)))
        args.append(final_gain)
    return pl.pallas_call(
        functools.partial(_combine_kernel, ne=e_, nb=nb, cap=cap, final=final),
        grid_spec=pltpu.PrefetchScalarGridSpec(
            num_scalar_prefetch=2,
            grid=(b_, t_ // tb),
            in_specs=in_specs,
            out_specs=row,
            scratch_shapes=[pltpu.VMEM((TOKEN_BLOCK, d_), F32)],
        ),
        out_shape=jax.ShapeDtypeStruct((b_, t_, d_), F32),
        compiler_params=_params("arbitrary", "arbitrary", vmem=V7X_VMEM_LIMIT),
        name="moe_combine_final" if final else "moe_combine",
    )(*args)


def _moe(x, h, aff, gate, layer, w_gate, w_up, w_down, final_gain=None):
    slot, tbl, dense = _route(aff)
    xg = _gather(h, slot, tbl, dense)
    y = _experts(xg, layer, w_gate, w_up, w_down)
    return _combine(y, slot, aff, tbl, dense, x, gate, final_gain)


def kernel(x, c, ctx, c_ctx, ada_w, ada_b, norm_mix, norm_ffn, norm_final, hg_w_in, hg_lb_logits, hg_norm, hg_w_out, sc_w_in, sc_conv, sc_w_out, moe_router, moe_w_gate, moe_w_up, moe_w_down):
    b_, t_, d_ = x.shape
    depth = ada_w.shape[0]
    n_ada = ada_w.shape[-1] // d_
    n_heads = d_ // HEAD_DIM
    n_experts = moe_router.shape[-1]
    cap = EC_CAPACITY_FACTOR * t_ // n_experts
    assert depth == 2 and n_ada == 6 and b_ + 1 <= 8
    assert t_ % GLA_CHUNK == 0 and ctx.shape[1] % GLA_CHUNK == 0 and SUB_ROWS % GRID_W == 0
    assert d_ % (2 * HEAD_DIM) == 0 and t_ % min(ROW_TILE, t_) == 0
    assert t_ % (TOKEN_BLOCK * BLOCKS_PER_STEP) == 0 and t_ % (TOKEN_BLOCK * COMBINE_BLOCKS) == 0
    assert cap % SLOT_WINDOW == 0 and cap >= max(DENSE_WINDOWS) and cap % min(FFN_ROWS, cap) == 0
    assert moe_w_gate.shape[-1] % min(FFN_TILE, moe_w_gate.shape[-1]) == 0

    cond = jnp.concatenate([c, c_ctx[None], jnp.zeros((8 - b_ - 1, d_), F32)], axis=0)
    mod = _ada_vectors(cond, ada_w, ada_b, n_ada)
    vec = lambda i, j: mod[i, j, :b_][:, None, :]
    cvec = lambda i, j: mod[i, j, b_][None, None, :]
    rowtab, coltab, lower = _tables(hg_lb_logits, t_)
    row_of = lambda a, i: a[i][None, :]
    router_t = lambda i: jnp.swapaxes(moe_router[i], 0, 1)

    w_in = hg_w_in[0].astype(BF16)
    lb0 = row_of(lower, 0)
    gain0 = row_of(norm_mix, 0)
    qc, vc, lfc_f, lfc_b = _hgrn_in(ctx, None, gain0, cvec(0, 0), cvec(0, 1), lb0, w_in, with_gate=False)
    zeros = jnp.zeros((b_, n_heads, HEAD_DIM, HEAD_DIM), F32)
    _, _, s_f, s_b = _gla_bidir(qc, vc, lfc_f, lfc_b, zeros, zeros, lb0)
    x0, q, v, lf_f, lf_b, g = _hgrn_in(x, (rowtab, coltab), gain0, vec(0, 0), vec(0, 1), lb0, w_in, with_gate=True)
    o_f, o_b, _, _ = _gla_bidir(q, v, lf_f, lf_b, s_f, s_b, lb0)
    x1, h, aff = _mixer_out(_hgrn_out_kernel, "hgrn_out", [o_f, o_b, g], [], x0, [],
                            [row_of(hg_norm, 0), hg_w_out[0].astype(BF16)],
                            vec(0, 2), row_of(norm_ffn, 0), vec(0, 3), vec(0, 4), router_t(0))
    x2 = _moe(x1, h, aff, vec(0, 5), 0, moe_w_gate, moe_w_up, moe_w_down)

    x3, h, aff = _mixer_out(_conv_mixer_kernel, "conv_mixer", [], [x2, x2], x2, [vec(1, 0), vec(1, 1)],
                            [row_of(norm_mix, 1), sc_w_in[0].astype(BF16), sc_conv[0], sc_w_out[0].astype(BF16)],
                            vec(1, 2), row_of(norm_ffn, 1), vec(1, 3), vec(1, 4), router_t(1))
    return _moe(x3, h, aff, vec(1, 5), 1, moe_w_gate, moe_w_up, moe_w_down, final_gain=norm_final[None, :])
```

```python
import functools
import math

import numpy as np
import jax
import jax.numpy as jnp
from jax import lax
from jax.experimental import pallas as pl
from jax.experimental.pallas import tpu as pltpu

F32 = jnp.float32
BF16 = jnp.bfloat16

EPS = 1e-6
POS_TEMP = 10000.0
GRID_W = 64
HEAD_DIM = 128
EC_CAPACITY_FACTOR = 2
GLA_CHUNK = 256
GLA_LEVELS = 8
GLA_HEADS_PER_STEP = 8
GLA_MERGED = 5
GLA_MAX_EXPONENT = 80.0
ROW_TILE = 512
MIXER_ROW_TILE = 1024
SUB_ROWS = 256
TOKEN_BLOCK = 256
BLOCKS_PER_STEP = 4
COMBINE_BLOCKS = 2
SLOT_WINDOW = 64
DENSE_WINDOWS = (96, 128)
SLOT_ALIGN = 16
FFN_TILE = 1024
FFN_ROWS = 256
V7X_VMEM_LIMIT = 56 * 1024 * 1024

NT_DIMS = (((1,), (1,)), ((), ()))
TN_DIMS = (((0,), (0,)), ((), ()))


def _params(*sem, vmem=None):
    return pltpu.CompilerParams(dimension_semantics=sem, vmem_limit_bytes=vmem)


def _dot_bf16x3(a, b, dims):
    a0, b0 = a.astype(BF16), b.astype(BF16)
    a1 = (a - a0.astype(F32)).astype(BF16)
    b1 = (b - b0.astype(F32)).astype(BF16)
    d = lambda x, y: lax.dot_general(x, y, dims, preferred_element_type=F32)
    return (d(a0, b1) + d(a1, b0)) + d(a0, b0)


def _sub_tiles(n_rows):
    sub = min(SUB_ROWS, n_rows)
    return [slice(s, s + sub) for s in range(0, n_rows, sub)]


def _sigmoid(x):
    return 1.0 / (1.0 + jnp.exp(-x))


def _silu(x):
    return x * _sigmoid(x)


def _modulate(x, gain, shift, scale):
    y = x * lax.rsqrt(jnp.mean(x * x, axis=-1, keepdims=True) + EPS)
    return y * (gain * (1.0 + scale)) + shift


def _ada_kernel(cond_ref, w_ref, b_ref, out_ref):
    s = _silu(cond_ref[...])
    out_ref[0, 0] = _dot_bf16x3(s, w_ref[0], (((1,), (0,)), ((), ()))) + b_ref[0, 0]


def _ada_vectors(cond, ada_w, ada_b, n_ada):
    depth, d_, _ = ada_w.shape
    return pl.pallas_call(
        _ada_kernel,
        grid=(depth, n_ada),
        in_specs=[pl.BlockSpec((8, d_), lambda i, j: (0, 0)),
                  pl.BlockSpec((1, d_, d_), lambda i, j: (i, 0, j)),
                  pl.BlockSpec((1, 1, 1, d_), lambda i, j: (i, j, 0, 0))],
        out_specs=pl.BlockSpec((1, 1, 8, d_), lambda i, j: (i, j, 0, 0)),
        out_shape=jax.ShapeDtypeStruct((depth, n_ada, 8, d_), F32),
        compiler_params=_params("arbitrary", "arbitrary"),
        name="ada_vectors",
    )(cond, ada_w, ada_b.reshape(depth, n_ada, 1, d_))


def _tables_kernel(lb_logits_ref, rowtab_ref, coltab_ref, lb_ref, *, n_freq):
    def table(n_pos):
        p = lax.broadcasted_iota(jnp.int32, (n_pos, n_freq), 0).astype(F32)
        j = lax.broadcasted_iota(jnp.int32, (n_pos, n_freq), 1).astype(F32)
        omega = jnp.exp(j * (-math.log(POS_TEMP) / n_freq))
        ang = p * omega
        return jnp.concatenate([jnp.sin(ang), jnp.cos(ang)], axis=-1)

    rowtab_ref[...] = table(rowtab_ref.shape[0])
    coltab_ref[...] = table(coltab_ref.shape[0])
    logits = lb_logits_ref[...]
    e = jnp.exp(logits - jnp.max(logits, axis=0, keepdims=True))
    sm = e / jnp.sum(e, axis=0, keepdims=True)
    acc = sm[0:1]
    lb_ref[0:1] = acc
    for i in range(1, lb_ref.shape[0]):
        acc = acc + sm[i:i + 1]
        lb_ref[i:i + 1] = acc


def _tables(lb_logits, n_tokens):
    n_lb, d_ = lb_logits.shape
    n_freq = d_ // 4
    rows = n_tokens // GRID_W
    return pl.pallas_call(
        functools.partial(_tables_kernel, n_freq=n_freq),
        out_shape=[jax.ShapeDtypeStruct((rows, 2 * n_freq), F32),
                   jax.ShapeDtypeStruct((GRID_W, 2 * n_freq), F32),
                   jax.ShapeDtypeStruct((n_lb, d_), F32)],
        name="pos_tables",
    )(lb_logits)


def _hgrn_in_kernel(*refs, with_pos, with_gate):
    it = iter(refs)
    x_ref = next(it)
    if with_pos:
        rowtab_ref, coltab_ref = next(it), next(it)
    gain_ref, shift_ref, scale_ref, lb_ref, w_ref = next(it), next(it), next(it), next(it), next(it)
    if with_pos:
        x0_ref = next(it)
    q_ref, v_ref, lff_ref, lfb_ref = next(it), next(it), next(it), next(it)
    g_ref = next(it) if with_gate else None

    d_ = x_ref.shape[-1]
    lb = lb_ref[...]
    for rows in _sub_tiles(x_ref.shape[1]):
        x = x_ref[0, rows]
        if with_pos:
            grid_rows = range(rows.start // GRID_W, rows.stop // GRID_W)
            pos_row = jnp.concatenate(
                [jnp.broadcast_to(rowtab_ref[0, r:r + 1, :], (GRID_W, rowtab_ref.shape[-1])) for r in grid_rows], axis=0)
            pos_col = jnp.concatenate([coltab_ref[...]] * len(grid_rows), axis=0)
            x = x + jnp.concatenate([pos_row, pos_col], axis=-1)
            x0_ref[0, rows] = x
        h = _modulate(x, gain_ref[...], shift_ref[0], scale_ref[0]).astype(BF16)
        part = lambda p: jnp.dot(h, w_ref[:, p * d_:(p + 1) * d_], preferred_element_type=F32)
        lff_ref[0, rows] = jnp.log(lb + (1.0 - lb) * _sigmoid(part(2)))
        lfb_ref[0, rows] = jnp.log(lb + (1.0 - lb) * _sigmoid(part(3)))
        q_ref[0, rows] = (part(0) * HEAD_DIM ** -0.5).astype(BF16)
        v_ref[0, rows] = part(1).astype(BF16)
        if with_gate:
            g_ref[0, rows] = part(4).astype(BF16)


def _hgrn_in(x, tabs, gain, shift, scale, lb, w_in, *, with_gate):
    b_, t_, d_ = x.shape
    tm = min(ROW_TILE, t_)
    with_pos = tabs is not None
    per_sample = lambda a: pl.BlockSpec((1, 1, d_), (lambda b, i: (b, 0, 0)) if a.shape[0] > 1 else (lambda b, i: (0, 0, 0)))
    row = pl.BlockSpec((1, tm, d_), lambda b, i: (b, i, 0))
    vec = pl.BlockSpec((1, d_), lambda b, i: (0, 0))
    args, in_specs = [x], [row]
    if with_pos:
        rowtab, coltab = tabs
        rows_per_tile = tm // GRID_W
        args += [rowtab.reshape(rowtab.shape[0] // rows_per_tile, rows_per_tile, rowtab.shape[1]), coltab]
        in_specs += [pl.BlockSpec((1, rows_per_tile, rowtab.shape[1]), lambda b, i: (i, 0, 0)),
                     pl.BlockSpec(coltab.shape, lambda b, i: (0, 0))]
    args += [gain, shift, scale, lb, w_in]
    in_specs += [vec, per_sample(shift), per_sample(scale), vec, pl.BlockSpec(w_in.shape, lambda b, i: (0, 0))]
    out_shape, out_specs = [], []
    if with_pos:
        out_shape.append(jax.ShapeDtypeStruct((b_, t_, d_), F32))
        out_specs.append(row)
    out_shape += [jax.ShapeDtypeStruct((b_, t_, d_), BF16)] * 2 + [jax.ShapeDtypeStruct((b_, t_, d_), F32)] * 2
    out_specs += [row] * 4
    if with_gate:
        out_shape.append(jax.ShapeDtypeStruct((b_, t_, d_), BF16))
        out_specs.append(row)
    return pl.pallas_call(
        functools.partial(_hgrn_in_kernel, with_pos=with_pos, with_gate=with_gate),
        grid=(b_, t_ // tm),
        in_specs=in_specs, out_specs=out_specs, out_shape=out_shape,
        compiler_params=_params("arbitrary", "arbitrary", vmem=V7X_VMEM_LIMIT),
        name="hgrn_in_latent" if with_pos else "hgrn_in_context",
    )(*args)


def _gla_consts(reverse):
    c = GLA_CHUNK
    idx = np.arange(c)
    rank = (c - 1 - idx) if reverse else idx
    tri = (rank[None, :] <= rank[:, None]).astype(np.float32)
    hc = c // 2
    hrank = rank[:hc] - rank[:hc].min()
    lvl = np.full((hc, hc), -1, np.int32)
    rt, rs = hrank[:, None], hrank[None, :]
    lvl[rt == rs] = 0
    for level in range(1, GLA_LEVELS):
        blk, half = 1 << level, 1 << (level - 1)
        lvl[(rt // blk == rs // blk) & ((rt % blk) >= half) & ((rs % blk) < half)] = level
    return jnp.asarray(tri, BF16), jnp.asarray(lvl)


def _later_group(level, reverse, group):
    rank = (GLA_CHUNK - 1 - 8 * group) if reverse else 8 * group
    return (rank % (1 << level)) >= (1 << (level - 1))


def _boundary_rows(level, reverse):
    c = GLA_CHUNK
    blk, half = 1 << level, 1 << (level - 1)
    rows = []
    for i in range(c):
        rank = (c - 1 - i) if reverse else i
        brank = (rank // blk) * blk + half - 1
        rows.append((c - 1 - brank) if reverse else brank)
    return rows


def _gla_low_levels(q, kk, lf, cum, lvl, bcast, halves, reverse):
    ng = q.shape[0] // 8
    qb, kb = q.astype(BF16), kk.astype(BF16)
    tiles = [jnp.where(lvl == 0, lax.dot_general(qb[hs], kb[hs], NT_DIMS, preferred_element_type=F32), 0.0)
             for hs in halves]
    row8 = lax.broadcasted_iota(jnp.int32, (8, HEAD_DIM), 0)
    rank8 = (7 - row8) if reverse else row8
    for level in range(1, 4):
        sgn8 = jnp.where(((rank8 >> (level - 1)) & 1) == 1, 1.0, -1.0)
        sgn = jnp.concatenate([sgn8] * ng, axis=0)
        later = sgn > 0.0
        if level == 1:
            g = jnp.where(later, lf, 0.0)
        else:
            brow = _boundary_rows(level, reverse)
            pieces = []
            for grp in range(ng):
                first = bcast(brow[8 * grp])
                pieces.append(jnp.where(row8 < 4, first, bcast(brow[8 * grp + 7])) if level == 2 else first)
            g = (cum - jnp.concatenate(pieces, axis=0)) * sgn
        xe = (jnp.where(later, q, kk) * jnp.exp(g)).astype(BF16)
        for h, hs in enumerate(halves):
            s = lax.dot_general(xe[hs], xe[hs], NT_DIMS, preferred_element_type=F32)
            tiles[h] = jnp.where(lvl == level, s, tiles[h])
    return tiles


def _gla_chunk(q_ref, v_ref, lf_ref, lanes, st_ref, cum_ref, tri_ref, lvl_ref, o_ref, reverse, merged):
    c = GLA_CHUNK
    hc, ng = c // 2, c // 8
    halves = (slice(0, hc), slice(hc, c))
    lf = lf_ref[0, :, lanes]
    q = q_ref[0, :, lanes].astype(F32)
    v = v_ref[0, :, lanes]
    kk = 1.0 - jnp.exp(lf)
    hi = lf.astype(BF16)
    lo = (lf - hi.astype(F32)).astype(BF16)
    two = jnp.dot(tri_ref[...], jnp.concatenate([hi, lo], axis=1), preferred_element_type=F32)
    cum = two[:, HEAD_DIM:] + two[:, :HEAD_DIM]
    cum_ref[...] = cum
    lvl = lvl_ref[...]
    bcast = lambda r: jnp.broadcast_to(cum_ref[r:r + 1, :], (8, HEAD_DIM))
    groups = lambda a: [a[8 * i:8 * i + 8] for i in range(a.shape[0] // 8)]

    if merged:
        blk = 1 << merged
        cache, pieces = {}, []
        for grp in range(ng):
            rank = (c - 1 - 8 * grp) if reverse else 8 * grp
            first = (rank // blk) * blk
            row = (c - 1 - first) if reverse else first
            pieces.append(cache.setdefault(row, bcast(row)))
        inside = jnp.logical_and(lvl >= 0, lvl <= merged)
        tiles = []
        for h, hs in enumerate(halves):
            d = cum[hs] - jnp.concatenate(pieces[h * ng // 2:(h + 1) * ng // 2], axis=0)
            xq = (q[hs] * jnp.exp(d)).astype(BF16)
            xk = (kk[hs] * jnp.exp(-d)).astype(BF16)
            tiles.append(jnp.where(inside, lax.dot_general(xq, xk, NT_DIMS, preferred_element_type=F32), 0.0))
    else:
        tiles = _gla_low_levels(q, kk, lf, cum, lvl, bcast, halves, reverse)

    tile_rows = [groups(t) for t in tiles]
    lvl_rows = groups(lvl)
    q_rows, k_rows, cum_rows = groups(q), groups(kk), groups(cum)
    for h in range(2):
        half_groups = range(h * ng // 2, (h + 1) * ng // 2)
        for level in range(max(4, merged + 1), GLA_LEVELS):
            brow = _boundary_rows(level, reverse)
            cache = {}
            g_rows, x_rows = [], []
            for grp in half_groups:
                cb = cache.setdefault(brow[8 * grp], bcast(brow[8 * grp]))
                later = _later_group(level, reverse, grp)
                g_rows.append(cum_rows[grp] - cb if later else cb - cum_rows[grp])
                x_rows.append(q_rows[grp] if later else k_rows[grp])
            xe = jnp.concatenate(x_rows, axis=0) * jnp.exp(jnp.concatenate(g_rows, axis=0))
            xe_rows = groups(xe)
            later_local = [grp - half_groups[0] for grp in half_groups if _later_group(level, reverse, grp)]
            qc = jnp.concatenate([xe_rows[i] for i in later_local], axis=0).astype(BF16)
            s = lax.dot_general(qc, xe.astype(BF16), NT_DIMS, preferred_element_type=F32)
            for i, local in enumerate(later_local):
                tile_rows[h][local] = jnp.where(lvl_rows[local] == level, s[8 * i:8 * i + 8], tile_rows[h][local])

    early, late = (1, 0) if reverse else (0, 1)
    cb = cum_ref[_boundary_rows(GLA_LEVELS, reverse)[0]:_boundary_rows(GLA_LEVELS, reverse)[0] + 1, :]
    ql = (q[halves[late]] * jnp.exp(cum[halves[late]] - cb)).astype(BF16)
    ke = (kk[halves[early]] * jnp.exp(cb - cum[halves[early]])).astype(BF16)
    cross = lax.dot_general(ql, ke, NT_DIMS, preferred_element_type=F32)
    t_a, t_b = (jnp.concatenate(rows, axis=0) for rows in tile_rows)
    zero = jnp.zeros((hc, hc), F32)
    if reverse:
        scores = jnp.concatenate([jnp.concatenate([t_a, cross], axis=1), jnp.concatenate([zero, t_b], axis=1)], axis=0)
    else:
        scores = jnp.concatenate([jnp.concatenate([t_a, zero], axis=1), jnp.concatenate([cross, t_b], axis=1)], axis=0)

    st = st_ref[...]
    qe = (q * jnp.exp(cum)).astype(BF16)
    o = jnp.dot(scores.astype(BF16), v, preferred_element_type=F32)
    o = o + lax.dot_general(qe, st.astype(BF16), NT_DIMS, preferred_element_type=F32)
    o_ref[0, :, lanes] = o.astype(o_ref.dtype)
    last_row = 0 if reverse else c - 1
    last = cum_ref[last_row:last_row + 1, :]
    ke_all = (kk * jnp.exp(last - cum)).astype(BF16)
    st_ref[...] = st * jnp.exp(last) + lax.dot_general(v, ke_all, TN_DIMS, preferred_element_type=F32)


def _gla_kernel(qf_ref, vf_ref, lff_ref, qb_ref, vb_ref, lfb_ref, s0f_ref, s0b_ref,
                trif_ref, lvlf_ref, trib_ref, lvlb_ref,
                of_ref, ob_ref, sff_ref, sfb_ref, stf_ref, stb_ref, cumf_ref, cumb_ref, *, merged):
    j = pl.program_id(2)

    @pl.when(j == 0)
    def _():
        stf_ref[...] = s0f_ref[0]
        stb_ref[...] = s0b_ref[0]

    for k in range(stf_ref.shape[0]):
        lanes = slice(k * HEAD_DIM, (k + 1) * HEAD_DIM)
        _gla_chunk(qf_ref, vf_ref, lff_ref, lanes, stf_ref.at[k], cumf_ref.at[k], trif_ref, lvlf_ref, of_ref,
                   False, merged)
        _gla_chunk(qb_ref, vb_ref, lfb_ref, lanes, stb_ref.at[k], cumb_ref.at[k], trib_ref, lvlb_ref, ob_ref,
                   True, merged)

    @pl.when(j == pl.num_programs(2) - 1)
    def _():
        sff_ref[0] = stf_ref[...]
        sfb_ref[0] = stb_ref[...]


def _gla_bidir(q, v, lf_f, lf_b, s0f, s0b, lb):
    worst = (2 ** GLA_MERGED - 1) * jnp.max(-jnp.log(lb))
    run = lambda merged: (lambda *a: _gla_call(*a, merged=merged))
    return lax.cond(worst < GLA_MAX_EXPONENT, run(GLA_MERGED), run(0), q, v, lf_f, lf_b, s0f, s0b)


def _gla_call(q, v, lf_f, lf_b, s0f, s0b, *, merged):
    b_, t_, d_ = q.shape
    h_ = d_ // HEAD_DIM
    hp = math.gcd(GLA_HEADS_PER_STEP, h_)
    c = GLA_CHUNK
    n = t_ // c
    fwd = lambda b, h, j: (b, j, h)
    bwd = lambda b, h, j: (b, n - 1 - j, h)
    st = lambda b, h, j: (b, h, 0, 0)
    const = lambda b, h, j: (0, 0)
    blk = lambda im: pl.BlockSpec((1, c, hp * HEAD_DIM), im)
    st_spec = pl.BlockSpec((1, hp, HEAD_DIM, HEAD_DIM), st)
    cspecs = [pl.BlockSpec((c, c), const), pl.BlockSpec((c // 2, c // 2), const)]
    state = pltpu.VMEM((hp, HEAD_DIM, HEAD_DIM), F32)
    cum = pltpu.VMEM((hp, c, HEAD_DIM), F32)
    return pl.pallas_call(
        functools.partial(_gla_kernel, merged=merged),
        grid=(b_, h_ // hp, n),
        in_specs=[blk(fwd), blk(fwd), blk(fwd), blk(bwd), blk(bwd), blk(bwd), st_spec, st_spec] + cspecs + cspecs,
        out_specs=[blk(fwd), blk(bwd), st_spec, st_spec],
        out_shape=[jax.ShapeDtypeStruct((b_, t_, d_), BF16)] * 2
                  + [jax.ShapeDtypeStruct((b_, h_, HEAD_DIM, HEAD_DIM), F32)] * 2,
        scratch_shapes=[state, state, cum, cum],
        compiler_params=_params("arbitrary", "arbitrary", "arbitrary"),
        name="gla_merged" if merged else "gla_split",
    )(q, v, lf_f, q, v, lf_b, s0f, s0b, *_gla_consts(False), *_gla_consts(True))


def _mixer_epilogue(y, rows, x_ref, w_ref, gate_ref, gain_ref, shift_ref, scale_ref, wr_ref, x1_ref, h_ref, aff_ref):
    y = jnp.dot(y.astype(BF16), w_ref[...], preferred_element_type=F32)
    x1 = x_ref[0, rows] + gate_ref[0] * y
    x1_ref[0, rows] = x1
    hf = _modulate(x1, gain_ref[...], shift_ref[0], scale_ref[0])
    h_ref[0, rows] = hf.astype(BF16)
    logits = _dot_bf16x3(wr_ref[...], hf, NT_DIMS)
    e = jnp.exp(logits - jnp.max(logits, axis=0, keepdims=True))
    aff_ref[0, :, rows] = e / jnp.sum(e, axis=0, keepdims=True)


def _hgrn_out_kernel(of_ref, ob_ref, g_ref, x_ref, hnorm_ref, w_ref, *rest):
    hn = hnorm_ref[...]
    for rows in _sub_tiles(x_ref.shape[1]):
        o = of_ref[0, rows].astype(F32) + ob_ref[0, rows].astype(F32)
        heads = []
        for h in range(o.shape[-1] // HEAD_DIM):
            oh = o[:, h * HEAD_DIM:(h + 1) * HEAD_DIM]
            heads.append(oh * lax.rsqrt(jnp.mean(oh * oh, axis=-1, keepdims=True) + EPS) * hn)
        y = jnp.concatenate(heads, axis=-1) * _silu(g_ref[0, rows].astype(F32))
        _mixer_epilogue(y, rows, x_ref, w_ref, *rest)


def _conv_mixer_kernel(prev_ref, next_ref, x_ref, shift1_ref, scale1_ref, gain1_ref, win_ref, wc_ref, w_ref, *rest):
    i = pl.program_id(1)
    tm, d_ = x_ref.shape[1], x_ref.shape[2]
    subs = _sub_tiles(tm)
    halo = jnp.concatenate([prev_ref[0], next_ref[0]], axis=0)
    b_gate, cu = [], []
    for k, rows in enumerate(subs):
        xs = x_ref[0, rows]
        if k == 0:
            xs = jnp.concatenate([xs, halo], axis=0)
        h = _modulate(xs, gain1_ref[...], shift1_ref[0], scale1_ref[0]).astype(BF16)
        part = lambda p, hh: jnp.dot(hh, win_ref[:, p * d_:(p + 1) * d_], preferred_element_type=F32)
        n = rows.stop - rows.start
        b_gate.append(part(0, h[:n]))
        cu.append(part(1, h) * part(2, h))
    cu_halo = cu[0][subs[0].stop - subs[0].start:]
    cu = jnp.concatenate([cu[0][:subs[0].stop - subs[0].start]] + cu[1:], axis=0)
    rid = lax.broadcasted_iota(jnp.int32, cu.shape, 0)
    before = jnp.where(i == 0, 0.0, cu_halo[7:8])
    after = jnp.where(i == pl.num_programs(1) - 1, 0.0, cu_halo[8:9])
    left = jnp.where(rid == 0, before, pltpu.roll(cu, 1, 0))
    right = jnp.where(rid == tm - 1, after, pltpu.roll(cu, tm - 1, 0))
    wc = wc_ref[...]
    conv = left * wc[0:1] + cu * wc[1:2] + right * wc[2:3]
    for k, rows in enumerate(subs):
        _mixer_epilogue(b_gate[k] * conv[rows], rows, x_ref, w_ref, *rest)


def _mixer_out(kernel, name, row_args, halo_args, x, sample_args, small_args, gate, gain, shift, scale, w_router_t):
    b_, t_, d_ = x.shape
    e_ = w_router_t.shape[0]
    tm = min(MIXER_ROW_TILE, t_)
    row = pl.BlockSpec((1, tm, d_), lambda b, i: (b, i, 0))
    per_sample = pl.BlockSpec((1, 1, d_), lambda b, i: (b, 0, 0))
    whole = lambda a: pl.BlockSpec(a.shape, lambda b, i: (0,) * a.ndim)
    n8 = t_ // 8
    halo_specs = [pl.BlockSpec((1, 8, d_), lambda b, i: (b, jnp.maximum(i * (tm // 8) - 1, 0), 0)),
                  pl.BlockSpec((1, 8, d_), lambda b, i: (b, jnp.minimum((i + 1) * (tm // 8), n8 - 1), 0))]
    return pl.pallas_call(
        kernel,
        grid=(b_, t_ // tm),
        in_specs=[row] * len(row_args) + halo_specs[:len(halo_args)] + [row] + [per_sample] * len(sample_args)
                 + [whole(a) for a in small_args] + [per_sample, whole(gain), per_sample, per_sample, whole(w_router_t)],
        out_specs=[row, row, pl.BlockSpec((1, e_, tm), lambda b, i: (b, 0, i))],
        out_shape=[jax.ShapeDtypeStruct((b_, t_, d_), F32), jax.ShapeDtypeStruct((b_, t_, d_), BF16),
                   jax.ShapeDtypeStruct((b_, e_, t_), F32)],
        compiler_params=_params("arbitrary", "arbitrary", vmem=V7X_VMEM_LIMIT),
        name=name,
    )(*row_args, *halo_args, x, *sample_args, *small_args, gate, gain, shift, scale, w_router_t)


def _route_kernel(aff_ref, tri_ref, blockind_ref, slot_ref, base_ref, dense_ref, *, cap, n_experts):
    aff = aff_ref[...]
    e_, t_ = aff.shape

    def as_float(word):
        return pltpu.bitcast(word, F32)

    def count_ge(th):
        return jnp.sum(jnp.where(aff >= th, 1.0, 0.0), axis=1, keepdims=True)

    def search(_, carry):
        lo, hi = carry
        mid = lo + ((hi - lo + 1) >> 1)
        ok = count_ge(as_float(mid)) >= cap
        return jnp.where(ok, mid, lo), jnp.where(ok, hi, mid - 1)

    lo0 = jnp.zeros((e_, 1), jnp.int32)
    hi0 = jnp.full((e_, 1), 0x7F7FFFFF, jnp.int32)
    kth, _ = lax.fori_loop(0, 32, search, (lo0, hi0))
    above = aff >= as_float(kth + 1)
    tied = jnp.logical_and(aff >= as_float(kth), jnp.logical_not(above))
    need = cap - jnp.sum(jnp.where(above, 1.0, 0.0), axis=1, keepdims=True)
    tri = tri_ref[...]
    tb = tri.shape[0]
    carry_t = jnp.zeros((e_, 1), F32)
    carry_s = jnp.zeros((e_, 1), F32)
    sel_blocks = []
    for j in range(t_ // tb):
        cols = slice(j * tb, (j + 1) * tb)
        tied_j = tied[:, cols]
        ct = jnp.dot(jnp.where(tied_j, 1.0, 0.0).astype(BF16), tri, preferred_element_type=F32) + carry_t
        carry_t = ct[:, tb - 1:tb]
        sel_j = jnp.where(above[:, cols], 1.0, jnp.where(tied_j & (ct <= need), 1.0, 0.0))
        cs = jnp.dot(sel_j.astype(BF16), tri, preferred_element_type=F32) + carry_s
        carry_s = cs[:, tb - 1:tb]
        slot_ref[:, cols] = jnp.where(sel_j > 0.0, cs - 1.0, -1.0).astype(jnp.int32)
        sel_blocks.append(sel_j.astype(BF16))
    sel = jnp.concatenate(sel_blocks, axis=1)
    counts = jnp.dot(sel, blockind_ref[...], preferred_element_type=F32)
    base, end = counts[:, :128], counts[:, 128:]
    base_ref[...] = base.astype(jnp.int32)
    span = end - jnp.floor(base * (1.0 / SLOT_ALIGN)) * SLOT_ALIGN
    for b in range(e_ // n_experts):
        widest = jnp.max(span[b * n_experts:(b + 1) * n_experts], axis=0, keepdims=True)
        dense_ref[b] = sum(jnp.where(widest <= w, 1, 0) for w in DENSE_WINDOWS).astype(jnp.int32)


def _route(aff):
    b_, e_, t_ = aff.shape
    cap = EC_CAPACITY_FACTOR * t_ // e_
    tb = TOKEN_BLOCK
    nb = t_ // tb
    tri = jnp.asarray(np.triu(np.ones((tb, tb), np.float32)), BF16)
    tok, col = np.arange(t_)[:, None], np.arange(128)[None, :]
    blockind = np.concatenate([(tok < col * tb) & (col <= nb), (tok < (col + 1) * tb) & (col < nb)], axis=1)
    slot, base, dense = pl.pallas_call(
        functools.partial(_route_kernel, cap=cap, n_experts=e_),
        out_shape=[jax.ShapeDtypeStruct((b_ * e_, t_), jnp.int32), jax.ShapeDtypeStruct((b_ * e_, 128), jnp.int32),
                   jax.ShapeDtypeStruct((b_, 1, 128), jnp.int32)],
        name="route",
    )(aff.reshape(b_ * e_, t_), tri, jnp.asarray(blockind.astype(np.float32), BF16))
    return slot.reshape(b_, e_, t_), base[:, :nb + 1].reshape(-1), dense[:, 0, :nb].reshape(-1)


def _window_plan(tbl_ref, b, e, tb, ne, nb):
    idx = (b * ne + e) * (nb + 1) + tb
    base, end = tbl_ref[idx], tbl_ref[idx + 1]
    start = _align_down(base)
    n_win = jnp.where(end > base, (end - start + SLOT_WINDOW - 1) >> (SLOT_WINDOW.bit_length() - 1), 0)
    return start, n_win


def _align_down(slot):
    shift = SLOT_ALIGN.bit_length() - 1
    return (slot >> shift) << shift


def _dense_start(tbl_ref, b, e, tb, ne, nb, cap, window):
    base = tbl_ref[(b * ne + e) * (nb + 1) + tb]
    return pl.multiple_of(jnp.minimum(_align_down(base), cap - window), SLOT_ALIGN)


def _gather_kernel(tbl_ref, dense_ref, h_ref, slot_ref, xg_ref, *, ne, nb, cap):
    b, step = pl.program_id(0), pl.program_id(2)
    n_tok = TOKEN_BLOCK

    @pl.when(step == 0)
    def _():
        xg_ref[...] = jnp.zeros(xg_ref.shape, xg_ref.dtype)

    def dense_block(sub, window):
        tb = step * BLOCKS_PER_STEP + sub
        toks = slice(sub * n_tok, (sub + 1) * n_tok)
        starts = [_dense_start(tbl_ref, b, e, tb, ne, nb, cap, window) for e in range(ne)]
        ids = lax.broadcasted_iota(jnp.int32, (window, n_tok), 0)
        onehot = jnp.concatenate(
            [jnp.where(ids == slot_ref[0, e:e + 1, toks] - starts[e], 1.0, 0.0).astype(BF16) for e in range(ne)],
            axis=0)
        rows = jnp.dot(onehot, h_ref[0, toks, :], preferred_element_type=F32).astype(BF16)
        for e in range(ne):
            win = xg_ref.at[0, e, pl.ds(starts[e], window), :]
            win[...] = win[...] + rows[e * window:(e + 1) * window]

    def windowed_block(sub):
        tb = step * BLOCKS_PER_STEP + sub
        toks = slice(sub * n_tok, (sub + 1) * n_tok)
        h = h_ref[0, toks, :]
        ids0 = lax.broadcasted_iota(jnp.int32, (SLOT_WINDOW, n_tok), 0)
        for e in range(ne):
            srow = slot_ref[0, e:e + 1, toks]
            start, n_win = _window_plan(tbl_ref, b, e, tb, ne, nb)

            def body(k, carry, e=e, srow=srow, start=start):
                lo = start + k * SLOT_WINDOW
                w0 = pl.multiple_of(jnp.minimum(lo, cap - SLOT_WINDOW), SLOT_ALIGN)
                ids = w0 + ids0
                onehot = jnp.where(ids >= lo, jnp.where(ids == srow, 1.0, 0.0), 0.0).astype(BF16)
                rows = jnp.dot(onehot, h, preferred_element_type=F32)
                win = xg_ref.at[0, e, pl.ds(w0, SLOT_WINDOW), :]
                win[...] = win[...] + rows.astype(BF16)
                return carry

            lax.fori_loop(0, n_win, body, 0)

    _per_block_dispatch(dense_ref, b * nb + step * BLOCKS_PER_STEP, BLOCKS_PER_STEP, dense_block, windowed_block)


def _per_block_dispatch(dense_ref, first, n_blocks, dense_block, windowed_block):
    fits = [dense_ref[first + sub] for sub in range(n_blocks)]
    common = functools.reduce(jnp.minimum, fits)
    n_tiers = len(DENSE_WINDOWS)
    for k, window in enumerate(DENSE_WINDOWS):
        chosen = (common >= n_tiers) if k == 0 else (common == n_tiers - k)

        @pl.when(chosen)
        def _(window=window):
            for sub in range(n_blocks):
                dense_block(sub, window)

    @pl.when(common == 0)
    def _():
        for sub in range(n_blocks):
            pl.when(fits[sub] > 0)(functools.partial(dense_block, sub, DENSE_WINDOWS[-1]))
            pl.when(fits[sub] == 0)(functools.partial(windowed_block, sub))


def _gather(h, slot, tbl, dense):
    b_, t_, d_ = h.shape
    e_ = slot.shape[1]
    cap = EC_CAPACITY_FACTOR * t_ // e_
    tb = TOKEN_BLOCK * BLOCKS_PER_STEP
    nb = t_ // TOKEN_BLOCK
    dh = d_ // 2
    return pl.pallas_call(
        functools.partial(_gather_kernel, ne=e_, nb=nb, cap=cap),
        grid_spec=pltpu.PrefetchScalarGridSpec(
            num_scalar_prefetch=2,
            grid=(b_, 2, t_ // tb),
            in_specs=[pl.BlockSpec((1, tb, dh), lambda b, c, i, *_: (b, i, c)),
                      pl.BlockSpec((1, e_, tb), lambda b, c, i, *_: (b, 0, i))],
            out_specs=pl.BlockSpec((1, e_, cap, dh), lambda b, c, i, *_: (b, 0, 0, c)),
        ),
        out_shape=jax.ShapeDtypeStruct((b_, e_, cap, d_), BF16),
        compiler_params=_params("arbitrary", "arbitrary", "arbitrary", vmem=V7X_VMEM_LIMIT),
        name="moe_gather",
    )(tbl, dense, h, slot)


def _expert_kernel(xg_ref, wg_ref, wu_ref, wd_ref, y_ref, acc_ref):
    f = pl.program_id(1)
    n_b, _, cap, _ = xg_ref.shape
    wg = wg_ref[0, 0].astype(BF16)
    wu = wu_ref[0, 0].astype(BF16)
    wd = wd_ref[0, 0].astype(BF16)

    @pl.when(jnp.logical_and(pl.program_id(0) == 0, f == 0))
    def _():
        acc_ref[...] = jnp.zeros(acc_ref.shape, F32)

    for b in range(n_b):
        for r in range(cap // FFN_ROWS):
            rows = pl.ds(r * FFN_ROWS, FFN_ROWS)
            acc_rows = pl.ds((b * cap) + r * FFN_ROWS, FFN_ROWS)
            xr = xg_ref[b, 0, rows, :]
            a = jnp.dot(xr, wg, preferred_element_type=F32)
            u = jnp.dot(xr, wu, preferred_element_type=F32)
            part = jnp.dot((_silu(a) * u).astype(BF16), wd, preferred_element_type=F32)
            total = jnp.where(f == 0, 0.0, acc_ref[acc_rows, :]) + part
            acc_ref[acc_rows, :] = total
            y_ref[b, 0, rows, :] = total.astype(BF16)


def _experts(xg, layer, w_gate, w_up, w_down):
    b_, e_, cap, d_ = xg.shape
    f_ = w_gate.shape[-1]
    ft = min(FFN_TILE, f_)
    return pl.pallas_call(
        _expert_kernel,
        grid=(e_, f_ // ft),
        in_specs=[pl.BlockSpec((b_, 1, cap, d_), lambda e, f: (0, e, 0, 0)),
                  pl.BlockSpec((1, 1, d_, ft), lambda e, f: (layer, e, 0, f)),
                  pl.BlockSpec((1, 1, d_, ft), lambda e, f: (layer, e, 0, f)),
                  pl.BlockSpec((1, 1, ft, d_), lambda e, f: (layer, e, f, 0))],
        out_specs=pl.BlockSpec((b_, 1, cap, d_), lambda e, f: (0, e, 0, 0)),
        out_shape=jax.ShapeDtypeStruct((b_, e_, cap, d_), BF16),
        scratch_shapes=[pltpu.VMEM((b_ * cap, d_), F32)],
        compiler_params=_params("arbitrary", "arbitrary", vmem=V7X_VMEM_LIMIT),
        name="moe_experts",
    )(xg, w_gate, w_up, w_down)


def _combine_kernel(tbl_ref, dense_ref, y_ref, slot_ref, aff_ref, x_ref, gate_ref, *rest, ne, nb, cap, final):
    if final:
        gain_ref, out_ref, acc_ref = rest
    else:
        out_ref, acc_ref = rest
    b, step = pl.program_id(0), pl.program_id(1)
    n_tok = TOKEN_BLOCK

    def finish(toks, moe):
        out = x_ref[0, toks, :] + gate_ref[0] * moe
        if final:
            out = out * lax.rsqrt(jnp.mean(out * out, axis=-1, keepdims=True) + EPS) * gain_ref[...]
        out_ref[0, toks, :] = out

    def dense_block(sub, window):
        tb = step * COMBINE_BLOCKS + sub
        toks = slice(sub * n_tok, (sub + 1) * n_tok)
        starts = [_dense_start(tbl_ref, b, e, tb, ne, nb, cap, window) for e in range(ne)]
        ids = lax.broadcasted_iota(jnp.int32, (window, n_tok), 0)
        weights = jnp.concatenate(
            [jnp.where(ids == slot_ref[0, e:e + 1, toks] - starts[e], aff_ref[0, e:e + 1, toks], 0.0).astype(BF16)
             for e in range(ne)], axis=0)
        yw = jnp.concatenate([y_ref[0, e, pl.ds(starts[e], window), :] for e in range(ne)], axis=0)
        finish(toks, lax.dot_general(weights, yw, TN_DIMS, preferred_element_type=F32))

    def windowed_block(sub):
        tb = step * COMBINE_BLOCKS + sub
        toks = slice(sub * n_tok, (sub + 1) * n_tok)
        ids0 = lax.broadcasted_iota(jnp.int32, (SLOT_WINDOW, n_tok), 0)
        acc_ref[...] = jnp.zeros(acc_ref.shape, F32)
        for e in range(ne):
            srow = slot_ref[0, e:e + 1, toks]
            grow = aff_ref[0, e:e + 1, toks]
            start, n_win = _window_plan(tbl_ref, b, e, tb, ne, nb)

            def body(k, carry, e=e, srow=srow, grow=grow, start=start):
                lo = start + k * SLOT_WINDOW
                w0 = pl.multiple_of(jnp.minimum(lo, cap - SLOT_WINDOW), SLOT_ALIGN)
                ids = w0 + ids0
                weights = jnp.where(ids >= lo, jnp.where(ids == srow, grow, 0.0), 0.0).astype(BF16)
                yw = y_ref[0, e, pl.ds(w0, SLOT_WINDOW), :]
                acc_ref[...] = acc_ref[...] + lax.dot_general(weights, yw, TN_DIMS, preferred_element_type=F32)
                return carry

            lax.fori_loop(0, n_win, body, 0)
        finish(toks, acc_ref[...])

    _per_block_dispatch(dense_ref, b * nb + step * COMBINE_BLOCKS, COMBINE_BLOCKS, dense_block, windowed_block)


def _combine(y, slot, aff, tbl, dense, x, gate, final_gain):
    b_, t_, d_ = x.shape
    e_, cap = y.shape[1], y.shape[2]
    tb = TOKEN_BLOCK * COMBINE_BLOCKS
    nb = t_ // TOKEN_BLOCK
    final = final_gain is not None
    route_spec = pl.BlockSpec((1, e_, tb), lambda b, i, *_: (b, 0, i))
    row = pl.BlockSpec((1, tb, d_), lambda b, i, *_: (b, i, 0))
    in_specs = [pl.BlockSpec((1, e_, cap, d_), lambda b, i, *_: (b, 0, 0, 0), pipeline_mode=pl.Buffered(1)),
                route_spec, route_spec, row, pl.BlockSpec((1, 1, d_), lambda b, i, *_: (b, 0, 0))]
    args = [tbl, dense, y, slot, aff, x, gate]
    if final:
        in_specs.append(pl.BlockSpec((1, d_), lambda b, i, *_: (0, 0)))
        args.append(final_gain)
    return pl.pallas_call(
        functools.partial(_combine_kernel, ne=e_, nb=nb, cap=cap, final=final),
        grid_spec=pltpu.PrefetchScalarGridSpec(
            num_scalar_prefetch=2,
            grid=(b_, t_ // tb),
            in_specs=in_specs,
            out_specs=row,
            scratch_shapes=[pltpu.VMEM((TOKEN_BLOCK, d_), F32)],
        ),
        out_shape=jax.ShapeDtypeStruct((b_, t_, d_), F32),
        compiler_params=_params("arbitrary", "arbitrary", vmem=V7X_VMEM_LIMIT),
        name="moe_combine_final" if final else "moe_combine",
    )(*args)


def _moe(x, h, aff, gate, layer, w_gate, w_up, w_down, final_gain=None):
    slot, tbl, dense = _route(aff)
    xg = _gather(h, slot, tbl, dense)
    y = _experts(xg, layer, w_gate, w_up, w_down)
    return _combine(y, slot, aff, tbl, dense, x, gate, final_gain)


def kernel(x, c, ctx, c_ctx, ada_w, ada_b, norm_mix, norm_ffn, norm_final, hg_w_in, hg_lb_logits, hg_norm, hg_w_out, sc_w_in, sc_conv, sc_w_out, moe_router, moe_w_gate, moe_w_up, moe_w_down):
    b_, t_, d_ = x.shape
    depth = ada_w.shape[0]
    n_ada = ada_w.shape[-1] // d_
    n_heads = d_ // HEAD_DIM
    n_experts = moe_router.shape[-1]
    cap = EC_CAPACITY_FACTOR * t_ // n_experts
    assert depth == 2 and n_ada == 6 and b_ + 1 <= 8
    assert t_ % GLA_CHUNK == 0 and ctx.shape[1] % GLA_CHUNK == 0 and SUB_ROWS % GRID_W == 0
    assert d_ % (2 * HEAD_DIM) == 0 and t_ % min(ROW_TILE, t_) == 0 and t_ % min(MIXER_ROW_TILE, t_) == 0
    assert t_ % (TOKEN_BLOCK * BLOCKS_PER_STEP) == 0 and t_ % (TOKEN_BLOCK * COMBINE_BLOCKS) == 0
    assert cap % SLOT_WINDOW == 0 and cap >= max(DENSE_WINDOWS) and cap % min(FFN_ROWS, cap) == 0
    assert moe_w_gate.shape[-1] % min(FFN_TILE, moe_w_gate.shape[-1]) == 0

    cond = jnp.concatenate([c, c_ctx[None], jnp.zeros((8 - b_ - 1, d_), F32)], axis=0)
    mod = _ada_vectors(cond, ada_w, ada_b, n_ada)
    vec = lambda i, j: mod[i, j, :b_][:, None, :]
    cvec = lambda i, j: mod[i, j, b_][None, None, :]
    rowtab, coltab, lower = _tables(hg_lb_logits, t_)
    row_of = lambda a, i: a[i][None, :]
    router_t = lambda i: jnp.swapaxes(moe_router[i], 0, 1)

    w_in = hg_w_in[0].astype(BF16)
    lb0 = row_of(lower, 0)
    gain0 = row_of(norm_mix, 0)
    qc, vc, lfc_f, lfc_b = _hgrn_in(ctx, None, gain0, cvec(0, 0), cvec(0, 1), lb0, w_in, with_gate=False)
    zeros = jnp.zeros((b_, n_heads, HEAD_DIM, HEAD_DIM), F32)
    _, _, s_f, s_b = _gla_bidir(qc, vc, lfc_f, lfc_b, zeros, zeros, lb0)
    x0, q, v, lf_f, lf_b, g = _hgrn_in(x, (rowtab, coltab), gain0, vec(0, 0), vec(0, 1), lb0, w_in, with_gate=True)
    o_f, o_b, _, _ = _gla_bidir(q, v, lf_f, lf_b, s_f, s_b, lb0)
    x1, h, aff = _mixer_out(_hgrn_out_kernel, "hgrn_out", [o_f, o_b, g], [], x0, [],
                            [row_of(hg_norm, 0), hg_w_out[0].astype(BF16)],
                            vec(0, 2), row_of(norm_ffn, 0), vec(0, 3), vec(0, 4), router_t(0))
    x2 = _moe(x1, h, aff, vec(0, 5), 0, moe_w_gate, moe_w_up, moe_w_down)

    x3, h, aff = _mixer_out(_conv_mixer_kernel, "conv_mixer", [], [x2, x2], x2, [vec(1, 0), vec(1, 1)],
                            [row_of(norm_mix, 1), sc_w_in[0].astype(BF16), sc_conv[0], sc_w_out[0].astype(BF16)],
                            vec(1, 2), row_of(norm_ffn, 1), vec(1, 3), vec(1, 4), router_t(1))
    return _moe(x3, h, aff, vec(1, 5), 1, moe_w_gate, moe_w_up, moe_w_down, final_gain=norm_final[None, :])
```

```python
import functools
import math

import numpy as np
import jax
import jax.numpy as jnp
from jax import lax
from jax.experimental import pallas as pl
from jax.experimental.pallas import tpu as pltpu

F32 = jnp.float32
BF16 = jnp.bfloat16

EPS = 1e-6
POS_TEMP = 10000.0
GRID_W = 64
HEAD_DIM = 128
EC_CAPACITY_FACTOR = 2
GLA_CHUNK = 256
GLA_LEVELS = 8
GLA_HEADS_PER_STEP = 8
GLA_MERGED = 5
GLA_MAX_EXPONENT = 80.0
ROW_TILE = 512
MIXER_ROW_TILE = 1024
SUB_ROWS = 1024
IN_PROJ_SUB_ROWS = 128
TOKEN_BLOCK = 256
BLOCKS_PER_STEP = 4
COMBINE_BLOCKS = 2
SLOT_WINDOW = 64
DENSE_WINDOWS = (96, 128)
SLOT_ALIGN = 16
FFN_TILE = 1024
FFN_ROWS = 256
V7X_VMEM_LIMIT = 56 * 1024 * 1024

NT_DIMS = (((1,), (1,)), ((), ()))
TN_DIMS = (((0,), (0,)), ((), ()))


def _params(*sem, vmem=None):
    return pltpu.CompilerParams(dimension_semantics=sem, vmem_limit_bytes=vmem)


def _dot_bf16x3(a, b, dims):
    a0, b0 = a.astype(BF16), b.astype(BF16)
    a1 = (a - a0.astype(F32)).astype(BF16)
    b1 = (b - b0.astype(F32)).astype(BF16)
    d = lambda x, y: lax.dot_general(x, y, dims, preferred_element_type=F32)
    return (d(a0, b1) + d(a1, b0)) + d(a0, b0)


def _sub_tiles(n_rows, sub_rows=SUB_ROWS):
    sub = min(sub_rows, n_rows)
    return [slice(s, s + sub) for s in range(0, n_rows, sub)]


def _sigmoid(x):
    return 1.0 / (1.0 + jnp.exp(-x))


def _silu(x):
    return x * _sigmoid(x)


def _modulate(x, gain, shift, scale):
    y = x * lax.rsqrt(jnp.mean(x * x, axis=-1, keepdims=True) + EPS)
    return y * (gain * (1.0 + scale)) + shift


def _ada_kernel(cond_ref, w_ref, b_ref, out_ref):
    s = _silu(cond_ref[...])
    out_ref[0, 0] = _dot_bf16x3(s, w_ref[0], (((1,), (0,)), ((), ()))) + b_ref[0, 0]


def _ada_vectors(cond, ada_w, ada_b, n_ada):
    depth, d_, _ = ada_w.shape
    return pl.pallas_call(
        _ada_kernel,
        grid=(depth, n_ada),
        in_specs=[pl.BlockSpec((8, d_), lambda i, j: (0, 0)),
                  pl.BlockSpec((1, d_, d_), lambda i, j: (i, 0, j)),
                  pl.BlockSpec((1, 1, 1, d_), lambda i, j: (i, j, 0, 0))],
        out_specs=pl.BlockSpec((1, 1, 8, d_), lambda i, j: (i, j, 0, 0)),
        out_shape=jax.ShapeDtypeStruct((depth, n_ada, 8, d_), F32),
        compiler_params=_params("arbitrary", "arbitrary"),
        name="ada_vectors",
    )(cond, ada_w, ada_b.reshape(depth, n_ada, 1, d_))


def _tables_kernel(lb_logits_ref, rowtab_ref, coltab_ref, lb_ref, *, n_freq):
    def table(n_pos):
        p = lax.broadcasted_iota(jnp.int32, (n_pos, n_freq), 0).astype(F32)
        j = lax.broadcasted_iota(jnp.int32, (n_pos, n_freq), 1).astype(F32)
        omega = jnp.exp(j * (-math.log(POS_TEMP) / n_freq))
        ang = p * omega
        return jnp.concatenate([jnp.sin(ang), jnp.cos(ang)], axis=-1)

    rowtab_ref[...] = table(rowtab_ref.shape[0])
    coltab_ref[...] = table(coltab_ref.shape[0])
    logits = lb_logits_ref[...]
    e = jnp.exp(logits - jnp.max(logits, axis=0, keepdims=True))
    sm = e / jnp.sum(e, axis=0, keepdims=True)
    acc = sm[0:1]
    lb_ref[0:1] = acc
    for i in range(1, lb_ref.shape[0]):
        acc = acc + sm[i:i + 1]
        lb_ref[i:i + 1] = acc


def _tables(lb_logits, n_tokens):
    n_lb, d_ = lb_logits.shape
    n_freq = d_ // 4
    rows = n_tokens // GRID_W
    return pl.pallas_call(
        functools.partial(_tables_kernel, n_freq=n_freq),
        out_shape=[jax.ShapeDtypeStruct((rows, 2 * n_freq), F32),
                   jax.ShapeDtypeStruct((GRID_W, 2 * n_freq), F32),
                   jax.ShapeDtypeStruct((n_lb, d_), F32)],
        name="pos_tables",
    )(lb_logits)


def _hgrn_in_kernel(*refs, with_pos, with_gate):
    it = iter(refs)
    x_ref = next(it)
    if with_pos:
        rowtab_ref, coltab_ref = next(it), next(it)
    gain_ref, shift_ref, scale_ref, lb_ref, w_ref = next(it), next(it), next(it), next(it), next(it)
    if with_pos:
        x0_ref = next(it)
    q_ref, v_ref, lff_ref, lfb_ref = next(it), next(it), next(it), next(it)
    g_ref = next(it) if with_gate else None

    d_ = x_ref.shape[-1]
    lb = lb_ref[...]
    for rows in _sub_tiles(x_ref.shape[1], IN_PROJ_SUB_ROWS):
        x = x_ref[0, rows]
        if with_pos:
            grid_rows = range(rows.start // GRID_W, rows.stop // GRID_W)
            pos_row = jnp.concatenate(
                [jnp.broadcast_to(rowtab_ref[0, r:r + 1, :], (GRID_W, rowtab_ref.shape[-1])) for r in grid_rows], axis=0)
            pos_col = jnp.concatenate([coltab_ref[...]] * len(grid_rows), axis=0)
            x = x + jnp.concatenate([pos_row, pos_col], axis=-1)
            x0_ref[0, rows] = x
        h = _modulate(x, gain_ref[...], shift_ref[0], scale_ref[0]).astype(BF16)
        part = lambda p: jnp.dot(h, w_ref[:, p * d_:(p + 1) * d_], preferred_element_type=F32)
        lff_ref[0, rows] = jnp.log(lb + (1.0 - lb) * _sigmoid(part(2)))
        lfb_ref[0, rows] = jnp.log(lb + (1.0 - lb) * _sigmoid(part(3)))
        q_ref[0, rows] = (part(0) * HEAD_DIM ** -0.5).astype(BF16)
        v_ref[0, rows] = part(1).astype(BF16)
        if with_gate:
            g_ref[0, rows] = part(4).astype(BF16)


def _hgrn_in(x, tabs, gain, shift, scale, lb, w_in, *, with_gate):
    b_, t_, d_ = x.shape
    tm = min(ROW_TILE, t_)
    with_pos = tabs is not None
    per_sample = lambda a: pl.BlockSpec((1, 1, d_), (lambda b, i: (b, 0, 0)) if a.shape[0] > 1 else (lambda b, i: (0, 0, 0)))
    row = pl.BlockSpec((1, tm, d_), lambda b, i: (b, i, 0))
    vec = pl.BlockSpec((1, d_), lambda b, i: (0, 0))
    args, in_specs = [x], [row]
    if with_pos:
        rowtab, coltab = tabs
        rows_per_tile = tm // GRID_W
        args += [rowtab.reshape(rowtab.shape[0] // rows_per_tile, rows_per_tile, rowtab.shape[1]), coltab]
        in_specs += [pl.BlockSpec((1, rows_per_tile, rowtab.shape[1]), lambda b, i: (i, 0, 0)),
                     pl.BlockSpec(coltab.shape, lambda b, i: (0, 0))]
    args += [gain, shift, scale, lb, w_in]
    in_specs += [vec, per_sample(shift), per_sample(scale), vec, pl.BlockSpec(w_in.shape, lambda b, i: (0, 0))]
    out_shape, out_specs = [], []
    if with_pos:
        out_shape.append(jax.ShapeDtypeStruct((b_, t_, d_), F32))
        out_specs.append(row)
    out_shape += [jax.ShapeDtypeStruct((b_, t_, d_), BF16)] * 2 + [jax.ShapeDtypeStruct((b_, t_, d_), F32)] * 2
    out_specs += [row] * 4
    if with_gate:
        out_shape.append(jax.ShapeDtypeStruct((b_, t_, d_), BF16))
        out_specs.append(row)
    return pl.pallas_call(
        functools.partial(_hgrn_in_kernel, with_pos=with_pos, with_gate=with_gate),
        grid=(b_, t_ // tm),
        in_specs=in_specs, out_specs=out_specs, out_shape=out_shape,
        compiler_params=_params("arbitrary", "arbitrary", vmem=V7X_VMEM_LIMIT),
        name="hgrn_in_latent" if with_pos else "hgrn_in_context",
    )(*args)


def _gla_consts(reverse):
    c = GLA_CHUNK
    idx = np.arange(c)
    rank = (c - 1 - idx) if reverse else idx
    tri = (rank[None, :] <= rank[:, None]).astype(np.float32)
    hc = c // 2
    hrank = rank[:hc] - rank[:hc].min()
    lvl = np.full((hc, hc), -1, np.int32)
    rt, rs = hrank[:, None], hrank[None, :]
    lvl[rt == rs] = 0
    for level in range(1, GLA_LEVELS):
        blk, half = 1 << level, 1 << (level - 1)
        lvl[(rt // blk == rs // blk) & ((rt % blk) >= half) & ((rs % blk) < half)] = level
    return jnp.asarray(tri, BF16), jnp.asarray(lvl)


def _later_group(level, reverse, group):
    rank = (GLA_CHUNK - 1 - 8 * group) if reverse else 8 * group
    return (rank % (1 << level)) >= (1 << (level - 1))


def _boundary_rows(level, reverse):
    c = GLA_CHUNK
    blk, half = 1 << level, 1 << (level - 1)
    rows = []
    for i in range(c):
        rank = (c - 1 - i) if reverse else i
        brank = (rank // blk) * blk + half - 1
        rows.append((c - 1 - brank) if reverse else brank)
    return rows


def _gla_low_levels(q, kk, lf, cum, lvl, bcast, halves, reverse):
    ng = q.shape[0] // 8
    qb, kb = q.astype(BF16), kk.astype(BF16)
    tiles = [jnp.where(lvl == 0, lax.dot_general(qb[hs], kb[hs], NT_DIMS, preferred_element_type=F32), 0.0)
             for hs in halves]
    row8 = lax.broadcasted_iota(jnp.int32, (8, HEAD_DIM), 0)
    rank8 = (7 - row8) if reverse else row8
    for level in range(1, 4):
        sgn8 = jnp.where(((rank8 >> (level - 1)) & 1) == 1, 1.0, -1.0)
        sgn = jnp.concatenate([sgn8] * ng, axis=0)
        later = sgn > 0.0
        if level == 1:
            g = jnp.where(later, lf, 0.0)
        else:
            brow = _boundary_rows(level, reverse)
            pieces = []
            for grp in range(ng):
                first = bcast(brow[8 * grp])
                pieces.append(jnp.where(row8 < 4, first, bcast(brow[8 * grp + 7])) if level == 2 else first)
            g = (cum - jnp.concatenate(pieces, axis=0)) * sgn
        xe = (jnp.where(later, q, kk) * jnp.exp(g)).astype(BF16)
        for h, hs in enumerate(halves):
            s = lax.dot_general(xe[hs], xe[hs], NT_DIMS, preferred_element_type=F32)
            tiles[h] = jnp.where(lvl == level, s, tiles[h])
    return tiles


def _gla_chunk(q_ref, v_ref, lf_ref, lanes, st_ref, cum_ref, tri_ref, lvl_ref, o_ref, reverse, merged):
    c = GLA_CHUNK
    hc, ng = c // 2, c // 8
    halves = (slice(0, hc), slice(hc, c))
    lf = lf_ref[0, :, lanes]
    q = q_ref[0, :, lanes].astype(F32)
    v = v_ref[0, :, lanes]
    kk = 1.0 - jnp.exp(lf)
    hi = lf.astype(BF16)
    lo = (lf - hi.astype(F32)).astype(BF16)
    two = jnp.dot(tri_ref[...], jnp.concatenate([hi, lo], axis=1), preferred_element_type=F32)
    cum = two[:, HEAD_DIM:] + two[:, :HEAD_DIM]
    cum_ref[...] = cum
    lvl = lvl_ref[...]
    bcast = lambda r: jnp.broadcast_to(cum_ref[r:r + 1, :], (8, HEAD_DIM))
    groups = lambda a: [a[8 * i:8 * i + 8] for i in range(a.shape[0] // 8)]

    if merged:
        blk = 1 << merged
        cache, pieces = {}, []
        for grp in range(ng):
            rank = (c - 1 - 8 * grp) if reverse else 8 * grp
            first = (rank // blk) * blk
            row = (c - 1 - first) if reverse else first
            pieces.append(cache.setdefault(row, bcast(row)))
        inside = jnp.logical_and(lvl >= 0, lvl <= merged)
        tiles = []
        for h, hs in enumerate(halves):
            d = cum[hs] - jnp.concatenate(pieces[h * ng // 2:(h + 1) * ng // 2], axis=0)
            xq = (q[hs] * jnp.exp(d)).astype(BF16)
            xk = (kk[hs] * jnp.exp(-d)).astype(BF16)
            tiles.append(jnp.where(inside, lax.dot_general(xq, xk, NT_DIMS, preferred_element_type=F32), 0.0))
    else:
        tiles = _gla_low_levels(q, kk, lf, cum, lvl, bcast, halves, reverse)

    tile_rows = [groups(t) for t in tiles]
    lvl_rows = groups(lvl)
    q_rows, k_rows, cum_rows = groups(q), groups(kk), groups(cum)
    for h in range(2):
        half_groups = range(h * ng // 2, (h + 1) * ng // 2)
        for level in range(max(4, merged + 1), GLA_LEVELS):
            brow = _boundary_rows(level, reverse)
            cache = {}
            g_rows, x_rows = [], []
            for grp in half_groups:
                cb = cache.setdefault(brow[8 * grp], bcast(brow[8 * grp]))
                later = _later_group(level, reverse, grp)
                g_rows.append(cum_rows[grp] - cb if later else cb - cum_rows[grp])
                x_rows.append(q_rows[grp] if later else k_rows[grp])
            xe = jnp.concatenate(x_rows, axis=0) * jnp.exp(jnp.concatenate(g_rows, axis=0))
            xe_rows = groups(xe)
            later_local = [grp - half_groups[0] for grp in half_groups if _later_group(level, reverse, grp)]
            qc = jnp.concatenate([xe_rows[i] for i in later_local], axis=0).astype(BF16)
            s = lax.dot_general(qc, xe.astype(BF16), NT_DIMS, preferred_element_type=F32)
            for i, local in enumerate(later_local):
                tile_rows[h][local] = jnp.where(lvl_rows[local] == level, s[8 * i:8 * i + 8], tile_rows[h][local])

    early, late = (1, 0) if reverse else (0, 1)
    cb = cum_ref[_boundary_rows(GLA_LEVELS, reverse)[0]:_boundary_rows(GLA_LEVELS, reverse)[0] + 1, :]
    ql = (q[halves[late]] * jnp.exp(cum[halves[late]] - cb)).astype(BF16)
    ke = (kk[halves[early]] * jnp.exp(cb - cum[halves[early]])).astype(BF16)
    cross = lax.dot_general(ql, ke, NT_DIMS, preferred_element_type=F32)
    t_a, t_b = (jnp.concatenate(rows, axis=0) for rows in tile_rows)
    zero = jnp.zeros((hc, hc), F32)
    if reverse:
        scores = jnp.concatenate([jnp.concatenate([t_a, cross], axis=1), jnp.concatenate([zero, t_b], axis=1)], axis=0)
    else:
        scores = jnp.concatenate([jnp.concatenate([t_a, zero], axis=1), jnp.concatenate([cross, t_b], axis=1)], axis=0)

    st = st_ref[...]
    qe = (q * jnp.exp(cum)).astype(BF16)
    o = jnp.dot(scores.astype(BF16), v, preferred_element_type=F32)
    o = o + lax.dot_general(qe, st.astype(BF16), NT_DIMS, preferred_element_type=F32)
    o_ref[0, :, lanes] = o.astype(o_ref.dtype)
    last_row = 0 if reverse else c - 1
    last = cum_ref[last_row:last_row + 1, :]
    ke_all = (kk * jnp.exp(last - cum)).astype(BF16)
    st_ref[...] = st * jnp.exp(last) + lax.dot_general(v, ke_all, TN_DIMS, preferred_element_type=F32)


def _gla_kernel(qf_ref, vf_ref, lff_ref, qb_ref, vb_ref, lfb_ref, s0f_ref, s0b_ref,
                trif_ref, lvlf_ref, trib_ref, lvlb_ref,
                of_ref, ob_ref, sff_ref, sfb_ref, stf_ref, stb_ref, cumf_ref, cumb_ref, *, merged):
    j = pl.program_id(2)

    @pl.when(j == 0)
    def _():
        stf_ref[...] = s0f_ref[0]
        stb_ref[...] = s0b_ref[0]

    for k in range(stf_ref.shape[0]):
        lanes = slice(k * HEAD_DIM, (k + 1) * HEAD_DIM)
        _gla_chunk(qf_ref, vf_ref, lff_ref, lanes, stf_ref.at[k], cumf_ref.at[k], trif_ref, lvlf_ref, of_ref,
                   False, merged)
        _gla_chunk(qb_ref, vb_ref, lfb_ref, lanes, stb_ref.at[k], cumb_ref.at[k], trib_ref, lvlb_ref, ob_ref,
                   True, merged)

    @pl.when(j == pl.num_programs(2) - 1)
    def _():
        sff_ref[0] = stf_ref[...]
        sfb_ref[0] = stb_ref[...]


def _gla_bidir(q, v, lf_f, lf_b, s0f, s0b, lb):
    worst = (2 ** GLA_MERGED - 1) * jnp.max(-jnp.log(lb))
    run = lambda merged: (lambda *a: _gla_call(*a, merged=merged))
    return lax.cond(worst < GLA_MAX_EXPONENT, run(GLA_MERGED), run(0), q, v, lf_f, lf_b, s0f, s0b)


def _gla_call(q, v, lf_f, lf_b, s0f, s0b, *, merged):
    b_, t_, d_ = q.shape
    h_ = d_ // HEAD_DIM
    hp = math.gcd(GLA_HEADS_PER_STEP, h_)
    c = GLA_CHUNK
    n = t_ // c
    fwd = lambda b, h, j: (b, j, h)
    bwd = lambda b, h, j: (b, n - 1 - j, h)
    st = lambda b, h, j: (b, h, 0, 0)
    const = lambda b, h, j: (0, 0)
    blk = lambda im: pl.BlockSpec((1, c, hp * HEAD_DIM), im)
    st_spec = pl.BlockSpec((1, hp, HEAD_DIM, HEAD_DIM), st)
    cspecs = [pl.BlockSpec((c, c), const), pl.BlockSpec((c // 2, c // 2), const)]
    state = pltpu.VMEM((hp, HEAD_DIM, HEAD_DIM), F32)
    cum = pltpu.VMEM((hp, c, HEAD_DIM), F32)
    return pl.pallas_call(
        functools.partial(_gla_kernel, merged=merged),
        grid=(b_, h_ // hp, n),
        in_specs=[blk(fwd), blk(fwd), blk(fwd), blk(bwd), blk(bwd), blk(bwd), st_spec, st_spec] + cspecs + cspecs,
        out_specs=[blk(fwd), blk(bwd), st_spec, st_spec],
        out_shape=[jax.ShapeDtypeStruct((b_, t_, d_), BF16)] * 2
                  + [jax.ShapeDtypeStruct((b_, h_, HEAD_DIM, HEAD_DIM), F32)] * 2,
        scratch_shapes=[state, state, cum, cum],
        compiler_params=_params("arbitrary", "arbitrary", "arbitrary"),
        name="gla_merged" if merged else "gla_split",
    )(q, v, lf_f, q, v, lf_b, s0f, s0b, *_gla_consts(False), *_gla_consts(True))


def _mixer_epilogue(y, rows, x_ref, w_ref, gate_ref, gain_ref, shift_ref, scale_ref, wr_ref, x1_ref, h_ref, aff_ref):
    y = jnp.dot(y.astype(BF16), w_ref[...], preferred_element_type=F32)
    x1 = x_ref[0, rows] + gate_ref[0] * y
    x1_ref[0, rows] = x1
    hf = _modulate(x1, gain_ref[...], shift_ref[0], scale_ref[0])
    h_ref[0, rows] = hf.astype(BF16)
    logits = _dot_bf16x3(wr_ref[...], hf, NT_DIMS)
    e = jnp.exp(logits - jnp.max(logits, axis=0, keepdims=True))
    aff_ref[0, :, rows] = e / jnp.sum(e, axis=0, keepdims=True)


def _hgrn_out_kernel(of_ref, ob_ref, g_ref, x_ref, hnorm_ref, w_ref, *rest):
    hn = hnorm_ref[...]
    for rows in _sub_tiles(x_ref.shape[1]):
        o = of_ref[0, rows].astype(F32) + ob_ref[0, rows].astype(F32)
        heads = []
        for h in range(o.shape[-1] // HEAD_DIM):
            oh = o[:, h * HEAD_DIM:(h + 1) * HEAD_DIM]
            heads.append(oh * lax.rsqrt(jnp.mean(oh * oh, axis=-1, keepdims=True) + EPS) * hn)
        y = jnp.concatenate(heads, axis=-1) * _silu(g_ref[0, rows].astype(F32))
        _mixer_epilogue(y, rows, x_ref, w_ref, *rest)


def _conv_mixer_kernel(prev_ref, next_ref, x_ref, shift1_ref, scale1_ref, gain1_ref, win_ref, wc_ref, w_ref, *rest):
    i = pl.program_id(1)
    tm, d_ = x_ref.shape[1], x_ref.shape[2]
    subs = _sub_tiles(tm)
    halo = jnp.concatenate([prev_ref[0], next_ref[0]], axis=0)
    b_gate, cu = [], []
    for k, rows in enumerate(subs):
        xs = x_ref[0, rows]
        if k == 0:
            xs = jnp.concatenate([xs, halo], axis=0)
        h = _modulate(xs, gain1_ref[...], shift1_ref[0], scale1_ref[0]).astype(BF16)
        part = lambda p, hh: jnp.dot(hh, win_ref[:, p * d_:(p + 1) * d_], preferred_element_type=F32)
        n = rows.stop - rows.start
        b_gate.append(part(0, h[:n]))
        cu.append(part(1, h) * part(2, h))
    cu_halo = cu[0][subs[0].stop - subs[0].start:]
    cu = jnp.concatenate([cu[0][:subs[0].stop - subs[0].start]] + cu[1:], axis=0)
    rid = lax.broadcasted_iota(jnp.int32, cu.shape, 0)
    before = jnp.where(i == 0, 0.0, cu_halo[7:8])
    after = jnp.where(i == pl.num_programs(1) - 1, 0.0, cu_halo[8:9])
    left = jnp.where(rid == 0, before, pltpu.roll(cu, 1, 0))
    right = jnp.where(rid == tm - 1, after, pltpu.roll(cu, tm - 1, 0))
    wc = wc_ref[...]
    conv = left * wc[0:1] + cu * wc[1:2] + right * wc[2:3]
    for k, rows in enumerate(subs):
        _mixer_epilogue(b_gate[k] * conv[rows], rows, x_ref, w_ref, *rest)


def _mixer_out(kernel, name, row_args, halo_args, x, sample_args, small_args, gate, gain, shift, scale, w_router_t):
    b_, t_, d_ = x.shape
    e_ = w_router_t.shape[0]
    tm = min(MIXER_ROW_TILE, t_)
    row = pl.BlockSpec((1, tm, d_), lambda b, i: (b, i, 0))
    per_sample = pl.BlockSpec((1, 1, d_), lambda b, i: (b, 0, 0))
    whole = lambda a: pl.BlockSpec(a.shape, lambda b, i: (0,) * a.ndim)
    n8 = t_ // 8
    halo_specs = [pl.BlockSpec((1, 8, d_), lambda b, i: (b, jnp.maximum(i * (tm // 8) - 1, 0), 0)),
                  pl.BlockSpec((1, 8, d_), lambda b, i: (b, jnp.minimum((i + 1) * (tm // 8), n8 - 1), 0))]
    return pl.pallas_call(
        kernel,
        grid=(b_, t_ // tm),
        in_specs=[row] * len(row_args) + halo_specs[:len(halo_args)] + [row] + [per_sample] * len(sample_args)
                 + [whole(a) for a in small_args] + [per_sample, whole(gain), per_sample, per_sample, whole(w_router_t)],
        out_specs=[row, row, pl.BlockSpec((1, e_, tm), lambda b, i: (b, 0, i))],
        out_shape=[jax.ShapeDtypeStruct((b_, t_, d_), F32), jax.ShapeDtypeStruct((b_, t_, d_), BF16),
                   jax.ShapeDtypeStruct((b_, e_, t_), F32)],
        compiler_params=_params("arbitrary", "arbitrary", vmem=V7X_VMEM_LIMIT),
        name=name,
    )(*row_args, *halo_args, x, *sample_args, *small_args, gate, gain, shift, scale, w_router_t)


def _route_kernel(aff_ref, tri_ref, blockind_ref, slot_ref, base_ref, dense_ref, *, cap, n_experts):
    aff = aff_ref[...]
    e_, t_ = aff.shape

    def as_float(word):
        return pltpu.bitcast(word, F32)

    def count_ge(th):
        return jnp.sum(jnp.where(aff >= th, 1.0, 0.0), axis=1, keepdims=True)

    def search(_, carry):
        lo, hi = carry
        mid = lo + ((hi - lo + 1) >> 1)
        ok = count_ge(as_float(mid)) >= cap
        return jnp.where(ok, mid, lo), jnp.where(ok, hi, mid - 1)

    lo0 = jnp.zeros((e_, 1), jnp.int32)
    hi0 = jnp.full((e_, 1), 0x7F7FFFFF, jnp.int32)
    kth, _ = lax.fori_loop(0, 32, search, (lo0, hi0))
    above = aff >= as_float(kth + 1)
    tied = jnp.logical_and(aff >= as_float(kth), jnp.logical_not(above))
    need = cap - jnp.sum(jnp.where(above, 1.0, 0.0), axis=1, keepdims=True)
    tri = tri_ref[...]
    tb = tri.shape[0]
    carry_t = jnp.zeros((e_, 1), F32)
    carry_s = jnp.zeros((e_, 1), F32)
    sel_blocks = []
    for j in range(t_ // tb):
        cols = slice(j * tb, (j + 1) * tb)
        tied_j = tied[:, cols]
        ct = jnp.dot(jnp.where(tied_j, 1.0, 0.0).astype(BF16), tri, preferred_element_type=F32) + carry_t
        carry_t = ct[:, tb - 1:tb]
        sel_j = jnp.where(above[:, cols], 1.0, jnp.where(tied_j & (ct <= need), 1.0, 0.0))
        cs = jnp.dot(sel_j.astype(BF16), tri, preferred_element_type=F32) + carry_s
        carry_s = cs[:, tb - 1:tb]
        slot_ref[:, cols] = jnp.where(sel_j > 0.0, cs - 1.0, -1.0).astype(jnp.int32)
        sel_blocks.append(sel_j.astype(BF16))
    sel = jnp.concatenate(sel_blocks, axis=1)
    counts = jnp.dot(sel, blockind_ref[...], preferred_element_type=F32)
    base, end = counts[:, :128], counts[:, 128:]
    base_ref[...] = base.astype(jnp.int32)
    span = end - jnp.floor(base * (1.0 / SLOT_ALIGN)) * SLOT_ALIGN
    for b in range(e_ // n_experts):
        widest = jnp.max(span[b * n_experts:(b + 1) * n_experts], axis=0, keepdims=True)
        dense_ref[b] = sum(jnp.where(widest <= w, 1, 0) for w in DENSE_WINDOWS).astype(jnp.int32)


def _route(aff):
    b_, e_, t_ = aff.shape
    cap = EC_CAPACITY_FACTOR * t_ // e_
    tb = TOKEN_BLOCK
    nb = t_ // tb
    tri = jnp.asarray(np.triu(np.ones((tb, tb), np.float32)), BF16)
    tok, col = np.arange(t_)[:, None], np.arange(128)[None, :]
    blockind = np.concatenate([(tok < col * tb) & (col <= nb), (tok < (col + 1) * tb) & (col < nb)], axis=1)
    slot, base, dense = pl.pallas_call(
        functools.partial(_route_kernel, cap=cap, n_experts=e_),
        out_shape=[jax.ShapeDtypeStruct((b_ * e_, t_), jnp.int32), jax.ShapeDtypeStruct((b_ * e_, 128), jnp.int32),
                   jax.ShapeDtypeStruct((b_, 1, 128), jnp.int32)],
        name="route",
    )(aff.reshape(b_ * e_, t_), tri, jnp.asarray(blockind.astype(np.float32), BF16))
    return slot.reshape(b_, e_, t_), base[:, :nb + 1].reshape(-1), dense[:, 0, :nb].reshape(-1)


def _window_plan(tbl_ref, b, e, tb, ne, nb):
    idx = (b * ne + e) * (nb + 1) + tb
    base, end = tbl_ref[idx], tbl_ref[idx + 1]
    start = _align_down(base)
    n_win = jnp.where(end > base, (end - start + SLOT_WINDOW - 1) >> (SLOT_WINDOW.bit_length() - 1), 0)
    return start, n_win


def _align_down(slot):
    shift = SLOT_ALIGN.bit_length() - 1
    return (slot >> shift) << shift


def _dense_start(tbl_ref, b, e, tb, ne, nb, cap, window):
    base = tbl_ref[(b * ne + e) * (nb + 1) + tb]
    return pl.multiple_of(jnp.minimum(_align_down(base), cap - window), SLOT_ALIGN)


def _gather_kernel(tbl_ref, dense_ref, h_ref, slot_ref, xg_ref, *, ne, nb, cap):
    b, step = pl.program_id(0), pl.program_id(2)
    n_tok = TOKEN_BLOCK

    @pl.when(step == 0)
    def _():
        xg_ref[...] = jnp.zeros(xg_ref.shape, xg_ref.dtype)

    def dense_block(sub, window):
        tb = step * BLOCKS_PER_STEP + sub
        toks = slice(sub * n_tok, (sub + 1) * n_tok)
        starts = [_dense_start(tbl_ref, b, e, tb, ne, nb, cap, window) for e in range(ne)]
        ids = lax.broadcasted_iota(jnp.int32, (window, n_tok), 0)
        onehot = jnp.concatenate(
            [jnp.where(ids == slot_ref[0, e:e + 1, toks] - starts[e], 1.0, 0.0).astype(BF16) for e in range(ne)],
            axis=0)
        rows = jnp.dot(onehot, h_ref[0, toks, :], preferred_element_type=F32).astype(BF16)
        for e in range(ne):
            win = xg_ref.at[0, e, pl.ds(starts[e], window), :]
            win[...] = win[...] + rows[e * window:(e + 1) * window]

    def windowed_block(sub):
        tb = step * BLOCKS_PER_STEP + sub
        toks = slice(sub * n_tok, (sub + 1) * n_tok)
        h = h_ref[0, toks, :]
        ids0 = lax.broadcasted_iota(jnp.int32, (SLOT_WINDOW, n_tok), 0)
        for e in range(ne):
            srow = slot_ref[0, e:e + 1, toks]
            start, n_win = _window_plan(tbl_ref, b, e, tb, ne, nb)

            def body(k, carry, e=e, srow=srow, start=start):
                lo = start + k * SLOT_WINDOW
                w0 = pl.multiple_of(jnp.minimum(lo, cap - SLOT_WINDOW), SLOT_ALIGN)
                ids = w0 + ids0
                onehot = jnp.where(ids >= lo, jnp.where(ids == srow, 1.0, 0.0), 0.0).astype(BF16)
                rows = jnp.dot(onehot, h, preferred_element_type=F32)
                win = xg_ref.at[0, e, pl.ds(w0, SLOT_WINDOW), :]
                win[...] = win[...] + rows.astype(BF16)
                return carry

            lax.fori_loop(0, n_win, body, 0)

    _per_block_dispatch(dense_ref, b * nb + step * BLOCKS_PER_STEP, BLOCKS_PER_STEP, dense_block, windowed_block)


def _per_block_dispatch(dense_ref, first, n_blocks, dense_block, windowed_block):
    fits = [dense_ref[first + sub] for sub in range(n_blocks)]
    common = functools.reduce(jnp.minimum, fits)
    n_tiers = len(DENSE_WINDOWS)
    for k, window in enumerate(DENSE_WINDOWS):
        chosen = (common >= n_tiers) if k == 0 else (common == n_tiers - k)

        @pl.when(chosen)
        def _(window=window):
            for sub in range(n_blocks):
                dense_block(sub, window)

    @pl.when(common == 0)
    def _():
        for sub in range(n_blocks):
            pl.when(fits[sub] > 0)(functools.partial(dense_block, sub, DENSE_WINDOWS[-1]))
            pl.when(fits[sub] == 0)(functools.partial(windowed_block, sub))


def _gather(h, slot, tbl, dense):
    b_, t_, d_ = h.shape
    e_ = slot.shape[1]
    cap = EC_CAPACITY_FACTOR * t_ // e_
    tb = TOKEN_BLOCK * BLOCKS_PER_STEP
    nb = t_ // TOKEN_BLOCK
    dh = d_ // 2
    return pl.pallas_call(
        functools.partial(_gather_kernel, ne=e_, nb=nb, cap=cap),
        grid_spec=pltpu.PrefetchScalarGridSpec(
            num_scalar_prefetch=2,
            grid=(b_, 2, t_ // tb),
            in_specs=[pl.BlockSpec((1, tb, dh), lambda b, c, i, *_: (b, i, c)),
                      pl.BlockSpec((1, e_, tb), lambda b, c, i, *_: (b, 0, i))],
            out_specs=pl.BlockSpec((1, e_, cap, dh), lambda b, c, i, *_: (b, 0, 0, c)),
        ),
        out_shape=jax.ShapeDtypeStruct((b_, e_, cap, d_), BF16),
        compiler_params=_params("arbitrary", "arbitrary", "arbitrary", vmem=V7X_VMEM_LIMIT),
        name="moe_gather",
    )(tbl, dense, h, slot)


def _expert_kernel(xg_ref, wg_ref, wu_ref, wd_ref, y_ref, acc_ref):
    f = pl.program_id(1)
    n_b, _, cap, _ = xg_ref.shape
    wg = wg_ref[0, 0].astype(BF16)
    wu = wu_ref[0, 0].astype(BF16)
    wd = wd_ref[0, 0].astype(BF16)

    @pl.when(jnp.logical_and(pl.program_id(0) == 0, f == 0))
    def _():
        acc_ref[...] = jnp.zeros(acc_ref.shape, F32)

    for b in range(n_b):
        for r in range(cap // FFN_ROWS):
            rows = pl.ds(r * FFN_ROWS, FFN_ROWS)
            acc_rows = pl.ds((b * cap) + r * FFN_ROWS, FFN_ROWS)
            xr = xg_ref[b, 0, rows, :]
            a = jnp.dot(xr, wg, preferred_element_type=F32)
            u = jnp.dot(xr, wu, preferred_element_type=F32)
            part = jnp.dot((_silu(a) * u).astype(BF16), wd, preferred_element_type=F32)
            total = jnp.where(f == 0, 0.0, acc_ref[acc_rows, :]) + part
            acc_ref[acc_rows, :] = total
            y_ref[b, 0, rows, :] = total.astype(BF16)


def _experts(xg, layer, w_gate, w_up, w_down):
    b_, e_, cap, d_ = xg.shape
    f_ = w_gate.shape[-1]
    ft = min(FFN_TILE, f_)
    return pl.pallas_call(
        _expert_kernel,
        grid=(e_, f_ // ft),
        in_specs=[pl.BlockSpec((b_, 1, cap, d_), lambda e, f: (0, e, 0, 0)),
                  pl.BlockSpec((1, 1, d_, ft), lambda e, f: (layer, e, 0, f)),
                  pl.BlockSpec((1, 1, d_, ft), lambda e, f: (layer, e, 0, f)),
                  pl.BlockSpec((1, 1, ft, d_), lambda e, f: (layer, e, f, 0))],
        out_specs=pl.BlockSpec((b_, 1, cap, d_), lambda e, f: (0, e, 0, 0)),
        out_shape=jax.ShapeDtypeStruct((b_, e_, cap, d_), BF16),
        scratch_shapes=[pltpu.VMEM((b_ * cap, d_), F32)],
        compiler_params=_params("arbitrary", "arbitrary", vmem=V7X_VMEM_LIMIT),
        name="moe_experts",
    )(xg, w_gate, w_up, w_down)


def _combine_kernel(tbl_ref, dense_ref, y_ref, slot_ref, aff_ref, x_ref, gate_ref, *rest, ne, nb, cap, final):
    if final:
        gain_ref, out_ref, acc_ref = rest
    else:
        out_ref, acc_ref = rest
    b, step = pl.program_id(0), pl.program_id(1)
    n_tok = TOKEN_BLOCK

    def finish(toks, moe):
        out = x_ref[0, toks, :] + gate_ref[0] * moe
        if final:
            out = out * lax.rsqrt(jnp.mean(out * out, axis=-1, keepdims=True) + EPS) * gain_ref[...]
        out_ref[0, toks, :] = out

    def dense_block(sub, window):
        tb = step * COMBINE_BLOCKS + sub
        toks = slice(sub * n_tok, (sub + 1) * n_tok)
        starts = [_dense_start(tbl_ref, b, e, tb, ne, nb, cap, window) for e in range(ne)]
        ids = lax.broadcasted_iota(jnp.int32, (window, n_tok), 0)
        weights = jnp.concatenate(
            [jnp.where(ids == slot_ref[0, e:e + 1, toks] - starts[e], aff_ref[0, e:e + 1, toks], 0.0).astype(BF16)
             for e in range(ne)], axis=0)
        yw = jnp.concatenate([y_ref[0, e, pl.ds(starts[e], window), :] for e in range(ne)], axis=0)
        finish(toks, lax.dot_general(weights, yw, TN_DIMS, preferred_element_type=F32))

    def windowed_block(sub):
        tb = step * COMBINE_BLOCKS + sub
        toks = slice(sub * n_tok, (sub + 1) * n_tok)
        ids0 = lax.broadcasted_iota(jnp.int32, (SLOT_WINDOW, n_tok), 0)
        acc_ref[...] = jnp.zeros(acc_ref.shape, F32)
        for e in range(ne):
            srow = slot_ref[0, e:e + 1, toks]
            grow = aff_ref[0, e:e + 1, toks]
            start, n_win = _window_plan(tbl_ref, b, e, tb, ne, nb)

            def body(k, carry, e=e, srow=srow, grow=grow, start=start):
                lo = start + k * SLOT_WINDOW
                w0 = pl.multiple_of(jnp.minimum(lo, cap - SLOT_WINDOW), SLOT_ALIGN)
                ids = w0 + ids0
                weights = jnp.where(ids >= lo, jnp.where(ids == srow, grow, 0.0), 0.0).astype(BF16)
                yw = y_ref[0, e, pl.ds(w0, SLOT_WINDOW), :]
                acc_ref[...] = acc_ref[...] + lax.dot_general(weights, yw, TN_DIMS, preferred_element_type=F32)
                return carry

            lax.fori_loop(0, n_win, body, 0)
        finish(toks, acc_ref[...])

    _per_block_dispatch(dense_ref, b * nb + step * COMBINE_BLOCKS, COMBINE_BLOCKS, dense_block, windowed_block)


def _combine(y, slot, aff, tbl, dense, x, gate, final_gain):
    b_, t_, d_ = x.shape
    e_, cap = y.shape[1], y.shape[2]
    tb = TOKEN_BLOCK * COMBINE_BLOCKS
    nb = t_ // TOKEN_BLOCK
    final = final_gain is not None
    route_spec = pl.BlockSpec((1, e_, tb), lambda b, i, *_: (b, 0, i))
    row = pl.BlockSpec((1, tb, d_), lambda b, i, *_: (b, i, 0))
    in_specs = [pl.BlockSpec((1, e_, cap, d_), lambda b, i, *_: (b, 0, 0, 0), pipeline_mode=pl.Buffered(1)),
                route_spec, route_spec, row, pl.BlockSpec((1, 1, d_), lambda b, i, *_: (b, 0, 0))]
    args = [tbl, dense, y, slot, aff, x, gate]
    if final:
        in_specs.append(pl.BlockSpec((1, d_), lambda b, i, *_: (0, 0)))
        args.append(final_gain)
    return pl.pallas_call(
        functools.partial(_combine_kernel, ne=e_, nb=nb, cap=cap, final=final),
        grid_spec=pltpu.PrefetchScalarGridSpec(
            num_scalar_prefetch=2,
            grid=(b_, t_ // tb),
            in_specs=in_specs,
            out_specs=row,
            scratch_shapes=[pltpu.VMEM((TOKEN_BLOCK, d_), F32)],
        ),
        out_shape=jax.ShapeDtypeStruct((b_, t_, d_), F32),
        compiler_params=_params("arbitrary", "arbitrary", vmem=V7X_VMEM_LIMIT),
        name="moe_combine_final" if final else "moe_combine",
    )(*args)


def _moe(x, h, aff, gate, layer, w_gate, w_up, w_down, final_gain=None):
    slot, tbl, dense = _route(aff)
    xg = _gather(h, slot, tbl, dense)
    y = _experts(xg, layer, w_gate, w_up, w_down)
    return _combine(y, slot, aff, tbl, dense, x, gate, final_gain)


def kernel(x, c, ctx, c_ctx, ada_w, ada_b, norm_mix, norm_ffn, norm_final, hg_w_in, hg_lb_logits, hg_norm, hg_w_out, sc_w_in, sc_conv, sc_w_out, moe_router, moe_w_gate, moe_w_up, moe_w_down):
    b_, t_, d_ = x.shape
    depth = ada_w.shape[0]
    n_ada = ada_w.shape[-1] // d_
    n_heads = d_ // HEAD_DIM
    n_experts = moe_router.shape[-1]
    cap = EC_CAPACITY_FACTOR * t_ // n_experts
    assert depth == 2 and n_ada == 6 and b_ + 1 <= 8
    assert t_ % GLA_CHUNK == 0 and ctx.shape[1] % GLA_CHUNK == 0 and IN_PROJ_SUB_ROWS % GRID_W == 0
    assert d_ % (2 * HEAD_DIM) == 0 and t_ % min(ROW_TILE, t_) == 0 and t_ % min(MIXER_ROW_TILE, t_) == 0
    assert t_ % (TOKEN_BLOCK * BLOCKS_PER_STEP) == 0 and t_ % (TOKEN_BLOCK * COMBINE_BLOCKS) == 0
    assert cap % SLOT_WINDOW == 0 and cap >= max(DENSE_WINDOWS) and cap % min(FFN_ROWS, cap) == 0
    assert moe_w_gate.shape[-1] % min(FFN_TILE, moe_w_gate.shape[-1]) == 0

    cond = jnp.concatenate([c, c_ctx[None], jnp.zeros((8 - b_ - 1, d_), F32)], axis=0)
    mod = _ada_vectors(cond, ada_w, ada_b, n_ada)
    vec = lambda i, j: mod[i, j, :b_][:, None, :]
    cvec = lambda i, j: mod[i, j, b_][None, None, :]
    rowtab, coltab, lower = _tables(hg_lb_logits, t_)
    row_of = lambda a, i: a[i][None, :]
    router_t = lambda i: jnp.swapaxes(moe_router[i], 0, 1)

    w_in = hg_w_in[0].astype(BF16)
    lb0 = row_of(lower, 0)
    gain0 = row_of(norm_mix, 0)
    qc, vc, lfc_f, lfc_b = _hgrn_in(ctx, None, gain0, cvec(0, 0), cvec(0, 1), lb0, w_in, with_gate=False)
    zeros = jnp.zeros((b_, n_heads, HEAD_DIM, HEAD_DIM), F32)
    _, _, s_f, s_b = _gla_bidir(qc, vc, lfc_f, lfc_b, zeros, zeros, lb0)
    x0, q, v, lf_f, lf_b, g = _hgrn_in(x, (rowtab, coltab), gain0, vec(0, 0), vec(0, 1), lb0, w_in, with_gate=True)
    o_f, o_b, _, _ = _gla_bidir(q, v, lf_f, lf_b, s_f, s_b, lb0)
    x1, h, aff = _mixer_out(_hgrn_out_kernel, "hgrn_out", [o_f, o_b, g], [], x0, [],
                            [row_of(hg_norm, 0), hg_w_out[0].astype(BF16)],
                            vec(0, 2), row_of(norm_ffn, 0), vec(0, 3), vec(0, 4), router_t(0))
    x2 = _moe(x1, h, aff, vec(0, 5), 0, moe_w_gate, moe_w_up, moe_w_down)

    x3, h, aff = _mixer_out(_conv_mixer_kernel, "conv_mixer", [], [x2, x2], x2, [vec(1, 0), vec(1, 1)],
                            [row_of(norm_mix, 1), sc_w_in[0].astype(BF16), sc_conv[0], sc_w_out[0].astype(BF16)],
                            vec(1, 2), row_of(norm_ffn, 1), vec(1, 3), vec(1, 4), router_t(1))
    return _moe(x3, h, aff, vec(1, 5), 1, moe_w_gate, moe_w_up, moe_w_down, final_gain=norm_final[None, :])
```

```python
import functools
import math

import numpy as np
import jax
import jax.numpy as jnp
from jax import lax
from jax.experimental import pallas as pl
from jax.experimental.pallas import tpu as pltpu

F32 = jnp.float32
BF16 = jnp.bfloat16

EPS = 1e-6
POS_TEMP = 10000.0
GRID_W = 64
HEAD_DIM = 128
EC_CAPACITY_FACTOR = 2
GLA_CHUNK = 256
GLA_LEVELS = 8
GLA_HEADS_PER_STEP = 8
GLA_MERGED = 5
GLA_MAX_EXPONENT = 80.0
ROW_TILE = 512
MIXER_ROW_TILE = 1024
SUB_ROWS = 1024
IN_PROJ_SUB_ROWS = 128
TOKEN_BLOCK = 256
BLOCKS_PER_STEP = 4
COMBINE_BLOCKS = 2
SLOT_WINDOW = 64
DENSE_WINDOWS = (96, 128)
SLOT_ALIGN = 16
FFN_TILE = 1024
FFN_ROWS = 256
V7X_VMEM_LIMIT = 56 * 1024 * 1024

NT_DIMS = (((1,), (1,)), ((), ()))
TN_DIMS = (((0,), (0,)), ((), ()))


def _params(*sem, vmem=None):
    return pltpu.CompilerParams(dimension_semantics=sem, vmem_limit_bytes=vmem)


def _dot_bf16x3(a, b, dims):
    a0, b0 = a.astype(BF16), b.astype(BF16)
    a1 = (a - a0.astype(F32)).astype(BF16)
    b1 = (b - b0.astype(F32)).astype(BF16)
    d = lambda x, y: lax.dot_general(x, y, dims, preferred_element_type=F32)
    return (d(a0, b1) + d(a1, b0)) + d(a0, b0)


def _sub_tiles(n_rows, sub_rows=SUB_ROWS):
    sub = min(sub_rows, n_rows)
    return [slice(s, s + sub) for s in range(0, n_rows, sub)]


def _sigmoid(x):
    return 1.0 / (1.0 + jnp.exp(-x))


def _silu(x):
    return x * _sigmoid(x)


def _modulate(x, gain, shift, scale):
    y = x * lax.rsqrt(jnp.mean(x * x, axis=-1, keepdims=True) + EPS)
    return y * (gain * (1.0 + scale)) + shift


def _ada_kernel(cond_ref, w_ref, b_ref, out_ref):
    s = _silu(cond_ref[...])
    out_ref[0, 0] = _dot_bf16x3(s, w_ref[0], (((1,), (0,)), ((), ()))) + b_ref[0, 0]


def _ada_vectors(cond, ada_w, ada_b, n_ada):
    depth, d_, _ = ada_w.shape
    return pl.pallas_call(
        _ada_kernel,
        grid=(depth, n_ada),
        in_specs=[pl.BlockSpec((8, d_), lambda i, j: (0, 0)),
                  pl.BlockSpec((1, d_, d_), lambda i, j: (i, 0, j)),
                  pl.BlockSpec((1, 1, 1, d_), lambda i, j: (i, j, 0, 0))],
        out_specs=pl.BlockSpec((1, 1, 8, d_), lambda i, j: (i, j, 0, 0)),
        out_shape=jax.ShapeDtypeStruct((depth, n_ada, 8, d_), F32),
        compiler_params=_params("arbitrary", "arbitrary"),
        name="ada_vectors",
    )(cond, ada_w, ada_b.reshape(depth, n_ada, 1, d_))


def _tables_kernel(lb_logits_ref, rowtab_ref, coltab_ref, lb_ref, *, n_freq):
    def table(n_pos):
        p = lax.broadcasted_iota(jnp.int32, (n_pos, n_freq), 0).astype(F32)
        j = lax.broadcasted_iota(jnp.int32, (n_pos, n_freq), 1).astype(F32)
        omega = jnp.exp(j * (-math.log(POS_TEMP) / n_freq))
        ang = p * omega
        return jnp.concatenate([jnp.sin(ang), jnp.cos(ang)], axis=-1)

    rowtab_ref[...] = table(rowtab_ref.shape[0])
    coltab_ref[...] = table(coltab_ref.shape[0])
    logits = lb_logits_ref[...]
    e = jnp.exp(logits - jnp.max(logits, axis=0, keepdims=True))
    sm = e / jnp.sum(e, axis=0, keepdims=True)
    acc = sm[0:1]
    lb_ref[0:1] = acc
    for i in range(1, lb_ref.shape[0]):
        acc = acc + sm[i:i + 1]
        lb_ref[i:i + 1] = acc


def _tables(lb_logits, n_tokens):
    n_lb, d_ = lb_logits.shape
    n_freq = d_ // 4
    rows = n_tokens // GRID_W
    return pl.pallas_call(
        functools.partial(_tables_kernel, n_freq=n_freq),
        out_shape=[jax.ShapeDtypeStruct((rows, 2 * n_freq), F32),
                   jax.ShapeDtypeStruct((GRID_W, 2 * n_freq), F32),
                   jax.ShapeDtypeStruct((n_lb, d_), F32)],
        name="pos_tables",
    )(lb_logits)


def _hgrn_in_kernel(*refs, with_pos, with_gate):
    it = iter(refs)
    x_ref = next(it)
    if with_pos:
        rowtab_ref, coltab_ref = next(it), next(it)
    gain_ref, shift_ref, scale_ref, lb_ref, w_ref = next(it), next(it), next(it), next(it), next(it)
    if with_pos:
        x0_ref = next(it)
    q_ref, v_ref, lff_ref, lfb_ref = next(it), next(it), next(it), next(it)
    g_ref = next(it) if with_gate else None

    d_ = x_ref.shape[-1]
    lb = lb_ref[...]
    for rows in _sub_tiles(x_ref.shape[1], IN_PROJ_SUB_ROWS):
        x = x_ref[0, rows]
        if with_pos:
            grid_rows = range(rows.start // GRID_W, rows.stop // GRID_W)
            pos_row = jnp.concatenate(
                [jnp.broadcast_to(rowtab_ref[0, r:r + 1, :], (GRID_W, rowtab_ref.shape[-1])) for r in grid_rows], axis=0)
            pos_col = jnp.concatenate([coltab_ref[...]] * len(grid_rows), axis=0)
            x = x + jnp.concatenate([pos_row, pos_col], axis=-1)
            x0_ref[0, rows] = x
        h = _modulate(x, gain_ref[...], shift_ref[0], scale_ref[0]).astype(BF16)
        part = lambda p: jnp.dot(h, w_ref[:, p * d_:(p + 1) * d_], preferred_element_type=F32)
        lff_ref[0, rows] = jnp.log(lb + (1.0 - lb) * _sigmoid(part(2)))
        lfb_ref[0, rows] = jnp.log(lb + (1.0 - lb) * _sigmoid(part(3)))
        q_ref[0, rows] = (part(0) * HEAD_DIM ** -0.5).astype(BF16)
        v_ref[0, rows] = part(1).astype(BF16)
        if with_gate:
            g_ref[0, rows] = part(4).astype(BF16)


def _hgrn_in(x, tabs, gain, shift, scale, lb, w_in, *, with_gate):
    b_, t_, d_ = x.shape
    tm = min(ROW_TILE, t_)
    with_pos = tabs is not None
    per_sample = lambda a: pl.BlockSpec((1, 1, d_), (lambda b, i: (b, 0, 0)) if a.shape[0] > 1 else (lambda b, i: (0, 0, 0)))
    row = pl.BlockSpec((1, tm, d_), lambda b, i: (b, i, 0))
    vec = pl.BlockSpec((1, d_), lambda b, i: (0, 0))
    args, in_specs = [x], [row]
    if with_pos:
        rowtab, coltab = tabs
        rows_per_tile = tm // GRID_W
        args += [rowtab.reshape(rowtab.shape[0] // rows_per_tile, rows_per_tile, rowtab.shape[1]), coltab]
        in_specs += [pl.BlockSpec((1, rows_per_tile, rowtab.shape[1]), lambda b, i: (i, 0, 0)),
                     pl.BlockSpec(coltab.shape, lambda b, i: (0, 0))]
    args += [gain, shift, scale, lb, w_in]
    in_specs += [vec, per_sample(shift), per_sample(scale), vec, pl.BlockSpec(w_in.shape, lambda b, i: (0, 0))]
    out_shape, out_specs = [], []
    if with_pos:
        out_shape.append(jax.ShapeDtypeStruct((b_, t_, d_), F32))
        out_specs.append(row)
    out_shape += [jax.ShapeDtypeStruct((b_, t_, d_), BF16)] * 2 + [jax.ShapeDtypeStruct((b_, t_, d_), F32)] * 2
    out_specs += [row] * 4
    if with_gate:
        out_shape.append(jax.ShapeDtypeStruct((b_, t_, d_), BF16))
        out_specs.append(row)
    return pl.pallas_call(
        functools.partial(_hgrn_in_kernel, with_pos=with_pos, with_gate=with_gate),
        grid=(b_, t_ // tm),
        in_specs=in_specs, out_specs=out_specs, out_shape=out_shape,
        compiler_params=_params("arbitrary", "arbitrary", vmem=V7X_VMEM_LIMIT),
        name="hgrn_in_latent" if with_pos else "hgrn_in_context",
    )(*args)


def _gla_consts(reverse):
    c = GLA_CHUNK
    idx = np.arange(c)
    rank = (c - 1 - idx) if reverse else idx
    tri = (rank[None, :] <= rank[:, None]).astype(np.float32)
    hc = c // 2
    hrank = rank[:hc] - rank[:hc].min()
    lvl = np.full((hc, hc), -1, np.int32)
    rt, rs = hrank[:, None], hrank[None, :]
    lvl[rt == rs] = 0
    for level in range(1, GLA_LEVELS):
        blk, half = 1 << level, 1 << (level - 1)
        lvl[(rt // blk == rs // blk) & ((rt % blk) >= half) & ((rs % blk) < half)] = level
    return jnp.asarray(tri, BF16), jnp.asarray(lvl)


def _later_group(level, reverse, group):
    rank = (GLA_CHUNK - 1 - 8 * group) if reverse else 8 * group
    return (rank % (1 << level)) >= (1 << (level - 1))


def _boundary_rows(level, reverse):
    c = GLA_CHUNK
    blk, half = 1 << level, 1 << (level - 1)
    rows = []
    for i in range(c):
        rank = (c - 1 - i) if reverse else i
        brank = (rank // blk) * blk + half - 1
        rows.append((c - 1 - brank) if reverse else brank)
    return rows


def _gla_low_levels(q, kk, lf, cum, lvl, bcast, halves, reverse):
    ng = q.shape[0] // 8
    qb, kb = q.astype(BF16), kk.astype(BF16)
    tiles = [jnp.where(lvl == 0, lax.dot_general(qb[hs], kb[hs], NT_DIMS, preferred_element_type=F32), 0.0)
             for hs in halves]
    row8 = lax.broadcasted_iota(jnp.int32, (8, HEAD_DIM), 0)
    rank8 = (7 - row8) if reverse else row8
    for level in range(1, 4):
        sgn8 = jnp.where(((rank8 >> (level - 1)) & 1) == 1, 1.0, -1.0)
        sgn = jnp.concatenate([sgn8] * ng, axis=0)
        later = sgn > 0.0
        if level == 1:
            g = jnp.where(later, lf, 0.0)
        else:
            brow = _boundary_rows(level, reverse)
            pieces = []
            for grp in range(ng):
                first = bcast(brow[8 * grp])
                pieces.append(jnp.where(row8 < 4, first, bcast(brow[8 * grp + 7])) if level == 2 else first)
            g = (cum - jnp.concatenate(pieces, axis=0)) * sgn
        xe = (jnp.where(later, q, kk) * jnp.exp(g)).astype(BF16)
        for h, hs in enumerate(halves):
            s = lax.dot_general(xe[hs], xe[hs], NT_DIMS, preferred_element_type=F32)
            tiles[h] = jnp.where(lvl == level, s, tiles[h])
    return tiles


def _gla_chunk(q_ref, v_ref, lf_ref, lanes, st_ref, cum_ref, tri_ref, lvl_ref, o_ref, reverse, merged):
    c = GLA_CHUNK
    hc, ng = c // 2, c // 8
    halves = (slice(0, hc), slice(hc, c))
    lf = lf_ref[0, :, lanes]
    q = q_ref[0, :, lanes].astype(F32)
    v = v_ref[0, :, lanes]
    kk = 1.0 - jnp.exp(lf)
    hi = lf.astype(BF16)
    lo = (lf - hi.astype(F32)).astype(BF16)
    two = jnp.dot(tri_ref[...], jnp.concatenate([hi, lo], axis=1), preferred_element_type=F32)
    cum = two[:, HEAD_DIM:] + two[:, :HEAD_DIM]
    cum_ref[...] = cum
    lvl = lvl_ref[...]
    bcast = lambda r: jnp.broadcast_to(cum_ref[r:r + 1, :], (8, HEAD_DIM))
    groups = lambda a: [a[8 * i:8 * i + 8] for i in range(a.shape[0] // 8)]

    if merged:
        blk = 1 << merged
        cache, pieces = {}, []
        for grp in range(ng):
            rank = (c - 1 - 8 * grp) if reverse else 8 * grp
            first = (rank // blk) * blk
            row = (c - 1 - first) if reverse else first
            pieces.append(cache.setdefault(row, bcast(row)))
        inside = jnp.logical_and(lvl >= 0, lvl <= merged)
        tiles = []
        for h, hs in enumerate(halves):
            d = cum[hs] - jnp.concatenate(pieces[h * ng // 2:(h + 1) * ng // 2], axis=0)
            xq = (q[hs] * jnp.exp(d)).astype(BF16)
            xk = (kk[hs] * jnp.exp(-d)).astype(BF16)
            tiles.append(jnp.where(inside, lax.dot_general(xq, xk, NT_DIMS, preferred_element_type=F32), 0.0))
    else:
        tiles = _gla_low_levels(q, kk, lf, cum, lvl, bcast, halves, reverse)

    tile_rows = [groups(t) for t in tiles]
    lvl_rows = groups(lvl)
    q_rows, k_rows, cum_rows = groups(q), groups(kk), groups(cum)
    for h in range(2):
        half_groups = range(h * ng // 2, (h + 1) * ng // 2)
        for level in range(max(4, merged + 1), GLA_LEVELS):
            brow = _boundary_rows(level, reverse)
            cache = {}
            g_rows, x_rows = [], []
            for grp in half_groups:
                cb = cache.setdefault(brow[8 * grp], bcast(brow[8 * grp]))
                later = _later_group(level, reverse, grp)
                g_rows.append(cum_rows[grp] - cb if later else cb - cum_rows[grp])
                x_rows.append(q_rows[grp] if later else k_rows[grp])
            xe = jnp.concatenate(x_rows, axis=0) * jnp.exp(jnp.concatenate(g_rows, axis=0))
            xe_rows = groups(xe)
            later_local = [grp - half_groups[0] for grp in half_groups if _later_group(level, reverse, grp)]
            qc = jnp.concatenate([xe_rows[i] for i in later_local], axis=0).astype(BF16)
            s = lax.dot_general(qc, xe.astype(BF16), NT_DIMS, preferred_element_type=F32)
            for i, local in enumerate(later_local):
                tile_rows[h][local] = jnp.where(lvl_rows[local] == level, s[8 * i:8 * i + 8], tile_rows[h][local])

    early, late = (1, 0) if reverse else (0, 1)
    cb = cum_ref[_boundary_rows(GLA_LEVELS, reverse)[0]:_boundary_rows(GLA_LEVELS, reverse)[0] + 1, :]
    ql = (q[halves[late]] * jnp.exp(cum[halves[late]] - cb)).astype(BF16)
    ke = (kk[halves[early]] * jnp.exp(cb - cum[halves[early]])).astype(BF16)
    cross = lax.dot_general(ql, ke, NT_DIMS, preferred_element_type=F32)
    t_a, t_b = (jnp.concatenate(rows, axis=0) for rows in tile_rows)
    zero = jnp.zeros((hc, hc), F32)
    if reverse:
        scores = jnp.concatenate([jnp.concatenate([t_a, cross], axis=1), jnp.concatenate([zero, t_b], axis=1)], axis=0)
    else:
        scores = jnp.concatenate([jnp.concatenate([t_a, zero], axis=1), jnp.concatenate([cross, t_b], axis=1)], axis=0)

    st = st_ref[...]
    qe = (q * jnp.exp(cum)).astype(BF16)
    o = jnp.dot(scores.astype(BF16), v, preferred_element_type=F32)
    o = o + lax.dot_general(qe, st.astype(BF16), NT_DIMS, preferred_element_type=F32)
    o_ref[0, :, lanes] = o.astype(o_ref.dtype)
    last_row = 0 if reverse else c - 1
    last = cum_ref[last_row:last_row + 1, :]
    ke_all = (kk * jnp.exp(last - cum)).astype(BF16)
    st_ref[...] = st * jnp.exp(last) + lax.dot_general(v, ke_all, TN_DIMS, preferred_element_type=F32)


def _gla_kernel(mergeable_ref, qf_ref, vf_ref, lff_ref, qb_ref, vb_ref, lfb_ref, s0f_ref, s0b_ref,
                trif_ref, lvlf_ref, trib_ref, lvlb_ref,
                of_ref, ob_ref, sff_ref, sfb_ref, stf_ref, stb_ref, cumf_ref, cumb_ref):
    j = pl.program_id(2)

    @pl.when(j == 0)
    def _():
        stf_ref[...] = s0f_ref[0]
        stb_ref[...] = s0b_ref[0]

    def chunks(merged):
        for k in range(stf_ref.shape[0]):
            lanes = slice(k * HEAD_DIM, (k + 1) * HEAD_DIM)
            _gla_chunk(qf_ref, vf_ref, lff_ref, lanes, stf_ref.at[k], cumf_ref.at[k], trif_ref, lvlf_ref, of_ref,
                       False, merged)
            _gla_chunk(qb_ref, vb_ref, lfb_ref, lanes, stb_ref.at[k], cumb_ref.at[k], trib_ref, lvlb_ref, ob_ref,
                       True, merged)

    pl.when(mergeable_ref[0] > 0)(functools.partial(chunks, GLA_MERGED))
    pl.when(mergeable_ref[0] == 0)(functools.partial(chunks, 0))

    @pl.when(j == pl.num_programs(2) - 1)
    def _():
        sff_ref[0] = stf_ref[...]
        sfb_ref[0] = stb_ref[...]


def _gla_bidir(q, v, lf_f, lf_b, s0f, s0b, lb):
    b_, t_, d_ = q.shape
    h_ = d_ // HEAD_DIM
    hp = math.gcd(GLA_HEADS_PER_STEP, h_)
    c = GLA_CHUNK
    n = t_ // c
    worst = (2 ** GLA_MERGED - 1) * jnp.max(-jnp.log(lb))
    mergeable = jnp.where(worst < GLA_MAX_EXPONENT, 1, 0).astype(jnp.int32).reshape(1)
    fwd = lambda b, h, j, *_: (b, j, h)
    bwd = lambda b, h, j, *_: (b, n - 1 - j, h)
    st = lambda b, h, j, *_: (b, h, 0, 0)
    const = lambda b, h, j, *_: (0, 0)
    blk = lambda im: pl.BlockSpec((1, c, hp * HEAD_DIM), im)
    st_spec = pl.BlockSpec((1, hp, HEAD_DIM, HEAD_DIM), st)
    cspecs = [pl.BlockSpec((c, c), const), pl.BlockSpec((c // 2, c // 2), const)]
    state = pltpu.VMEM((hp, HEAD_DIM, HEAD_DIM), F32)
    cum = pltpu.VMEM((hp, c, HEAD_DIM), F32)
    return pl.pallas_call(
        _gla_kernel,
        grid_spec=pltpu.PrefetchScalarGridSpec(
            num_scalar_prefetch=1,
            grid=(b_, h_ // hp, n),
            in_specs=[blk(fwd), blk(fwd), blk(fwd), blk(bwd), blk(bwd), blk(bwd), st_spec, st_spec] + cspecs + cspecs,
            out_specs=[blk(fwd), blk(bwd), st_spec, st_spec],
            scratch_shapes=[state, state, cum, cum],
        ),
        out_shape=[jax.ShapeDtypeStruct((b_, t_, d_), BF16)] * 2
                  + [jax.ShapeDtypeStruct((b_, h_, HEAD_DIM, HEAD_DIM), F32)] * 2,
        compiler_params=_params("arbitrary", "arbitrary", "arbitrary"),
        name="gla_bidir",
    )(mergeable, q, v, lf_f, q, v, lf_b, s0f, s0b, *_gla_consts(False), *_gla_consts(True))


def _mixer_epilogue(y, rows, x_ref, w_ref, gate_ref, gain_ref, shift_ref, scale_ref, wr_ref, x1_ref, h_ref, aff_ref):
    y = jnp.dot(y.astype(BF16), w_ref[...], preferred_element_type=F32)
    x1 = x_ref[0, rows] + gate_ref[0] * y
    x1_ref[0, rows] = x1
    hf = _modulate(x1, gain_ref[...], shift_ref[0], scale_ref[0])
    h_ref[0, rows] = hf.astype(BF16)
    logits = _dot_bf16x3(wr_ref[...], hf, NT_DIMS)
    e = jnp.exp(logits - jnp.max(logits, axis=0, keepdims=True))
    aff_ref[0, :, rows] = e / jnp.sum(e, axis=0, keepdims=True)


def _hgrn_out_kernel(of_ref, ob_ref, g_ref, x_ref, hnorm_ref, w_ref, *rest):
    hn = hnorm_ref[...]
    for rows in _sub_tiles(x_ref.shape[1]):
        o = of_ref[0, rows].astype(F32) + ob_ref[0, rows].astype(F32)
        heads = []
        for h in range(o.shape[-1] // HEAD_DIM):
            oh = o[:, h * HEAD_DIM:(h + 1) * HEAD_DIM]
            heads.append(oh * lax.rsqrt(jnp.mean(oh * oh, axis=-1, keepdims=True) + EPS) * hn)
        y = jnp.concatenate(heads, axis=-1) * _silu(g_ref[0, rows].astype(F32))
        _mixer_epilogue(y, rows, x_ref, w_ref, *rest)


def _conv_mixer_kernel(prev_ref, next_ref, x_ref, shift1_ref, scale1_ref, gain1_ref, win_ref, wc_ref, w_ref, *rest):
    i = pl.program_id(1)
    tm, d_ = x_ref.shape[1], x_ref.shape[2]
    subs = _sub_tiles(tm)
    halo = jnp.concatenate([prev_ref[0], next_ref[0]], axis=0)
    b_gate, cu = [], []
    for k, rows in enumerate(subs):
        xs = x_ref[0, rows]
        if k == 0:
            xs = jnp.concatenate([xs, halo], axis=0)
        h = _modulate(xs, gain1_ref[...], shift1_ref[0], scale1_ref[0]).astype(BF16)
        part = lambda p, hh: jnp.dot(hh, win_ref[:, p * d_:(p + 1) * d_], preferred_element_type=F32)
        n = rows.stop - rows.start
        b_gate.append(part(0, h[:n]))
        cu.append(part(1, h) * part(2, h))
    cu_halo = cu[0][subs[0].stop - subs[0].start:]
    cu = jnp.concatenate([cu[0][:subs[0].stop - subs[0].start]] + cu[1:], axis=0)
    rid = lax.broadcasted_iota(jnp.int32, cu.shape, 0)
    before = jnp.where(i == 0, 0.0, cu_halo[7:8])
    after = jnp.where(i == pl.num_programs(1) - 1, 0.0, cu_halo[8:9])
    left = jnp.where(rid == 0, before, pltpu.roll(cu, 1, 0))
    right = jnp.where(rid == tm - 1, after, pltpu.roll(cu, tm - 1, 0))
    wc = wc_ref[...]
    conv = left * wc[0:1] + cu * wc[1:2] + right * wc[2:3]
    for k, rows in enumerate(subs):
        _mixer_epilogue(b_gate[k] * conv[rows], rows, x_ref, w_ref, *rest)


def _mixer_out(kernel, name, row_args, halo_args, x, sample_args, small_args, gate, gain, shift, scale, w_router_t):
    b_, t_, d_ = x.shape
    e_ = w_router_t.shape[0]
    tm = min(MIXER_ROW_TILE, t_)
    row = pl.BlockSpec((1, tm, d_), lambda b, i: (b, i, 0))
    per_sample = pl.BlockSpec((1, 1, d_), lambda b, i: (b, 0, 0))
    whole = lambda a: pl.BlockSpec(a.shape, lambda b, i: (0,) * a.ndim)
    n8 = t_ // 8
    halo_specs = [pl.BlockSpec((1, 8, d_), lambda b, i: (b, jnp.maximum(i * (tm // 8) - 1, 0), 0)),
                  pl.BlockSpec((1, 8, d_), lambda b, i: (b, jnp.minimum((i + 1) * (tm // 8), n8 - 1), 0))]
    return pl.pallas_call(
        kernel,
        grid=(b_, t_ // tm),
        in_specs=[row] * len(row_args) + halo_specs[:len(halo_args)] + [row] + [per_sample] * len(sample_args)
                 + [whole(a) for a in small_args] + [per_sample, whole(gain), per_sample, per_sample, whole(w_router_t)],
        out_specs=[row, row, pl.BlockSpec((1, e_, tm), lambda b, i: (b, 0, i))],
        out_shape=[jax.ShapeDtypeStruct((b_, t_, d_), F32), jax.ShapeDtypeStruct((b_, t_, d_), BF16),
                   jax.ShapeDtypeStruct((b_, e_, t_), F32)],
        compiler_params=_params("arbitrary", "arbitrary", vmem=V7X_VMEM_LIMIT),
        name=name,
    )(*row_args, *halo_args, x, *sample_args, *small_args, gate, gain, shift, scale, w_router_t)


def _route_kernel(aff_ref, tri_ref, blockind_ref, slot_ref, base_ref, dense_ref, *, cap, n_experts):
    aff = aff_ref[...]
    e_, t_ = aff.shape

    def as_float(word):
        return pltpu.bitcast(word, F32)

    def count_ge(th):
        return jnp.sum(jnp.where(aff >= th, 1.0, 0.0), axis=1, keepdims=True)

    def search(_, carry):
        lo, hi = carry
        mid = lo + ((hi - lo + 1) >> 1)
        ok = count_ge(as_float(mid)) >= cap
        return jnp.where(ok, mid, lo), jnp.where(ok, hi, mid - 1)

    lo0 = jnp.zeros((e_, 1), jnp.int32)
    hi0 = jnp.full((e_, 1), 0x7F7FFFFF, jnp.int32)
    kth, _ = lax.fori_loop(0, 32, search, (lo0, hi0))
    above = aff >= as_float(kth + 1)
    tied = jnp.logical_and(aff >= as_float(kth), jnp.logical_not(above))
    need = cap - jnp.sum(jnp.where(above, 1.0, 0.0), axis=1, keepdims=True)
    tri = tri_ref[...]
    tb = tri.shape[0]
    carry_t = jnp.zeros((e_, 1), F32)
    carry_s = jnp.zeros((e_, 1), F32)
    sel_blocks = []
    for j in range(t_ // tb):
        cols = slice(j * tb, (j + 1) * tb)
        tied_j = tied[:, cols]
        ct = jnp.dot(jnp.where(tied_j, 1.0, 0.0).astype(BF16), tri, preferred_element_type=F32) + carry_t
        carry_t = ct[:, tb - 1:tb]
        sel_j = jnp.where(above[:, cols], 1.0, jnp.where(tied_j & (ct <= need), 1.0, 0.0))
        cs = jnp.dot(sel_j.astype(BF16), tri, preferred_element_type=F32) + carry_s
        carry_s = cs[:, tb - 1:tb]
        slot_ref[:, cols] = jnp.where(sel_j > 0.0, cs - 1.0, -1.0).astype(jnp.int32)
        sel_blocks.append(sel_j.astype(BF16))
    sel = jnp.concatenate(sel_blocks, axis=1)
    counts = jnp.dot(sel, blockind_ref[...], preferred_element_type=F32)
    base, end = counts[:, :128], counts[:, 128:]
    base_ref[...] = base.astype(jnp.int32)
    span = end - jnp.floor(base * (1.0 / SLOT_ALIGN)) * SLOT_ALIGN
    for b in range(e_ // n_experts):
        widest = jnp.max(span[b * n_experts:(b + 1) * n_experts], axis=0, keepdims=True)
        dense_ref[b] = sum(jnp.where(widest <= w, 1, 0) for w in DENSE_WINDOWS).astype(jnp.int32)


def _route(aff):
    b_, e_, t_ = aff.shape
    cap = EC_CAPACITY_FACTOR * t_ // e_
    tb = TOKEN_BLOCK
    nb = t_ // tb
    tri = jnp.asarray(np.triu(np.ones((tb, tb), np.float32)), BF16)
    tok, col = np.arange(t_)[:, None], np.arange(128)[None, :]
    blockind = np.concatenate([(tok < col * tb) & (col <= nb), (tok < (col + 1) * tb) & (col < nb)], axis=1)
    slot, base, dense = pl.pallas_call(
        functools.partial(_route_kernel, cap=cap, n_experts=e_),
        out_shape=[jax.ShapeDtypeStruct((b_ * e_, t_), jnp.int32), jax.ShapeDtypeStruct((b_ * e_, 128), jnp.int32),
                   jax.ShapeDtypeStruct((b_, 1, 128), jnp.int32)],
        name="route",
    )(aff.reshape(b_ * e_, t_), tri, jnp.asarray(blockind.astype(np.float32), BF16))
    return slot.reshape(b_, e_, t_), base[:, :nb + 1].reshape(-1), dense[:, 0, :nb].reshape(-1)


def _window_plan(tbl_ref, b, e, tb, ne, nb):
    idx = (b * ne + e) * (nb + 1) + tb
    base, end = tbl_ref[idx], tbl_ref[idx + 1]
    start = _align_down(base)
    n_win = jnp.where(end > base, (end - start + SLOT_WINDOW - 1) >> (SLOT_WINDOW.bit_length() - 1), 0)
    return start, n_win


def _align_down(slot):
    shift = SLOT_ALIGN.bit_length() - 1
    return (slot >> shift) << shift


def _dense_start(tbl_ref, b, e, tb, ne, nb, cap, window):
    base = tbl_ref[(b * ne + e) * (nb + 1) + tb]
    return pl.multiple_of(jnp.minimum(_align_down(base), cap - window), SLOT_ALIGN)


def _gather_kernel(tbl_ref, dense_ref, h_ref, slot_ref, xg_ref, *, ne, nb, cap):
    b, step = pl.program_id(0), pl.program_id(2)
    n_tok = TOKEN_BLOCK

    @pl.when(step == 0)
    def _():
        xg_ref[...] = jnp.zeros(xg_ref.shape, xg_ref.dtype)

    def dense_block(sub, window):
        tb = step * BLOCKS_PER_STEP + sub
        toks = slice(sub * n_tok, (sub + 1) * n_tok)
        starts = [_dense_start(tbl_ref, b, e, tb, ne, nb, cap, window) for e in range(ne)]
        ids = lax.broadcasted_iota(jnp.int32, (window, n_tok), 0)
        onehot = jnp.concatenate(
            [jnp.where(ids == slot_ref[0, e:e + 1, toks] - starts[e], 1.0, 0.0).astype(BF16) for e in range(ne)],
            axis=0)
        rows = jnp.dot(onehot, h_ref[0, toks, :], preferred_element_type=F32).astype(BF16)
        for e in range(ne):
            win = xg_ref.at[0, e, pl.ds(starts[e], window), :]
            win[...] = win[...] + rows[e * window:(e + 1) * window]

    def windowed_block(sub):
        tb = step * BLOCKS_PER_STEP + sub
        toks = slice(sub * n_tok, (sub + 1) * n_tok)
        h = h_ref[0, toks, :]
        ids0 = lax.broadcasted_iota(jnp.int32, (SLOT_WINDOW, n_tok), 0)
        for e in range(ne):
            srow = slot_ref[0, e:e + 1, toks]
            start, n_win = _window_plan(tbl_ref, b, e, tb, ne, nb)

            def body(k, carry, e=e, srow=srow, start=start):
                lo = start + k * SLOT_WINDOW
                w0 = pl.multiple_of(jnp.minimum(lo, cap - SLOT_WINDOW), SLOT_ALIGN)
                ids = w0 + ids0
                onehot = jnp.where(ids >= lo, jnp.where(ids == srow, 1.0, 0.0), 0.0).astype(BF16)
                rows = jnp.dot(onehot, h, preferred_element_type=F32)
                win = xg_ref.at[0, e, pl.ds(w0, SLOT_WINDOW), :]
                win[...] = win[...] + rows.astype(BF16)
                return carry

            lax.fori_loop(0, n_win, body, 0)

    _per_block_dispatch(dense_ref, b * nb + step * BLOCKS_PER_STEP, BLOCKS_PER_STEP, dense_block, windowed_block)


def _per_block_dispatch(dense_ref, first, n_blocks, dense_block, windowed_block):
    fits = [dense_ref[first + sub] for sub in range(n_blocks)]
    common = functools.reduce(jnp.minimum, fits)
    n_tiers = len(DENSE_WINDOWS)
    for k, window in enumerate(DENSE_WINDOWS):
        chosen = (common >= n_tiers) if k == 0 else (common == n_tiers - k)

        @pl.when(chosen)
        def _(window=window):
            for sub in range(n_blocks):
                dense_block(sub, window)

    @pl.when(common == 0)
    def _():
        for sub in range(n_blocks):
            pl.when(fits[sub] > 0)(functools.partial(dense_block, sub, DENSE_WINDOWS[-1]))
            pl.when(fits[sub] == 0)(functools.partial(windowed_block, sub))


def _gather(h, slot, tbl, dense):
    b_, t_, d_ = h.shape
    e_ = slot.shape[1]
    cap = EC_CAPACITY_FACTOR * t_ // e_
    tb = TOKEN_BLOCK * BLOCKS_PER_STEP
    nb = t_ // TOKEN_BLOCK
    dh = d_ // 2
    return pl.pallas_call(
        functools.partial(_gather_kernel, ne=e_, nb=nb, cap=cap),
        grid_spec=pltpu.PrefetchScalarGridSpec(
            num_scalar_prefetch=2,
            grid=(b_, 2, t_ // tb),
            in_specs=[pl.BlockSpec((1, tb, dh), lambda b, c, i, *_: (b, i, c)),
                      pl.BlockSpec((1, e_, tb), lambda b, c, i, *_: (b, 0, i))],
            out_specs=pl.BlockSpec((1, e_, cap, dh), lambda b, c, i, *_: (b, 0, 0, c)),
        ),
        out_shape=jax.ShapeDtypeStruct((b_, e_, cap, d_), BF16),
        compiler_params=_params("arbitrary", "arbitrary", "arbitrary", vmem=V7X_VMEM_LIMIT),
        name="moe_gather",
    )(tbl, dense, h, slot)


def _expert_kernel(xg_ref, wg_ref, wu_ref, wd_ref, y_ref, acc_ref):
    f = pl.program_id(1)
    n_b, _, cap, _ = xg_ref.shape
    wg = wg_ref[0, 0].astype(BF16)
    wu = wu_ref[0, 0].astype(BF16)
    wd = wd_ref[0, 0].astype(BF16)

    @pl.when(jnp.logical_and(pl.program_id(0) == 0, f == 0))
    def _():
        acc_ref[...] = jnp.zeros(acc_ref.shape, F32)

    for b in range(n_b):
        for r in range(cap // FFN_ROWS):
            rows = pl.ds(r * FFN_ROWS, FFN_ROWS)
            acc_rows = pl.ds((b * cap) + r * FFN_ROWS, FFN_ROWS)
            xr = xg_ref[b, 0, rows, :]
            a = jnp.dot(xr, wg, preferred_element_type=F32)
            u = jnp.dot(xr, wu, preferred_element_type=F32)
            part = jnp.dot((_silu(a) * u).astype(BF16), wd, preferred_element_type=F32)
            total = jnp.where(f == 0, 0.0, acc_ref[acc_rows, :]) + part
            acc_ref[acc_rows, :] = total
            y_ref[b, 0, rows, :] = total.astype(BF16)


def _experts(xg, layer, w_gate, w_up, w_down):
    b_, e_, cap, d_ = xg.shape
    f_ = w_gate.shape[-1]
    ft = min(FFN_TILE, f_)
    return pl.pallas_call(
        _expert_kernel,
        grid=(e_, f_ // ft),
        in_specs=[pl.BlockSpec((b_, 1, cap, d_), lambda e, f: (0, e, 0, 0)),
                  pl.BlockSpec((1, 1, d_, ft), lambda e, f: (layer, e, 0, f)),
                  pl.BlockSpec((1, 1, d_, ft), lambda e, f: (layer, e, 0, f)),
                  pl.BlockSpec((1, 1, ft, d_), lambda e, f: (layer, e, f, 0))],
        out_specs=pl.BlockSpec((b_, 1, cap, d_), lambda e, f: (0, e, 0, 0)),
        out_shape=jax.ShapeDtypeStruct((b_, e_, cap, d_), BF16),
        scratch_shapes=[pltpu.VMEM((b_ * cap, d_), F32)],
        compiler_params=_params("arbitrary", "arbitrary", vmem=V7X_VMEM_LIMIT),
        name="moe_experts",
    )(xg, w_gate, w_up, w_down)


def _combine_kernel(tbl_ref, dense_ref, y_ref, slot_ref, aff_ref, x_ref, gate_ref, *rest, ne, nb, cap, final):
    if final:
        gain_ref, out_ref, acc_ref = rest
    else:
        out_ref, acc_ref = rest
    b, step = pl.program_id(0), pl.program_id(1)
    n_tok = TOKEN_BLOCK

    def finish(toks, moe):
        out = x_ref[0, toks, :] + gate_ref[0] * moe
        if final:
            out = out * lax.rsqrt(jnp.mean(out * out, axis=-1, keepdims=True) + EPS) * gain_ref[...]
        out_ref[0, toks, :] = out

    def dense_block(sub, window):
        tb = step * COMBINE_BLOCKS + sub
        toks = slice(sub * n_tok, (sub + 1) * n_tok)
        starts = [_dense_start(tbl_ref, b, e, tb, ne, nb, cap, window) for e in range(ne)]
        ids = lax.broadcasted_iota(jnp.int32, (window, n_tok), 0)
        weights = jnp.concatenate(
            [jnp.where(ids == slot_ref[0, e:e + 1, toks] - starts[e], aff_ref[0, e:e + 1, toks], 0.0).astype(BF16)
             for e in range(ne)], axis=0)
        yw = jnp.concatenate([y_ref[0, e, pl.ds(starts[e], window), :] for e in range(ne)], axis=0)
        finish(toks, lax.dot_general(weights, yw, TN_DIMS, preferred_element_type=F32))

    def windowed_block(sub):
        tb = step * COMBINE_BLOCKS + sub
        toks = slice(sub * n_tok, (sub + 1) * n_tok)
        ids0 = lax.broadcasted_iota(jnp.int32, (SLOT_WINDOW, n_tok), 0)
        acc_ref[...] = jnp.zeros(acc_ref.shape, F32)
        for e in range(ne):
            srow = slot_ref[0, e:e + 1, toks]
            grow = aff_ref[0, e:e + 1, toks]
            start, n_win = _window_plan(tbl_ref, b, e, tb, ne, nb)

            def body(k, carry, e=e, srow=srow, grow=grow, start=start):
                lo = start + k * SLOT_WINDOW
                w0 = pl.multiple_of(jnp.minimum(lo, cap - SLOT_WINDOW), SLOT_ALIGN)
                ids = w0 + ids0
                weights = jnp.where(ids >= lo, jnp.where(ids == srow, grow, 0.0), 0.0).astype(BF16)
                yw = y_ref[0, e, pl.ds(w0, SLOT_WINDOW), :]
                acc_ref[...] = acc_ref[...] + lax.dot_general(weights, yw, TN_DIMS, preferred_element_type=F32)
                return carry

            lax.fori_loop(0, n_win, body, 0)
        finish(toks, acc_ref[...])

    _per_block_dispatch(dense_ref, b * nb + step * COMBINE_BLOCKS, COMBINE_BLOCKS, dense_block, windowed_block)


def _combine(y, slot, aff, tbl, dense, x, gate, final_gain):
    b_, t_, d_ = x.shape
    e_, cap = y.shape[1], y.shape[2]
    tb = TOKEN_BLOCK * COMBINE_BLOCKS
    nb = t_ // TOKEN_BLOCK
    final = final_gain is not None
    route_spec = pl.BlockSpec((1, e_, tb), lambda b, i, *_: (b, 0, i))
    row = pl.BlockSpec((1, tb, d_), lambda b, i, *_: (b, i, 0))
    in_specs = [pl.BlockSpec((1, e_, cap, d_), lambda b, i, *_: (b, 0, 0, 0), pipeline_mode=pl.Buffered(1)),
                route_spec, route_spec, row, pl.BlockSpec((1, 1, d_), lambda b, i, *_: (b, 0, 0))]
    args = [tbl, dense, y, slot, aff, x, gate]
    if final:
        in_specs.append(pl.BlockSpec((1, d_), lambda b, i, *_: (0, 0)))
        args.append(final_gain)
    return pl.pallas_call(
        functools.partial(_combine_kernel, ne=e_, nb=nb, cap=cap, final=final),
        grid_spec=pltpu.PrefetchScalarGridSpec(
            num_scalar_prefetch=2,
            grid=(b_, t_ // tb),
            in_specs=in_specs,
            out_specs=row,
            scratch_shapes=[pltpu.VMEM((TOKEN_BLOCK, d_), F32)],
        ),
        out_shape=jax.ShapeDtypeStruct((b_, t_, d_), F32),
        compiler_params=_params("arbitrary", "arbitrary", vmem=V7X_VMEM_LIMIT),
        name="moe_combine_final" if final else "moe_combine",
    )(*args)


def _moe(x, h, aff, gate, layer, w_gate, w_up, w_down, final_gain=None):
    slot, tbl, dense = _route(aff)
    xg = _gather(h, slot, tbl, dense)
    y = _experts(xg, layer, w_gate, w_up, w_down)
    return _combine(y, slot, aff, tbl, dense, x, gate, final_gain)


def kernel(x, c, ctx, c_ctx, ada_w, ada_b, norm_mix, norm_ffn, norm_final, hg_w_in, hg_lb_logits, hg_norm, hg_w_out, sc_w_in, sc_conv, sc_w_out, moe_router, moe_w_gate, moe_w_up, moe_w_down):
    b_, t_, d_ = x.shape
    depth = ada_w.shape[0]
    n_ada = ada_w.shape[-1] // d_
    n_heads = d_ // HEAD_DIM
    n_experts = moe_router.shape[-1]
    cap = EC_CAPACITY_FACTOR * t_ // n_experts
    assert depth == 2 and n_ada == 6 and b_ + 1 <= 8
    assert t_ % GLA_CHUNK == 0 and ctx.shape[1] % GLA_CHUNK == 0 and IN_PROJ_SUB_ROWS % GRID_W == 0
    assert d_ % (2 * HEAD_DIM) == 0 and t_ % min(ROW_TILE, t_) == 0 and t_ % min(MIXER_ROW_TILE, t_) == 0
    assert t_ % (TOKEN_BLOCK * BLOCKS_PER_STEP) == 0 and t_ % (TOKEN_BLOCK * COMBINE_BLOCKS) == 0
    assert cap % SLOT_WINDOW == 0 and cap >= max(DENSE_WINDOWS) and cap % min(FFN_ROWS, cap) == 0
    assert moe_w_gate.shape[-1] % min(FFN_TILE, moe_w_gate.shape[-1]) == 0

    cond = jnp.concatenate([c, c_ctx[None], jnp.zeros((8 - b_ - 1, d_), F32)], axis=0)
    mod = _ada_vectors(cond, ada_w, ada_b, n_ada)
    vec = lambda i, j: mod[i, j, :b_][:, None, :]
    cvec = lambda i, j: mod[i, j, b_][None, None, :]
    rowtab, coltab, lower = _tables(hg_lb_logits, t_)
    row_of = lambda a, i: a[i][None, :]
    router_t = lambda i: jnp.swapaxes(moe_router[i], 0, 1)

    w_in = hg_w_in[0].astype(BF16)
    lb0 = row_of(lower, 0)
    gain0 = row_of(norm_mix, 0)
    qc, vc, lfc_f, lfc_b = _hgrn_in(ctx, None, gain0, cvec(0, 0), cvec(0, 1), lb0, w_in, with_gate=False)
    zeros = jnp.zeros((b_, n_heads, HEAD_DIM, HEAD_DIM), F32)
    _, _, s_f, s_b = _gla_bidir(qc, vc, lfc_f, lfc_b, zeros, zeros, lb0)
    x0, q, v, lf_f, lf_b, g = _hgrn_in(x, (rowtab, coltab), gain0, vec(0, 0), vec(0, 1), lb0, w_in, with_gate=True)
    o_f, o_b, _, _ = _gla_bidir(q, v, lf_f, lf_b, s_f, s_b, lb0)
    x1, h, aff = _mixer_out(_hgrn_out_kernel, "hgrn_out", [o_f, o_b, g], [], x0, [],
                            [row_of(hg_norm, 0), hg_w_out[0].astype(BF16)],
                            vec(0, 2), row_of(norm_ffn, 0), vec(0, 3), vec(0, 4), router_t(0))
    x2 = _moe(x1, h, aff, vec(0, 5), 0, moe_w_gate, moe_w_up, moe_w_down)

    x3, h, aff = _mixer_out(_conv_mixer_kernel, "conv_mixer", [], [x2, x2], x2, [vec(1, 0), vec(1, 1)],
                            [row_of(norm_mix, 1), sc_w_in[0].astype(BF16), sc_conv[0], sc_w_out[0].astype(BF16)],
                            vec(1, 2), row_of(norm_ffn, 1), vec(1, 3), vec(1, 4), router_t(1))
    return _moe(x3, h, aff, vec(1, 5), 1, moe_w_gate, moe_w_up, moe_w_down, final_gain=norm_final[None, :])
```

```python
import functools
import math

import numpy as np
import jax
import jax.numpy as jnp
from jax import lax
from jax.experimental import pallas as pl
from jax.experimental.pallas import tpu as pltpu

F32 = jnp.float32
BF16 = jnp.bfloat16

EPS = 1e-6
POS_TEMP = 10000.0
GRID_W = 64
HEAD_DIM = 128
EC_CAPACITY_FACTOR = 2
GLA_CHUNK = 256
GLA_LEVELS = 8
GLA_HEADS_PER_STEP = 8
GLA_MERGED = 5
GLA_MAX_EXPONENT = 80.0
ROW_TILE = 512
MIXER_ROW_TILE = 1024
SUB_ROWS = 1024
IN_PROJ_SUB_ROWS = 128
TOKEN_BLOCK = 256
BLOCKS_PER_STEP = 4
COMBINE_BLOCKS = 2
SLOT_WINDOW = 64
DENSE_WINDOWS = (96, 128)
SLOT_ALIGN = 16
FFN_TILE = 1024
FFN_ROWS = 256
V7X_VMEM_LIMIT = 56 * 1024 * 1024

NT_DIMS = (((1,), (1,)), ((), ()))
TN_DIMS = (((0,), (0,)), ((), ()))


def _params(*sem, vmem=None):
    return pltpu.CompilerParams(dimension_semantics=sem, vmem_limit_bytes=vmem)


def _dot_bf16x3(a, b, dims):
    a0, b0 = a.astype(BF16), b.astype(BF16)
    a1 = (a - a0.astype(F32)).astype(BF16)
    b1 = (b - b0.astype(F32)).astype(BF16)
    d = lambda x, y: lax.dot_general(x, y, dims, preferred_element_type=F32)
    return (d(a0, b1) + d(a1, b0)) + d(a0, b0)


def _sub_tiles(n_rows, sub_rows=SUB_ROWS):
    sub = min(sub_rows, n_rows)
    return [slice(s, s + sub) for s in range(0, n_rows, sub)]


def _sigmoid(x):
    return 1.0 / (1.0 + jnp.exp(-x))


def _silu(x):
    return x * _sigmoid(x)


def _modulate(x, gain, shift, scale):
    y = x * lax.rsqrt(jnp.mean(x * x, axis=-1, keepdims=True) + EPS)
    return y * (gain * (1.0 + scale)) + shift


def _ada_kernel(cond_ref, w_ref, b_ref, out_ref):
    s = _silu(cond_ref[...])
    out_ref[0, 0] = _dot_bf16x3(s, w_ref[0], (((1,), (0,)), ((), ()))) + b_ref[0, 0]


def _ada_vectors(cond, ada_w, ada_b, n_ada):
    depth, d_, _ = ada_w.shape
    return pl.pallas_call(
        _ada_kernel,
        grid=(depth, n_ada),
        in_specs=[pl.BlockSpec((8, d_), lambda i, j: (0, 0)),
                  pl.BlockSpec((1, d_, d_), lambda i, j: (i, 0, j)),
                  pl.BlockSpec((1, 1, 1, d_), lambda i, j: (i, j, 0, 0))],
        out_specs=pl.BlockSpec((1, 1, 8, d_), lambda i, j: (i, j, 0, 0)),
        out_shape=jax.ShapeDtypeStruct((depth, n_ada, 8, d_), F32),
        compiler_params=_params("arbitrary", "arbitrary"),
        name="ada_vectors",
    )(cond, ada_w, ada_b.reshape(depth, n_ada, 1, d_))


def _tables_kernel(lb_logits_ref, rowtab_ref, coltab_ref, lb_ref, *, n_freq):
    def table(n_pos):
        p = lax.broadcasted_iota(jnp.int32, (n_pos, n_freq), 0).astype(F32)
        j = lax.broadcasted_iota(jnp.int32, (n_pos, n_freq), 1).astype(F32)
        omega = jnp.exp(j * (-math.log(POS_TEMP) / n_freq))
        ang = p * omega
        return jnp.concatenate([jnp.sin(ang), jnp.cos(ang)], axis=-1)

    rowtab_ref[...] = table(rowtab_ref.shape[0])
    coltab_ref[...] = table(coltab_ref.shape[0])
    logits = lb_logits_ref[...]
    e = jnp.exp(logits - jnp.max(logits, axis=0, keepdims=True))
    sm = e / jnp.sum(e, axis=0, keepdims=True)
    acc = sm[0:1]
    lb_ref[0:1] = acc
    for i in range(1, lb_ref.shape[0]):
        acc = acc + sm[i:i + 1]
        lb_ref[i:i + 1] = acc


def _tables(lb_logits, n_tokens):
    n_lb, d_ = lb_logits.shape
    n_freq = d_ // 4
    rows = n_tokens // GRID_W
    return pl.pallas_call(
        functools.partial(_tables_kernel, n_freq=n_freq),
        out_shape=[jax.ShapeDtypeStruct((rows, 2 * n_freq), F32),
                   jax.ShapeDtypeStruct((GRID_W, 2 * n_freq), F32),
                   jax.ShapeDtypeStruct((n_lb, d_), F32)],
        name="pos_tables",
    )(lb_logits)


def _hgrn_in_kernel(*refs, with_pos, with_gate):
    it = iter(refs)
    x_ref = next(it)
    if with_pos:
        rowtab_ref, coltab_ref = next(it), next(it)
    gain_ref, shift_ref, scale_ref, lb_ref, w_ref = next(it), next(it), next(it), next(it), next(it)
    if with_pos:
        x0_ref = next(it)
    q_ref, v_ref, lff_ref, lfb_ref = next(it), next(it), next(it), next(it)
    g_ref = next(it) if with_gate else None

    d_ = x_ref.shape[-1]
    lb = lb_ref[...]
    for rows in _sub_tiles(x_ref.shape[1], IN_PROJ_SUB_ROWS):
        x = x_ref[0, rows]
        if with_pos:
            grid_rows = range(rows.start // GRID_W, rows.stop // GRID_W)
            pos_row = jnp.concatenate(
                [jnp.broadcast_to(rowtab_ref[0, r:r + 1, :], (GRID_W, rowtab_ref.shape[-1])) for r in grid_rows], axis=0)
            pos_col = jnp.concatenate([coltab_ref[...]] * len(grid_rows), axis=0)
            x = x + jnp.concatenate([pos_row, pos_col], axis=-1)
            x0_ref[0, rows] = x
        h = _modulate(x, gain_ref[...], shift_ref[0], scale_ref[0]).astype(BF16)
        part = lambda p: jnp.dot(h, w_ref[:, p * d_:(p + 1) * d_], preferred_element_type=F32)
        lff_ref[0, rows] = jnp.log(lb + (1.0 - lb) * _sigmoid(part(2)))
        lfb_ref[0, rows] = jnp.log(lb + (1.0 - lb) * _sigmoid(part(3)))
        q_ref[0, rows] = (part(0) * HEAD_DIM ** -0.5).astype(BF16)
        v_ref[0, rows] = part(1).astype(BF16)
        if with_gate:
            g_ref[0, rows] = part(4).astype(BF16)


def _hgrn_in(x, tabs, gain, shift, scale, lb, w_in, *, with_gate):
    b_, t_, d_ = x.shape
    tm = min(ROW_TILE, t_)
    with_pos = tabs is not None
    per_sample = lambda a: pl.BlockSpec((1, 1, d_), (lambda b, i: (b, 0, 0)) if a.shape[0] > 1 else (lambda b, i: (0, 0, 0)))
    row = pl.BlockSpec((1, tm, d_), lambda b, i: (b, i, 0))
    vec = pl.BlockSpec((1, d_), lambda b, i: (0, 0))
    args, in_specs = [x], [row]
    if with_pos:
        rowtab, coltab = tabs
        rows_per_tile = tm // GRID_W
        args += [rowtab.reshape(rowtab.shape[0] // rows_per_tile, rows_per_tile, rowtab.shape[1]), coltab]
        in_specs += [pl.BlockSpec((1, rows_per_tile, rowtab.shape[1]), lambda b, i: (i, 0, 0)),
                     pl.BlockSpec(coltab.shape, lambda b, i: (0, 0))]
    args += [gain, shift, scale, lb, w_in]
    in_specs += [vec, per_sample(shift), per_sample(scale), vec, pl.BlockSpec(w_in.shape, lambda b, i: (0, 0))]
    out_shape, out_specs = [], []
    if with_pos:
        out_shape.append(jax.ShapeDtypeStruct((b_, t_, d_), F32))
        out_specs.append(row)
    out_shape += [jax.ShapeDtypeStruct((b_, t_, d_), BF16)] * 2 + [jax.ShapeDtypeStruct((b_, t_, d_), F32)] * 2
    out_specs += [row] * 4
    if with_gate:
        out_shape.append(jax.ShapeDtypeStruct((b_, t_, d_), BF16))
        out_specs.append(row)
    return pl.pallas_call(
        functools.partial(_hgrn_in_kernel, with_pos=with_pos, with_gate=with_gate),
        grid=(b_, t_ // tm),
        in_specs=in_specs, out_specs=out_specs, out_shape=out_shape,
        compiler_params=_params("arbitrary", "arbitrary", vmem=V7X_VMEM_LIMIT),
        name="hgrn_in_latent" if with_pos else "hgrn_in_context",
    )(*args)


def _gla_consts(reverse):
    c = GLA_CHUNK
    idx = np.arange(c)
    rank = (c - 1 - idx) if reverse else idx
    tri = (rank[None, :] <= rank[:, None]).astype(np.float32)
    hc = c // 2
    hrank = rank[:hc] - rank[:hc].min()
    lvl = np.full((hc, hc), -1, np.int32)
    rt, rs = hrank[:, None], hrank[None, :]
    lvl[rt == rs] = 0
    for level in range(1, GLA_LEVELS):
        blk, half = 1 << level, 1 << (level - 1)
        lvl[(rt // blk == rs // blk) & ((rt % blk) >= half) & ((rs % blk) < half)] = level
    return jnp.asarray(tri, BF16), jnp.asarray(lvl)


def _later_group(level, reverse, group):
    rank = (GLA_CHUNK - 1 - 8 * group) if reverse else 8 * group
    return (rank % (1 << level)) >= (1 << (level - 1))


def _boundary_rows(level, reverse):
    c = GLA_CHUNK
    blk, half = 1 << level, 1 << (level - 1)
    rows = []
    for i in range(c):
        rank = (c - 1 - i) if reverse else i
        brank = (rank // blk) * blk + half - 1
        rows.append((c - 1 - brank) if reverse else brank)
    return rows


def _gla_low_levels(q, kk, lf, cum, lvl, bcast, halves, reverse):
    ng = q.shape[0] // 8
    qb, kb = q.astype(BF16), kk.astype(BF16)
    tiles = [jnp.where(lvl == 0, lax.dot_general(qb[hs], kb[hs], NT_DIMS, preferred_element_type=F32), 0.0)
             for hs in halves]
    row8 = lax.broadcasted_iota(jnp.int32, (8, HEAD_DIM), 0)
    rank8 = (7 - row8) if reverse else row8
    for level in range(1, 4):
        sgn8 = jnp.where(((rank8 >> (level - 1)) & 1) == 1, 1.0, -1.0)
        sgn = jnp.concatenate([sgn8] * ng, axis=0)
        later = sgn > 0.0
        if level == 1:
            g = jnp.where(later, lf, 0.0)
        else:
            brow = _boundary_rows(level, reverse)
            pieces = []
            for grp in range(ng):
                first = bcast(brow[8 * grp])
                pieces.append(jnp.where(row8 < 4, first, bcast(brow[8 * grp + 7])) if level == 2 else first)
            g = (cum - jnp.concatenate(pieces, axis=0)) * sgn
        xe = (jnp.where(later, q, kk) * jnp.exp(g)).astype(BF16)
        for h, hs in enumerate(halves):
            s = lax.dot_general(xe[hs], xe[hs], NT_DIMS, preferred_element_type=F32)
            tiles[h] = jnp.where(lvl == level, s, tiles[h])
    return tiles


def _gla_chunk(q_ref, v_ref, lf_ref, lanes, st_ref, cum_ref, tri_ref, lvl_ref, o_ref, reverse, merged):
    c = GLA_CHUNK
    hc, ng = c // 2, c // 8
    halves = (slice(0, hc), slice(hc, c))
    lf = lf_ref[0, :, lanes]
    q = q_ref[0, :, lanes].astype(F32)
    v = v_ref[0, :, lanes]
    kk = 1.0 - jnp.exp(lf)
    hi = lf.astype(BF16)
    lo = (lf - hi.astype(F32)).astype(BF16)
    two = jnp.dot(tri_ref[...], jnp.concatenate([hi, lo], axis=1), preferred_element_type=F32)
    cum = two[:, HEAD_DIM:] + two[:, :HEAD_DIM]
    cum_ref[...] = cum
    lvl = lvl_ref[...]
    bcast = lambda r: jnp.broadcast_to(cum_ref[r:r + 1, :], (8, HEAD_DIM))
    groups = lambda a: [a[8 * i:8 * i + 8] for i in range(a.shape[0] // 8)]

    if merged:
        blk = 1 << merged
        cache, pieces = {}, []
        for grp in range(ng):
            rank = (c - 1 - 8 * grp) if reverse else 8 * grp
            first = (rank // blk) * blk
            row = (c - 1 - first) if reverse else first
            pieces.append(cache.setdefault(row, bcast(row)))
        inside = jnp.logical_and(lvl >= 0, lvl <= merged)
        tiles = []
        for h, hs in enumerate(halves):
            d = cum_ref[hs, :] - jnp.concatenate(pieces[h * ng // 2:(h + 1) * ng // 2], axis=0)
            xq = (q_ref[0, hs, lanes].astype(F32) * jnp.exp(d)).astype(BF16)
            xk = (kk[hs] * jnp.exp(-d)).astype(BF16)
            tiles.append(jnp.where(inside, lax.dot_general(xq, xk, NT_DIMS, preferred_element_type=F32), 0.0))
    else:
        tiles = _gla_low_levels(q, kk, lf, cum, lvl, bcast, halves, reverse)

    tile_rows = [groups(t) for t in tiles]
    lvl_rows = groups(lvl)
    k_rows = groups(kk)
    for h in range(2):
        half_groups = range(h * ng // 2, (h + 1) * ng // 2)
        for level in range(max(4, merged + 1), GLA_LEVELS):
            brow = _boundary_rows(level, reverse)
            cache = {}
            g_rows, x_rows = [], []
            q_rows = groups(q_ref[0, halves[h], lanes].astype(F32))
            cum_rows = groups(cum_ref[halves[h], :])
            for grp in half_groups:
                cb = cache.setdefault(brow[8 * grp], bcast(brow[8 * grp]))
                later = _later_group(level, reverse, grp)
                local = grp - half_groups[0]
                g_rows.append(cum_rows[local] - cb if later else cb - cum_rows[local])
                x_rows.append(q_rows[local] if later else k_rows[grp])
            xe = jnp.concatenate(x_rows, axis=0) * jnp.exp(jnp.concatenate(g_rows, axis=0))
            xe_rows = groups(xe)
            later_local = [grp - half_groups[0] for grp in half_groups if _later_group(level, reverse, grp)]
            qc = jnp.concatenate([xe_rows[i] for i in later_local], axis=0).astype(BF16)
            s = lax.dot_general(qc, xe.astype(BF16), NT_DIMS, preferred_element_type=F32)
            for i, local in enumerate(later_local):
                tile_rows[h][local] = jnp.where(lvl_rows[local] == level, s[8 * i:8 * i + 8], tile_rows[h][local])

    early, late = (1, 0) if reverse else (0, 1)
    cb = cum_ref[_boundary_rows(GLA_LEVELS, reverse)[0]:_boundary_rows(GLA_LEVELS, reverse)[0] + 1, :]
    ql = (q_ref[0, halves[late], lanes].astype(F32) * jnp.exp(cum_ref[halves[late], :] - cb)).astype(BF16)
    ke = (kk[halves[early]] * jnp.exp(cb - cum_ref[halves[early], :])).astype(BF16)
    cross = lax.dot_general(ql, ke, NT_DIMS, preferred_element_type=F32)
    t_a, t_b = (jnp.concatenate(rows, axis=0) for rows in tile_rows)
    zero = jnp.zeros((hc, hc), F32)
    if reverse:
        scores = jnp.concatenate([jnp.concatenate([t_a, cross], axis=1), jnp.concatenate([zero, t_b], axis=1)], axis=0)
    else:
        scores = jnp.concatenate([jnp.concatenate([t_a, zero], axis=1), jnp.concatenate([cross, t_b], axis=1)], axis=0)

    st = st_ref[...]
    qe = (q_ref[0, :, lanes].astype(F32) * jnp.exp(cum_ref[...])).astype(BF16)
    o = jnp.dot(scores.astype(BF16), v, preferred_element_type=F32)
    o = o + lax.dot_general(qe, st.astype(BF16), NT_DIMS, preferred_element_type=F32)
    o_ref[0, :, lanes] = o.astype(o_ref.dtype)
    last_row = 0 if reverse else c - 1
    last = cum_ref[last_row:last_row + 1, :]
    ke_all = (kk * jnp.exp(last - cum_ref[...])).astype(BF16)
    st_ref[...] = st * jnp.exp(last) + lax.dot_general(v, ke_all, TN_DIMS, preferred_element_type=F32)


def _gla_kernel(qf_ref, vf_ref, lff_ref, qb_ref, vb_ref, lfb_ref, s0f_ref, s0b_ref,
                trif_ref, lvlf_ref, trib_ref, lvlb_ref,
                of_ref, ob_ref, sff_ref, sfb_ref, stf_ref, stb_ref, cumf_ref, cumb_ref, *, merged):
    j = pl.program_id(2)

    @pl.when(j == 0)
    def _():
        stf_ref[...] = s0f_ref[0]
        stb_ref[...] = s0b_ref[0]

    for k in range(stf_ref.shape[0]):
        lanes = slice(k * HEAD_DIM, (k + 1) * HEAD_DIM)
        _gla_chunk(qf_ref, vf_ref, lff_ref, lanes, stf_ref.at[k], cumf_ref.at[k], trif_ref, lvlf_ref, of_ref,
                   False, merged)
        _gla_chunk(qb_ref, vb_ref, lfb_ref, lanes, stb_ref.at[k], cumb_ref.at[k], trib_ref, lvlb_ref, ob_ref,
                   True, merged)

    @pl.when(j == pl.num_programs(2) - 1)
    def _():
        sff_ref[0] = stf_ref[...]
        sfb_ref[0] = stb_ref[...]


def _gla_bidir(q, v, lf_f, lf_b, s0f, s0b, lb):
    worst = (2 ** GLA_MERGED - 1) * jnp.max(-jnp.log(lb))
    run = lambda merged: (lambda *a: _gla_call(*a, merged=merged))
    return lax.cond(worst < GLA_MAX_EXPONENT, run(GLA_MERGED), run(0), q, v, lf_f, lf_b, s0f, s0b)


def _gla_call(q, v, lf_f, lf_b, s0f, s0b, *, merged):
    b_, t_, d_ = q.shape
    h_ = d_ // HEAD_DIM
    hp = math.gcd(GLA_HEADS_PER_STEP, h_)
    c = GLA_CHUNK
    n = t_ // c
    fwd = lambda b, h, j: (b, j, h)
    bwd = lambda b, h, j: (b, n - 1 - j, h)
    st = lambda b, h, j: (b, h, 0, 0)
    const = lambda b, h, j: (0, 0)
    blk = lambda im: pl.BlockSpec((1, c, hp * HEAD_DIM), im)
    st_spec = pl.BlockSpec((1, hp, HEAD_DIM, HEAD_DIM), st)
    cspecs = [pl.BlockSpec((c, c), const), pl.BlockSpec((c // 2, c // 2), const)]
    state = pltpu.VMEM((hp, HEAD_DIM, HEAD_DIM), F32)
    cum = pltpu.VMEM((hp, c, HEAD_DIM), F32)
    return pl.pallas_call(
        functools.partial(_gla_kernel, merged=merged),
        grid=(b_, h_ // hp, n),
        in_specs=[blk(fwd), blk(fwd), blk(fwd), blk(bwd), blk(bwd), blk(bwd), st_spec, st_spec] + cspecs + cspecs,
        out_specs=[blk(fwd), blk(bwd), st_spec, st_spec],
        out_shape=[jax.ShapeDtypeStruct((b_, t_, d_), BF16)] * 2
                  + [jax.ShapeDtypeStruct((b_, h_, HEAD_DIM, HEAD_DIM), F32)] * 2,
        scratch_shapes=[state, state, cum, cum],
        compiler_params=_params("arbitrary", "arbitrary", "arbitrary"),
        name="gla_merged" if merged else "gla_split",
    )(q, v, lf_f, q, v, lf_b, s0f, s0b, *_gla_consts(False), *_gla_consts(True))


def _mixer_epilogue(y, rows, x_ref, w_ref, gate_ref, gain_ref, shift_ref, scale_ref, wr_ref, x1_ref, h_ref, aff_ref):
    y = jnp.dot(y.astype(BF16), w_ref[...], preferred_element_type=F32)
    x1 = x_ref[0, rows] + gate_ref[0] * y
    x1_ref[0, rows] = x1
    hf = _modulate(x1, gain_ref[...], shift_ref[0], scale_ref[0])
    h_ref[0, rows] = hf.astype(BF16)
    logits = _dot_bf16x3(wr_ref[...], hf, NT_DIMS)
    e = jnp.exp(logits - jnp.max(logits, axis=0, keepdims=True))
    aff_ref[0, :, rows] = e / jnp.sum(e, axis=0, keepdims=True)


def _hgrn_out_kernel(of_ref, ob_ref, g_ref, x_ref, hnorm_ref, w_ref, *rest):
    hn = hnorm_ref[...]
    for rows in _sub_tiles(x_ref.shape[1]):
        o = of_ref[0, rows].astype(F32) + ob_ref[0, rows].astype(F32)
        heads = []
        for h in range(o.shape[-1] // HEAD_DIM):
            oh = o[:, h * HEAD_DIM:(h + 1) * HEAD_DIM]
            heads.append(oh * lax.rsqrt(jnp.mean(oh * oh, axis=-1, keepdims=True) + EPS) * hn)
        y = jnp.concatenate(heads, axis=-1) * _silu(g_ref[0, rows].astype(F32))
        _mixer_epilogue(y, rows, x_ref, w_ref, *rest)


def _conv_mixer_kernel(prev_ref, next_ref, x_ref, shift1_ref, scale1_ref, gain1_ref, win_ref, wc_ref, w_ref, *rest):
    i = pl.program_id(1)
    tm, d_ = x_ref.shape[1], x_ref.shape[2]
    subs = _sub_tiles(tm)
    halo = jnp.concatenate([prev_ref[0], next_ref[0]], axis=0)
    b_gate, cu = [], []
    for k, rows in enumerate(subs):
        xs = x_ref[0, rows]
        if k == 0:
            xs = jnp.concatenate([xs, halo], axis=0)
        h = _modulate(xs, gain1_ref[...], shift1_ref[0], scale1_ref[0]).astype(BF16)
        part = lambda p, hh: jnp.dot(hh, win_ref[:, p * d_:(p + 1) * d_], preferred_element_type=F32)
        n = rows.stop - rows.start
        b_gate.append(part(0, h[:n]))
        cu.append(part(1, h) * part(2, h))
    cu_halo = cu[0][subs[0].stop - subs[0].start:]
    cu = jnp.concatenate([cu[0][:subs[0].stop - subs[0].start]] + cu[1:], axis=0)
    rid = lax.broadcasted_iota(jnp.int32, cu.shape, 0)
    before = jnp.where(i == 0, 0.0, cu_halo[7:8])
    after = jnp.where(i == pl.num_programs(1) - 1, 0.0, cu_halo[8:9])
    left = jnp.where(rid == 0, before, pltpu.roll(cu, 1, 0))
    right = jnp.where(rid == tm - 1, after, pltpu.roll(cu, tm - 1, 0))
    wc = wc_ref[...]
    conv = left * wc[0:1] + cu * wc[1:2] + right * wc[2:3]
    for k, rows in enumerate(subs):
        _mixer_epilogue(b_gate[k] * conv[rows], rows, x_ref, w_ref, *rest)


def _mixer_out(kernel, name, row_args, halo_args, x, sample_args, small_args, gate, gain, shift, scale, w_router_t):
    b_, t_, d_ = x.shape
    e_ = w_router_t.shape[0]
    tm = min(MIXER_ROW_TILE, t_)
    row = pl.BlockSpec((1, tm, d_), lambda b, i: (b, i, 0))
    per_sample = pl.BlockSpec((1, 1, d_), lambda b, i: (b, 0, 0))
    whole = lambda a: pl.BlockSpec(a.shape, lambda b, i: (0,) * a.ndim)
    n8 = t_ // 8
    halo_specs = [pl.BlockSpec((1, 8, d_), lambda b, i: (b, jnp.maximum(i * (tm // 8) - 1, 0), 0)),
                  pl.BlockSpec((1, 8, d_), lambda b, i: (b, jnp.minimum((i + 1) * (tm // 8), n8 - 1), 0))]
    return pl.pallas_call(
        kernel,
        grid=(b_, t_ // tm),
        in_specs=[row] * len(row_args) + halo_specs[:len(halo_args)] + [row] + [per_sample] * len(sample_args)
                 + [whole(a) for a in small_args] + [per_sample, whole(gain), per_sample, per_sample, whole(w_router_t)],
        out_specs=[row, row, pl.BlockSpec((1, e_, tm), lambda b, i: (b, 0, i))],
        out_shape=[jax.ShapeDtypeStruct((b_, t_, d_), F32), jax.ShapeDtypeStruct((b_, t_, d_), BF16),
                   jax.ShapeDtypeStruct((b_, e_, t_), F32)],
        compiler_params=_params("arbitrary", "arbitrary", vmem=V7X_VMEM_LIMIT),
        name=name,
    )(*row_args, *halo_args, x, *sample_args, *small_args, gate, gain, shift, scale, w_router_t)


def _route_kernel(aff_ref, tri_ref, blockind_ref, slot_ref, base_ref, dense_ref, *, cap, n_experts):
    aff = aff_ref[...]
    e_, t_ = aff.shape

    def as_float(word):
        return pltpu.bitcast(word, F32)

    def count_ge(th):
        return jnp.sum(jnp.where(aff >= th, 1.0, 0.0), axis=1, keepdims=True)

    def search(_, carry):
        lo, hi = carry
        mid = lo + ((hi - lo + 1) >> 1)
        ok = count_ge(as_float(mid)) >= cap
        return jnp.where(ok, mid, lo), jnp.where(ok, hi, mid - 1)

    lo0 = jnp.zeros((e_, 1), jnp.int32)
    hi0 = jnp.full((e_, 1), 0x7F7FFFFF, jnp.int32)
    kth, _ = lax.fori_loop(0, 32, search, (lo0, hi0))
    above = aff >= as_float(kth + 1)
    tied = jnp.logical_and(aff >= as_float(kth), jnp.logical_not(above))
    need = cap - jnp.sum(jnp.where(above, 1.0, 0.0), axis=1, keepdims=True)
    tri = tri_ref[...]
    tb = tri.shape[0]
    carry_t = jnp.zeros((e_, 1), F32)
    carry_s = jnp.zeros((e_, 1), F32)
    sel_blocks = []
    for j in range(t_ // tb):
        cols = slice(j * tb, (j + 1) * tb)
        tied_j = tied[:, cols]
        ct = jnp.dot(jnp.where(tied_j, 1.0, 0.0).astype(BF16), tri, preferred_element_type=F32) + carry_t
        carry_t = ct[:, tb - 1:tb]
        sel_j = jnp.where(above[:, cols], 1.0, jnp.where(tied_j & (ct <= need), 1.0, 0.0))
        cs = jnp.dot(sel_j.astype(BF16), tri, preferred_element_type=F32) + carry_s
        carry_s = cs[:, tb - 1:tb]
        slot_ref[:, cols] = jnp.where(sel_j > 0.0, cs - 1.0, -1.0).astype(jnp.int32)
        sel_blocks.append(sel_j.astype(BF16))
    sel = jnp.concatenate(sel_blocks, axis=1)
    counts = jnp.dot(sel, blockind_ref[...], preferred_element_type=F32)
    base, end = counts[:, :128], counts[:, 128:]
    base_ref[...] = base.astype(jnp.int32)
    span = end - jnp.floor(base * (1.0 / SLOT_ALIGN)) * SLOT_ALIGN
    for b in range(e_ // n_experts):
        widest = jnp.max(span[b * n_experts:(b + 1) * n_experts], axis=0, keepdims=True)
        dense_ref[b] = sum(jnp.where(widest <= w, 1, 0) for w in DENSE_WINDOWS).astype(jnp.int32)


def _route(aff):
    b_, e_, t_ = aff.shape
    cap = EC_CAPACITY_FACTOR * t_ // e_
    tb = TOKEN_BLOCK
    nb = t_ // tb
    tri = jnp.asarray(np.triu(np.ones((tb, tb), np.float32)), BF16)
    tok, col = np.arange(t_)[:, None], np.arange(128)[None, :]
    blockind = np.concatenate([(tok < col * tb) & (col <= nb), (tok < (col + 1) * tb) & (col < nb)], axis=1)
    slot, base, dense = pl.pallas_call(
        functools.partial(_route_kernel, cap=cap, n_experts=e_),
        out_shape=[jax.ShapeDtypeStruct((b_ * e_, t_), jnp.int32), jax.ShapeDtypeStruct((b_ * e_, 128), jnp.int32),
                   jax.ShapeDtypeStruct((b_, 1, 128), jnp.int32)],
        name="route",
    )(aff.reshape(b_ * e_, t_), tri, jnp.asarray(blockind.astype(np.float32), BF16))
    return slot.reshape(b_, e_, t_), base[:, :nb + 1].reshape(-1), dense[:, 0, :nb].reshape(-1)


def _window_plan(tbl_ref, b, e, tb, ne, nb):
    idx = (b * ne + e) * (nb + 1) + tb
    base, end = tbl_ref[idx], tbl_ref[idx + 1]
    start = _align_down(base)
    n_win = jnp.where(end > base, (end - start + SLOT_WINDOW - 1) >> (SLOT_WINDOW.bit_length() - 1), 0)
    return start, n_win


def _align_down(slot):
    shift = SLOT_ALIGN.bit_length() - 1
    return (slot >> shift) << shift


def _dense_start(tbl_ref, b, e, tb, ne, nb, cap, window):
    base = tbl_ref[(b * ne + e) * (nb + 1) + tb]
    return pl.multiple_of(jnp.minimum(_align_down(base), cap - window), SLOT_ALIGN)


def _gather_kernel(tbl_ref, dense_ref, h_ref, slot_ref, xg_ref, *, ne, nb, cap):
    b, step = pl.program_id(0), pl.program_id(2)
    n_tok = TOKEN_BLOCK

    @pl.when(step == 0)
    def _():
        xg_ref[...] = jnp.zeros(xg_ref.shape, xg_ref.dtype)

    def dense_block(sub, window):
        tb = step * BLOCKS_PER_STEP + sub
        toks = slice(sub * n_tok, (sub + 1) * n_tok)
        starts = [_dense_start(tbl_ref, b, e, tb, ne, nb, cap, window) for e in range(ne)]
        ids = lax.broadcasted_iota(jnp.int32, (window, n_tok), 0)
        onehot = jnp.concatenate(
            [jnp.where(ids == slot_ref[0, e:e + 1, toks] - starts[e], 1.0, 0.0).astype(BF16) for e in range(ne)],
            axis=0)
        rows = jnp.dot(onehot, h_ref[0, toks, :], preferred_element_type=F32).astype(BF16)
        for e in range(ne):
            win = xg_ref.at[0, e, pl.ds(starts[e], window), :]
            win[...] = win[...] + rows[e * window:(e + 1) * window]

    def windowed_block(sub):
        tb = step * BLOCKS_PER_STEP + sub
        toks = slice(sub * n_tok, (sub + 1) * n_tok)
        h = h_ref[0, toks, :]
        ids0 = lax.broadcasted_iota(jnp.int32, (SLOT_WINDOW, n_tok), 0)
        for e in range(ne):
            srow = slot_ref[0, e:e + 1, toks]
            start, n_win = _window_plan(tbl_ref, b, e, tb, ne, nb)

            def body(k, carry, e=e, srow=srow, start=start):
                lo = start + k * SLOT_WINDOW
                w0 = pl.multiple_of(jnp.minimum(lo, cap - SLOT_WINDOW), SLOT_ALIGN)
                ids = w0 + ids0
                onehot = jnp.where(ids >= lo, jnp.where(ids == srow, 1.0, 0.0), 0.0).astype(BF16)
                rows = jnp.dot(onehot, h, preferred_element_type=F32)
                win = xg_ref.at[0, e, pl.ds(w0, SLOT_WINDOW), :]
                win[...] = win[...] + rows.astype(BF16)
                return carry

            lax.fori_loop(0, n_win, body, 0)

    _per_block_dispatch(dense_ref, b * nb + step * BLOCKS_PER_STEP, BLOCKS_PER_STEP, dense_block, windowed_block)


def _per_block_dispatch(dense_ref, first, n_blocks, dense_block, windowed_block):
    fits = [dense_ref[first + sub] for sub in range(n_blocks)]
    common = functools.reduce(jnp.minimum, fits)
    n_tiers = len(DENSE_WINDOWS)
    for k, window in enumerate(DENSE_WINDOWS):
        chosen = (common >= n_tiers) if k == 0 else (common == n_tiers - k)

        @pl.when(chosen)
        def _(window=window):
            for sub in range(n_blocks):
                dense_block(sub, window)

    @pl.when(common == 0)
    def _():
        for sub in range(n_blocks):
            pl.when(fits[sub] > 0)(functools.partial(dense_block, sub, DENSE_WINDOWS[-1]))
            pl.when(fits[sub] == 0)(functools.partial(windowed_block, sub))


def _gather(h, slot, tbl, dense):
    b_, t_, d_ = h.shape
    e_ = slot.shape[1]
    cap = EC_CAPACITY_FACTOR * t_ // e_
    tb = TOKEN_BLOCK * BLOCKS_PER_STEP
    nb = t_ // TOKEN_BLOCK
    dh = d_ // 2
    return pl.pallas_call(
        functools.partial(_gather_kernel, ne=e_, nb=nb, cap=cap),
        grid_spec=pltpu.PrefetchScalarGridSpec(
            num_scalar_prefetch=2,
            grid=(b_, 2, t_ // tb),
            in_specs=[pl.BlockSpec((1, tb, dh), lambda b, c, i, *_: (b, i, c)),
                      pl.BlockSpec((1, e_, tb), lambda b, c, i, *_: (b, 0, i))],
            out_specs=pl.BlockSpec((1, e_, cap, dh), lambda b, c, i, *_: (b, 0, 0, c)),
        ),
        out_shape=jax.ShapeDtypeStruct((b_, e_, cap, d_), BF16),
        compiler_params=_params("arbitrary", "arbitrary", "arbitrary", vmem=V7X_VMEM_LIMIT),
        name="moe_gather",
    )(tbl, dense, h, slot)


def _expert_kernel(xg_ref, wg_ref, wu_ref, wd_ref, y_ref, acc_ref):
    f = pl.program_id(1)
    n_b, _, cap, _ = xg_ref.shape
    wg = wg_ref[0, 0].astype(BF16)
    wu = wu_ref[0, 0].astype(BF16)
    wd = wd_ref[0, 0].astype(BF16)

    @pl.when(jnp.logical_and(pl.program_id(0) == 0, f == 0))
    def _():
        acc_ref[...] = jnp.zeros(acc_ref.shape, F32)

    for b in range(n_b):
        for r in range(cap // FFN_ROWS):
            rows = pl.ds(r * FFN_ROWS, FFN_ROWS)
            acc_rows = pl.ds((b * cap) + r * FFN_ROWS, FFN_ROWS)
            xr = xg_ref[b, 0, rows, :]
            a = jnp.dot(xr, wg, preferred_element_type=F32)
            u = jnp.dot(xr, wu, preferred_element_type=F32)
            part = jnp.dot((_silu(a) * u).astype(BF16), wd, preferred_element_type=F32)
            total = jnp.where(f == 0, 0.0, acc_ref[acc_rows, :]) + part
            acc_ref[acc_rows, :] = total
            y_ref[b, 0, rows, :] = total.astype(BF16)


def _experts(xg, layer, w_gate, w_up, w_down):
    b_, e_, cap, d_ = xg.shape
    f_ = w_gate.shape[-1]
    ft = min(FFN_TILE, f_)
    return pl.pallas_call(
        _expert_kernel,
        grid=(e_, f_ // ft),
        in_specs=[pl.BlockSpec((b_, 1, cap, d_), lambda e, f: (0, e, 0, 0)),
                  pl.BlockSpec((1, 1, d_, ft), lambda e, f: (layer, e, 0, f)),
                  pl.BlockSpec((1, 1, d_, ft), lambda e, f: (layer, e, 0, f)),
                  pl.BlockSpec((1, 1, ft, d_), lambda e, f: (layer, e, f, 0))],
        out_specs=pl.BlockSpec((b_, 1, cap, d_), lambda e, f: (0, e, 0, 0)),
        out_shape=jax.ShapeDtypeStruct((b_, e_, cap, d_), BF16),
        scratch_shapes=[pltpu.VMEM((b_ * cap, d_), F32)],
        compiler_params=_params("arbitrary", "arbitrary", vmem=V7X_VMEM_LIMIT),
        name="moe_experts",
    )(xg, w_gate, w_up, w_down)


def _combine_kernel(tbl_ref, dense_ref, y_ref, slot_ref, aff_ref, x_ref, gate_ref, *rest, ne, nb, cap, final):
    if final:
        gain_ref, out_ref, acc_ref = rest
    else:
        out_ref, acc_ref = rest
    b, step = pl.program_id(0), pl.program_id(1)
    n_tok = TOKEN_BLOCK

    def finish(toks, moe):
        out = x_ref[0, toks, :] + gate_ref[0] * moe
        if final:
            out = out * lax.rsqrt(jnp.mean(out * out, axis=-1, keepdims=True) + EPS) * gain_ref[...]
        out_ref[0, toks, :] = out

    def dense_block(sub, window):
        tb = step * COMBINE_BLOCKS + sub
        toks = slice(sub * n_tok, (sub + 1) * n_tok)
        starts = [_dense_start(tbl_ref, b, e, tb, ne, nb, cap, window) for e in range(ne)]
        ids = lax.broadcasted_iota(jnp.int32, (window, n_tok), 0)
        weights = jnp.concatenate(
            [jnp.where(ids == slot_ref[0, e:e + 1, toks] - starts[e], aff_ref[0, e:e + 1, toks], 0.0).astype(BF16)
             for e in range(ne)], axis=0)
        yw = jnp.concatenate([y_ref[0, e, pl.ds(starts[e], window), :] for e in range(ne)], axis=0)
        finish(toks, lax.dot_general(weights, yw, TN_DIMS, preferred_element_type=F32))

    def windowed_block(sub):
        tb = step * COMBINE_BLOCKS + sub
        toks = slice(sub * n_tok, (sub + 1) * n_tok)
        ids0 = lax.broadcasted_iota(jnp.int32, (SLOT_WINDOW, n_tok), 0)
        acc_ref[...] = jnp.zeros(acc_ref.shape, F32)
        for e in range(ne):
            srow = slot_ref[0, e:e + 1, toks]
            grow = aff_ref[0, e:e + 1, toks]
            start, n_win = _window_plan(tbl_ref, b, e, tb, ne, nb)

            def body(k, carry, e=e, srow=srow, grow=grow, start=start):
                lo = start + k * SLOT_WINDOW
                w0 = pl.multiple_of(jnp.minimum(lo, cap - SLOT_WINDOW), SLOT_ALIGN)
                ids = w0 + ids0
                weights = jnp.where(ids >= lo, jnp.where(ids == srow, grow, 0.0), 0.0).astype(BF16)
                yw = y_ref[0, e, pl.ds(w0, SLOT_WINDOW), :]
                acc_ref[...] = acc_ref[...] + lax.dot_general(weights, yw, TN_DIMS, preferred_element_type=F32)
                return carry

            lax.fori_loop(0, n_win, body, 0)
        finish(toks, acc_ref[...])

    _per_block_dispatch(dense_ref, b * nb + step * COMBINE_BLOCKS, COMBINE_BLOCKS, dense_block, windowed_block)


def _combine(y, slot, aff, tbl, dense, x, gate, final_gain):
    b_, t_, d_ = x.shape
    e_, cap = y.shape[1], y.shape[2]
    tb = TOKEN_BLOCK * COMBINE_BLOCKS
    nb = t_ // TOKEN_BLOCK
    final = final_gain is not None
    route_spec = pl.BlockSpec((1, e_, tb), lambda b, i, *_: (b, 0, i))
    row = pl.BlockSpec((1, tb, d_), lambda b, i, *_: (b, i, 0))
    in_specs = [pl.BlockSpec((1, e_, cap, d_), lambda b, i, *_: (b, 0, 0, 0), pipeline_mode=pl.Buffered(1)),
                route_spec, route_spec, row, pl.BlockSpec((1, 1, d_), lambda b, i, *_: (b, 0, 0))]
    args = [tbl, dense, y, slot, aff, x, gate]
    if final:
        in_specs.append(pl.BlockSpec((1, d_), lambda b, i, *_: (0, 0)))
        args.append(final_gain)
    return pl.pallas_call(
        functools.partial(_combine_kernel, ne=e_, nb=nb, cap=cap, final=final),
        grid_spec=pltpu.PrefetchScalarGridSpec(
            num_scalar_prefetch=2,
            grid=(b_, t_ // tb),
            in_specs=in_specs,
            out_specs=row,
            scratch_shapes=[pltpu.VMEM((TOKEN_BLOCK, d_), F32)],
        ),
        out_shape=jax.ShapeDtypeStruct((b_, t_, d_), F32),
        compiler_params=_params("arbitrary", "arbitrary", vmem=V7X_VMEM_LIMIT),
        name="moe_combine_final" if final else "moe_combine",
    )(*args)


def _moe(x, h, aff, gate, layer, w_gate, w_up, w_down, final_gain=None):
    slot, tbl, dense = _route(aff)
    xg = _gather(h, slot, tbl, dense)
    y = _experts(xg, layer, w_gate, w_up, w_down)
    return _combine(y, slot, aff, tbl, dense, x, gate, final_gain)


def kernel(x, c, ctx, c_ctx, ada_w, ada_b, norm_mix, norm_ffn, norm_final, hg_w_in, hg_lb_logits, hg_norm, hg_w_out, sc_w_in, sc_conv, sc_w_out, moe_router, moe_w_gate, moe_w_up, moe_w_down):
    b_, t_, d_ = x.shape
    depth = ada_w.shape[0]
    n_ada = ada_w.shape[-1] // d_
    n_heads = d_ // HEAD_DIM
    n_experts = moe_router.shape[-1]
    cap = EC_CAPACITY_FACTOR * t_ // n_experts
    assert depth == 2 and n_ada == 6 and b_ + 1 <= 8
    assert t_ % GLA_CHUNK == 0 and ctx.shape[1] % GLA_CHUNK == 0 and IN_PROJ_SUB_ROWS % GRID_W == 0
    assert d_ % (2 * HEAD_DIM) == 0 and t_ % min(ROW_TILE, t_) == 0 and t_ % min(MIXER_ROW_TILE, t_) == 0
    assert t_ % (TOKEN_BLOCK * BLOCKS_PER_STEP) == 0 and t_ % (TOKEN_BLOCK * COMBINE_BLOCKS) == 0
    assert cap % SLOT_WINDOW == 0 and cap >= max(DENSE_WINDOWS) and cap % min(FFN_ROWS, cap) == 0
    assert moe_w_gate.shape[-1] % min(FFN_TILE, moe_w_gate.shape[-1]) == 0

    cond = jnp.concatenate([c, c_ctx[None], jnp.zeros((8 - b_ - 1, d_), F32)], axis=0)
    mod = _ada_vectors(cond, ada_w, ada_b, n_ada)
    vec = lambda i, j: mod[i, j, :b_][:, None, :]
    cvec = lambda i, j: mod[i, j, b_][None, None, :]
    rowtab, coltab, lower = _tables(hg_lb_logits, t_)
    row_of = lambda a, i: a[i][None, :]
    router_t = lambda i: jnp.swapaxes(moe_router[i], 0, 1)

    w_in = hg_w_in[0].astype(BF16)
    lb0 = row_of(lower, 0)
    gain0 = row_of(norm_mix, 0)
    qc, vc, lfc_f, lfc_b = _hgrn_in(ctx, None, gain0, cvec(0, 0), cvec(0, 1), lb0, w_in, with_gate=False)
    zeros = jnp.zeros((b_, n_heads, HEAD_DIM, HEAD_DIM), F32)
    _, _, s_f, s_b = _gla_bidir(qc, vc, lfc_f, lfc_b, zeros, zeros, lb0)
    x0, q, v, lf_f, lf_b, g = _hgrn_in(x, (rowtab, coltab), gain0, vec(0, 0), vec(0, 1), lb0, w_in, with_gate=True)
    o_f, o_b, _, _ = _gla_bidir(q, v, lf_f, lf_b, s_f, s_b, lb0)
    x1, h, aff = _mixer_out(_hgrn_out_kernel, "hgrn_out", [o_f, o_b, g], [], x0, [],
                            [row_of(hg_norm, 0), hg_w_out[0].astype(BF16)],
                            vec(0, 2), row_of(norm_ffn, 0), vec(0, 3), vec(0, 4), router_t(0))
    x2 = _moe(x1, h, aff, vec(0, 5), 0, moe_w_gate, moe_w_up, moe_w_down)

    x3, h, aff = _mixer_out(_conv_mixer_kernel, "conv_mixer", [], [x2, x2], x2, [vec(1, 0), vec(1, 1)],
                            [row_of(norm_mix, 1), sc_w_in[0].astype(BF16), sc_conv[0], sc_w_out[0].astype(BF16)],
                            vec(1, 2), row_of(norm_ffn, 1), vec(1, 3), vec(1, 4), router_t(1))
    return _moe(x3, h, aff, vec(1, 5), 1, moe_w_gate, moe_w_up, moe_w_down, final_gain=norm_final[None, :])
```
